```python
import math
import jax, jax.numpy as jnp
from jax import lax
import numpy as np

D_MODEL = 1024
BATCH = 16
SEQ = 256
DEPTH = 2
DEC_BATCH = 2
DEC_SEQ = 4096
PAST_LEN = 256

GRID_W = 64
Q_BLOCK = 128
HEAD_DIM = 64
ROPE_THETA = 10000.0
A_HEADS = 4
A_KV_HEADS = 2
A_GROUP = A_HEADS // A_KV_HEADS
A_WIDTH = A_HEADS * HEAD_DIM
B_HEADS = 4
B_QK_DIM = 64
B_V_DIM = 2 * B_QK_DIM
B_WIDTH = B_HEADS * B_V_DIM
C_GROUPS = 4
C_GROUP_DIM = 64
C_WIDTH = C_GROUPS * C_GROUP_DIM
MIX_WIDTH = A_WIDTH + B_WIDTH + C_WIDTH
IN_SIZES = [A_HEADS * HEAD_DIM, A_KV_HEADS * HEAD_DIM, A_KV_HEADS * HEAD_DIM,
            B_HEADS * 2 * B_QK_DIM, B_HEADS * 2 * B_QK_DIM, B_HEADS * B_V_DIM, C_WIDTH]
IN_WIDTH = sum(IN_SIZES)
IN_SPLITS = [int(s) for s in np.cumsum(IN_SIZES)[:-1]]
FFN_DIM = 2816
N_MOD = 9
ALPHA = (2.0 * DEPTH) ** 0.25
BETA = (8.0 * DEPTH) ** -0.25
LN_EPS = 1e-5
RMS_EPS = 1e-6

kernel_name = "hybrid_diffusion_prefix_trunk_step"


def layer_norm(x, g, b):
    xf = x.astype(jnp.float32)
    mu = jnp.mean(xf, -1, keepdims=True)
    xc = xf - mu
    var = jnp.mean(xc * xc, -1, keepdims=True)
    return (xc * lax.rsqrt(var + LN_EPS) * g + b).astype(x.dtype)


def rms_norm(x, g):
    xf = x.astype(jnp.float32)
    y = xf * lax.rsqrt(jnp.mean(xf * xf, -1, keepdims=True) + RMS_EPS)
    return (y * g).astype(x.dtype)


def rope_tables(length):
    rows = length // GRID_W
    row = jnp.repeat(jnp.arange(rows), GRID_W).astype(jnp.float32)
    col = jnp.tile(jnp.arange(GRID_W), rows).astype(jnp.float32)
    half = HEAD_DIM // 2
    inv = 1.0 / (ROPE_THETA ** (jnp.arange(0, half, 2, dtype=jnp.float32) / half))
    ar = row[:, None] * inv
    ac = col[:, None] * inv
    cos = jnp.concatenate([jnp.cos(ar), jnp.cos(ar), jnp.cos(ac), jnp.cos(ac)], -1)
    sin = jnp.concatenate([jnp.sin(ar), jnp.sin(ar), jnp.sin(ac), jnp.sin(ac)], -1)
    return cos, sin


def _rot_half(y):
    q = y.shape[-1] // 2
    return jnp.concatenate([-y[..., q:], y[..., :q]], -1)


def apply_rope(x, cos, sin):
    half = HEAD_DIM // 2
    rot = jnp.concatenate([_rot_half(x[..., :half]), _rot_half(x[..., half:])], -1)
    out = x.astype(jnp.float32) * cos[:, None, :] + rot.astype(jnp.float32) * sin[:, None, :]
    return out.astype(x.dtype)


def sweep_queries(fn, q):
    b, length = q.shape[0], q.shape[1]
    nb = length // Q_BLOCK
    qb = jnp.moveaxis(q.reshape((b, nb, Q_BLOCK) + q.shape[2:]), 1, 0)
    out = jnp.moveaxis(lax.map(fn, qb), 0, 1)
    return out.reshape((b, length) + out.shape[3:])


def gqa_attention(q, k, v):
    scale = HEAD_DIM ** -0.5
    def blk(qb):
        s = jnp.einsum('bqhgd,bkhd->bhgqk', qb, k).astype(jnp.float32) * scale
        p = jax.nn.softmax(s, axis=-1)
        return jnp.einsum('bhgqk,bkhd->bqhgd', p.astype(v.dtype), v)
    return sweep_queries(blk, q)


def diff_attention(q, k, v, lam):
    scale = B_QK_DIM ** -0.5
    def blk(qb):
        s = jnp.einsum('bqhcd,bkhcd->bhcqk', qb, k).astype(jnp.float32) * scale
        p = jax.nn.softmax(s, axis=-1)
        w = p[:, :, 0] - lam * p[:, :, 1]
        return jnp.einsum('bhqk,bkhe->bqhe', w.astype(v.dtype), v)
    return sweep_queries(blk, q)


def fourier_mix(u):
    b, length, _ = u.shape
    ug = u.reshape(b, length, C_GROUPS, C_GROUP_DIM).astype(jnp.float32)
    f = jnp.fft.fft2(ug, axes=(1, 3), norm='ortho').real
    return f.reshape(b, length, C_WIDTH).astype(u.dtype)


def token_mix(h, w_in, g_qa, g_ka, lam, lam_init, g_subln, w_fourier, w_out, ctx):
    b, length, _ = h.shape
    proj = h @ w_in
    qa, ka, va, qb, kb, vb, uc = jnp.split(proj, IN_SPLITS, axis=-1)
    qa = rms_norm(qa.reshape(b, length, A_HEADS, HEAD_DIM), g_qa)
    ka = rms_norm(ka.reshape(b, length, A_KV_HEADS, HEAD_DIM), g_ka)
    va = va.reshape(b, length, A_KV_HEADS, HEAD_DIM)
    qb = qb.reshape(b, length, B_HEADS * 2, B_QK_DIM)
    kb = kb.reshape(b, length, B_HEADS * 2, B_QK_DIM)
    vb = vb.reshape(b, length, B_HEADS, B_V_DIM)
    if ctx is None:
        new = (ka, va, kb.reshape(b, length, B_HEADS, 2 * B_QK_DIM), vb)
        ka_all, va_all = ka, va
        kb_all = kb.reshape(b, length, B_HEADS, 2, B_QK_DIM)
        vb_all = vb
    else:
        cos, sin = rope_tables(length)
        qa = apply_rope(qa, cos, sin)
        ka = apply_rope(ka, cos, sin)
        qb = apply_rope(qb, cos, sin)
        kb = apply_rope(kb, cos, sin)
        c_ka, c_va, c_kb, c_vb = ctx
        lc = c_ka.shape[1]
        ka_all = jnp.concatenate([c_ka, ka], 1)
        va_all = jnp.concatenate([c_va, va], 1)
        kb_all = jnp.concatenate([c_kb.reshape(b, lc, B_HEADS, 2, B_QK_DIM),
                                  kb.reshape(b, length, B_HEADS, 2, B_QK_DIM)], 1)
        vb_all = jnp.concatenate([c_vb, vb], 1)
        new = None
    o_a = gqa_attention(qa.reshape(b, length, A_KV_HEADS, A_GROUP, HEAD_DIM), ka_all, va_all)
    o_a = o_a.reshape(b, length, A_WIDTH)
    o_b = diff_attention(qb.reshape(b, length, B_HEADS, 2, B_QK_DIM), kb_all, vb_all, lam)
    o_b = (rms_norm(o_b, g_subln) * (1.0 - lam_init)).reshape(b, length, B_WIDTH)
    o_c = fourier_mix(uc) @ w_fourier
    out = jnp.concatenate([o_a, o_b, o_c], -1) @ w_out
    return out, new


def swiglu(h, w_gu, w_down):
    g, u = jnp.split(h @ w_gu, 2, axis=-1)
    return (jax.nn.silu(g) * u) @ w_down


def trunk(x, cvec, caches, w_mod, b_mod, w_in, g_qa, g_ka, lam_q1, lam_k1, lam_q2, lam_k2,
          g_subln, w_fourier, w_out, w_ffn1_gu, w_ffn1_down, w_ffn2_gu, w_ffn2_down, ln_g, ln_b):
    new_list = []
    for l in range(DEPTH):
        mod = jax.nn.silu(cvec) @ w_mod[l] + b_mod[l]
        sh1, sc1, gt1, sh2, sc2, gt2, sh3, sc3, gt3 = [m[:, None, :] for m in jnp.split(mod, N_MOD, -1)]
        lam_init = 0.8 - 0.6 * math.exp(-0.3 * l)
        lam = (jnp.exp(jnp.sum(lam_q1[l].astype(jnp.float32) * lam_k1[l].astype(jnp.float32)))
               - jnp.exp(jnp.sum(lam_q2[l].astype(jnp.float32) * lam_k2[l].astype(jnp.float32)))
               + lam_init)
        f1 = swiglu(x * (1 + sc1) + sh1, w_ffn1_gu[l], w_ffn1_down[l])
        x = layer_norm(ALPHA * x + 0.5 * gt1 * f1, ln_g[l, 0], ln_b[l, 0])
        ctx = None if caches is None else tuple(cch[:, l] for cch in caches)
        mo, new = token_mix(x * (1 + sc2) + sh2, w_in[l], g_qa[l], g_ka[l], lam, lam_init,
                            g_subln[l], w_fourier[l], w_out[l], ctx)
        x = layer_norm(ALPHA * x + gt2 * mo, ln_g[l, 1], ln_b[l, 1])
        f2 = swiglu(x * (1 + sc3) + sh3, w_ffn2_gu[l], w_ffn2_down[l])
        x = layer_norm(ALPHA * x + 0.5 * gt3 * f2, ln_g[l, 2], ln_b[l, 2])
        if new is not None:
            new_list.append(new)
    return x, new_list


def setup_inputs(seed: int = 0) -> dict:
    key = jax.random.key(seed)
    ks = jax.random.split(key, 32)
    f32 = jnp.float32
    nrm = lambda k, s: jax.random.normal(k, s, f32)
    d = D_MODEL
    return {
        "x_prompt": nrm(ks[0], (BATCH, SEQ, d)),
        "x_sample": nrm(ks[1], (DEC_BATCH, DEC_SEQ, d)),
        "cache_a_k": nrm(ks[2], (DEC_BATCH, DEPTH, PAST_LEN, A_KV_HEADS, HEAD_DIM)),
        "cache_a_v": nrm(ks[3], (DEC_BATCH, DEPTH, PAST_LEN, A_KV_HEADS, HEAD_DIM)),
        "cache_b_k": nrm(ks[4], (DEC_BATCH, DEPTH, PAST_LEN, B_HEADS, 2 * B_QK_DIM)),
        "cache_b_v": nrm(ks[5], (DEC_BATCH, DEPTH, PAST_LEN, B_HEADS, B_V_DIM)),
        "c": nrm(ks[6], (DEC_BATCH, d)),
        "c_ctx": nrm(ks[7], (d,)),
        "w_mod": nrm(ks[8], (DEPTH, d, N_MOD * d)) * (0.5 * d ** -0.5),
        "b_mod": nrm(ks[9], (DEPTH, N_MOD * d)) * 0.01,
        "w_in": nrm(ks[10], (DEPTH, d, IN_WIDTH)) * d ** -0.5,
        "g_qa": 1.0 + 0.02 * nrm(ks[11], (DEPTH, HEAD_DIM)),
        "g_ka": 1.0 + 0.02 * nrm(ks[12], (DEPTH, HEAD_DIM)),
        "lam_q1": 0.1 * nrm(ks[13], (DEPTH, B_QK_DIM)),
        "lam_k1": 0.1 * nrm(ks[14], (DEPTH, B_QK_DIM)),
        "lam_q2": 0.1 * nrm(ks[15], (DEPTH, B_QK_DIM)),
        "lam_k2": 0.1 * nrm(ks[16], (DEPTH, B_QK_DIM)),
        "g_subln": 1.0 + 0.02 * nrm(ks[17], (DEPTH, B_V_DIM)),
        "w_fourier": nrm(ks[18], (DEPTH, C_WIDTH, C_WIDTH)) * C_WIDTH ** -0.5,
        "w_out": nrm(ks[19], (DEPTH, MIX_WIDTH, d)) * (BETA * MIX_WIDTH ** -0.5),
        "w_ffn1_gu": nrm(ks[20], (DEPTH, d, 2 * FFN_DIM)) * d ** -0.5,
        "w_ffn1_down": nrm(ks[21], (DEPTH, FFN_DIM, d)) * (BETA * FFN_DIM ** -0.5),
        "w_ffn2_gu": nrm(ks[22], (DEPTH, d, 2 * FFN_DIM)) * d ** -0.5,
        "w_ffn2_down": nrm(ks[23], (DEPTH, FFN_DIM, d)) * (BETA * FFN_DIM ** -0.5),
        "ln_g": 1.0 + 0.02 * nrm(ks[24], (DEPTH, 3, d)),
        "ln_b": 0.02 * nrm(ks[25], (DEPTH, 3, d)),
    }


def reference(x_prompt, x_sample, cache_a_k, cache_a_v, cache_b_k, cache_b_v, c, c_ctx,
              w_mod, b_mod, w_in, g_qa, g_ka, lam_q1, lam_k1, lam_q2, lam_k2, g_subln,
              w_fourier, w_out, w_ffn1_gu, w_ffn1_down, w_ffn2_gu, w_ffn2_down, ln_g, ln_b):
    weights = (w_mod, b_mod, w_in, g_qa, g_ka, lam_q1, lam_k1, lam_q2, lam_k2, g_subln,
               w_fourier, w_out, w_ffn1_gu, w_ffn1_down, w_ffn2_gu, w_ffn2_down, ln_g, ln_b)
    y_prompt, new = trunk(x_prompt, c_ctx[None, :], None, *weights)
    new_a_k = jnp.stack([n[0] for n in new], axis=1)
    new_a_v = jnp.stack([n[1] for n in new], axis=1)
    new_b_k = jnp.stack([n[2] for n in new], axis=1)
    new_b_v = jnp.stack([n[3] for n in new], axis=1)
    y_sample, _ = trunk(x_sample, c, (cache_a_k, cache_a_v, cache_b_k, cache_b_v), *weights)
    return (y_prompt, y_sample, new_a_k, new_a_v, new_b_k, new_b_v)
```

```python
import functools
import math

import numpy as np
import jax
import jax.numpy as jnp
from jax import lax
from jax.experimental import pallas as pl
from jax.experimental.pallas import tpu as pltpu

F32 = jnp.float32
BF16 = jnp.bfloat16

D_MODEL = 1024
DEPTH = 2
GRID_W = 64
HEAD_DIM = 64
ROPE_THETA = 10000.0
A_HEADS = 4
A_KV_HEADS = 2
B_HEADS = 4
B_QK_DIM = 64
B_V_DIM = 128
C_GROUPS = 4
C_GROUP_DIM = 64
C_WIDTH = 256
FFN_DIM = 2816
N_MOD = 9
IN_WIDTH = 2304
ALPHA = (2.0 * DEPTH) ** 0.25
LN_EPS = 1e-5
RMS_EPS = 1e-6

LANES = 128
MOD_ROWS = 8
SEG_LEN = 4096
VMEM_LIMIT = 52 << 20

QA_OFF, KA_OFF, VA_OFF, QB_OFF, KB_OFF, VB_OFF, UC_OFF = 0, 512, 640, 768, 1280, 1792, 2304
PROJ_W = 2560

FFN_TM = 1024
FFN_TF = 256
PROJ_TM = 512
OUT_TM = 512
ATT_TQ = 256
ATT_UNITS = 2
ATT_KC = 512
LOG2E = math.log2(math.e)
FOUR_TK = 256


def _cparams(*sem):
    return pltpu.CompilerParams(dimension_semantics=sem, vmem_limit_bytes=VMEM_LIMIT)


def _rope_tables(length):
    rows = length // GRID_W
    row = np.repeat(np.arange(rows), GRID_W).astype(np.float64)
    col = np.tile(np.arange(GRID_W), rows).astype(np.float64)
    half = HEAD_DIM // 2
    inv = 1.0 / (ROPE_THETA ** (np.arange(0, half, 2, dtype=np.float64) / half))
    ar = row[:, None] * inv
    ac = col[:, None] * inv
    cos = np.concatenate([np.cos(ar), np.cos(ar), np.cos(ac), np.cos(ac)], -1)
    sin = np.concatenate([-np.sin(ar), np.sin(ar), -np.sin(ac), np.sin(ac)], -1)
    return (jnp.asarray(np.tile(cos, (1, 2)), F32), jnp.asarray(np.tile(sin, (1, 2)), F32))


def _group_mean_matrix(width, group):
    m = np.kron(np.eye(width // group), np.full((group, group), 1.0 / group))
    return jnp.asarray(m, F32)


def _channel_dft():
    c = np.arange(C_GROUP_DIM)
    ang = 2.0 * np.pi * np.outer(c, c) / C_GROUP_DIM
    eye = np.eye(C_GROUPS)
    return (jnp.asarray(np.kron(eye, np.cos(ang)), F32), jnp.asarray(np.kron(eye, np.sin(ang)), F32))


def _position_dft_small(length):
    n = np.arange(length)
    ang = 2.0 * np.pi * ((np.outer(n, n)) % length) / length
    return jnp.asarray(np.concatenate([np.cos(ang), -np.sin(ang)], 1), F32)


def _position_dft_large(length):
    n = jnp.arange(length, dtype=jnp.int32)
    kn = (n[:, None] * n[None, :]) & (length - 1)
    ang = kn.astype(F32) * (2.0 * math.pi / length)
    return jnp.concatenate([jnp.cos(ang), -jnp.sin(ang)], 1).astype(BF16)


def _layer_norm(y, g, b):
    mu = jnp.mean(y, axis=-1, keepdims=True)
    yc = y - mu
    var = jnp.mean(yc * yc, axis=-1, keepdims=True)
    return yc * lax.rsqrt(var + LN_EPS) * g + b


def _dot(a, b):
    return jnp.dot(a, b, preferred_element_type=F32)


def _dot_nt(a, b):
    return lax.dot_general(a, b, (((1,), (1,)), ((), ())), preferred_element_type=F32)


def _dot_exact(a, b):
    return jnp.dot(a, b, preferred_element_type=F32, precision=lax.Precision.HIGHEST)


def _softmax_pv_units(units, s_ref):
    rows = units[0][0].shape[0]
    chunk_lists = []
    for _, pieces in units:
        chunks, off = [], 0
        for k_ref, v_ref in pieces:
            for st in range(0, k_ref.shape[0], ATT_KC):
                sz = min(ATT_KC, k_ref.shape[0] - st)
                chunks.append((k_ref, v_ref, st, sz, off))
                off += sz
        chunk_lists.append(chunks)
    nchunk = len(chunk_lists[0])
    run_max = [None] * len(units)
    row_max = [None] * len(units)
    acc = [None] * len(units)

    def scores(u, ci):
        k_ref, _, st, sz, off = chunk_lists[u][ci]
        s = _dot_nt(units[u][0], k_ref[st:st + sz, :].astype(BF16))
        s_ref[u, :, off:off + sz] = s
        part = functools.reduce(jnp.maximum, [s[:, j:j + LANES] for j in range(0, sz, LANES)])
        run_max[u] = part if run_max[u] is None else jnp.maximum(run_max[u], part)

    def finish_max(u):
        m = jnp.max(run_max[u], axis=1, keepdims=True)
        row_max[u] = jnp.broadcast_to(m, (rows, LANES))

    def weighted(u, ci):
        _, v_ref, st, sz, off = chunk_lists[u][ci]
        e = jnp.concatenate([jnp.exp2(s_ref[u, :, off + j:off + j + LANES] - row_max[u])
                             for j in range(0, sz, LANES)], axis=1).astype(BF16)
        v1 = jnp.concatenate([v_ref[st:st + sz, :].astype(BF16), jnp.ones((sz, LANES), BF16)], axis=1)
        d = _dot(e, v1)
        acc[u] = d if acc[u] is None else acc[u] + d

    for ci in range(nchunk):
        scores(0, ci)
    finish_max(0)
    for u in range(1, len(units)):
        for ci in range(nchunk):
            weighted(u - 1, ci)
            scores(u, ci)
        finish_max(u)
    for ci in range(nchunk):
        weighted(len(units) - 1, ci)
    return [a[:, :LANES] / a[:, LANES:] for a in acc]


def _gqa_query(qpad):
    return jnp.concatenate([qpad[:, :LANES], qpad[:, LANES:]], axis=0)


def _gqa_output(o, group):
    tq = o.shape[0] // 2
    o0, o1 = o[:tq], o[tq:]
    low = lax.broadcasted_iota(jnp.int32, (tq, LANES), 1) < HEAD_DIM
    first = group == 0
    r0, r1 = pltpu.roll(o0, HEAD_DIM, 1), pltpu.roll(o1, HEAD_DIM, 1)
    return jnp.where(low, jnp.where(first, o0, r0), jnp.where(first, r1, o1))


def _diff_query(q):
    low = lax.broadcasted_iota(jnp.int32, q.shape, 1) < B_QK_DIM
    zero = jnp.zeros_like(q)
    return jnp.concatenate([jnp.where(low, q, zero), jnp.where(low, zero, q)], axis=0)


def _diff_output(o, lam, g_sub, lam_init):
    tq = o.shape[0] // 2
    w = o[:tq] - lam * o[tq:]
    y = w * lax.rsqrt(jnp.mean(w * w, axis=-1, keepdims=True) + RMS_EPS)
    return y * g_sub * (1.0 - lam_init)


def _lambda(lp):
    s1 = jnp.sum(lp[0:1] * lp[1:2], axis=1, keepdims=True)
    s2 = jnp.sum(lp[2:3] * lp[3:4], axis=1, keepdims=True)
    return jnp.exp(s1) - jnp.exp(s2)


def _mod_kernel(c_ref, w_ref, b_ref, o_ref):
    c = c_ref[...]
    s = (c * jax.nn.sigmoid(c)).astype(BF16)
    o_ref[...] = _dot(s, w_ref[...].astype(BF16)) + b_ref[...]


def _modulation(cvec, w_mod, b_mod):
    out = pl.pallas_call(
        _mod_kernel,
        out_shape=jax.ShapeDtypeStruct((DEPTH, N_MOD, MOD_ROWS, D_MODEL), F32),
        grid=(DEPTH, N_MOD),
        in_specs=[
            pl.BlockSpec((MOD_ROWS, D_MODEL), lambda l, n: (0, 0)),
            pl.BlockSpec((None, D_MODEL, D_MODEL), lambda l, n: (l, 0, n)),
            pl.BlockSpec((None, None, 1, D_MODEL), lambda l, n: (l, n, 0, 0)),
        ],
        out_specs=pl.BlockSpec((None, None, MOD_ROWS, D_MODEL), lambda l, n: (l, n, 0, 0)),
        compiler_params=_cparams("parallel", "parallel"),
        name="modulation",
    )(cvec, w_mod, b_mod.reshape(DEPTH, N_MOD, 1, D_MODEL))
    return out.reshape(DEPTH, N_MOD, MOD_ROWS, 1, D_MODEL)


def _mod_spec(layer, chunk, row0, tm, ngrid):
    def seg(i):
        return row0 + (i * tm) // SEG_LEN
    if ngrid == 1:
        return pl.BlockSpec((None, None, None, 1, D_MODEL), lambda i: (layer, chunk, seg(i), 0, 0))
    return pl.BlockSpec((None, None, None, 1, D_MODEL), lambda i, j: (layer, chunk, seg(i), 0, 0))


def _ffn_kernel(x_ref, sh_ref, sc_ref, gt_ref, wg_ref, wu_ref, wd_ref, lng_ref, lnb_ref,
                o_ref, h_ref, acc_ref):
    j = pl.program_id(1)

    @pl.when(j == 0)
    def _():
        h_ref[...] = (x_ref[...] * (1.0 + sc_ref[...]) + sh_ref[...]).astype(BF16)
        acc_ref[...] = jnp.zeros_like(acc_ref)

    h = h_ref[...]
    g = _dot(h, wg_ref[...].astype(BF16))
    u = _dot(h, wu_ref[...].astype(BF16))
    a = (g * jax.nn.sigmoid(g) * u).astype(BF16)
    acc_ref[...] += _dot(a, wd_ref[...].astype(BF16))

    @pl.when(j == pl.num_programs(1) - 1)
    def _():
        y = ALPHA * x_ref[...] + (0.5 * gt_ref[...]) * acc_ref[...]
        o_ref[...] = _layer_norm(y, lng_ref[...], lnb_ref[...])


def _ffn(x, mod, layer, sub, row0, w_gu, w_down, ln_g, ln_b):
    t = x.shape[0]
    nf = FFN_DIM // FFN_TF
    return pl.pallas_call(
        _ffn_kernel,
        out_shape=jax.ShapeDtypeStruct((t, D_MODEL), F32),
        grid=(t // FFN_TM, nf),
        in_specs=[
            pl.BlockSpec((FFN_TM, D_MODEL), lambda i, j: (i, 0)),
            _mod_spec(layer, 3 * sub, row0, FFN_TM, 2),
            _mod_spec(layer, 3 * sub + 1, row0, FFN_TM, 2),
            _mod_spec(layer, 3 * sub + 2, row0, FFN_TM, 2),
            pl.BlockSpec((None, D_MODEL, FFN_TF), lambda i, j: (layer, 0, j)),
            pl.BlockSpec((None, D_MODEL, FFN_TF), lambda i, j: (layer, 0, j + nf)),
            pl.BlockSpec((None, FFN_TF, D_MODEL), lambda i, j: (layer, j, 0)),
            pl.BlockSpec((None, None, 1, D_MODEL), lambda i, j: (layer, sub, 0, 0)),
            pl.BlockSpec((None, None, 1, D_MODEL), lambda i, j: (layer, sub, 0, 0)),
        ],
        out_specs=pl.BlockSpec((FFN_TM, D_MODEL), lambda i, j: (i, 0)),
        scratch_shapes=[pltpu.VMEM((FFN_TM, D_MODEL), BF16), pltpu.VMEM((FFN_TM, D_MODEL), F32)],
        compiler_params=_cparams("parallel", "arbitrary"),
        name="ffn",
    )(x, mod, mod, mod, w_gu, w_gu, w_down, ln_g, ln_b)


def _rope(x, cos, sin):
    quarter = HEAD_DIM // 4
    lane = lax.broadcasted_iota(jnp.int32, x.shape, 1)
    first = (lane % (2 * quarter)) < quarter
    rot = jnp.where(first, pltpu.roll(x, LANES - quarter, 1), pltpu.roll(x, quarter, 1))
    return x * cos + rot * sin


def _proj_kernel(rope, x_ref, sh_ref, sc_ref, w_ref, gq_ref, gk_ref, pm_ref, *rest):
    if rope:
        cos_ref, sin_ref, p_ref, wb_ref = rest
    else:
        p_ref, nak_ref, nav_ref, nbk_ref, nbv_ref, wb_ref = rest

    @pl.when(pl.program_id(0) == 0)
    def _():
        wb_ref[...] = w_ref[...].astype(BF16)

    h = (x_ref[...] * (1.0 + sc_ref[...]) + sh_ref[...]).astype(BF16)
    proj = _dot(h, wb_ref[...])
    tm = proj.shape[0]

    qa = proj[:, 0:256]
    qa = qa * lax.rsqrt(_dot_exact(qa * qa, pm_ref[...]) + RMS_EPS) * gq_ref[...]
    ka = proj[:, 256:384]
    ka = ka * lax.rsqrt(_dot_exact(ka * ka, pm_ref[0:LANES, 0:LANES]) + RMS_EPS) * gk_ref[...]
    va = proj[:, 384:512]
    qb = proj[:, 512:1024]
    kb = proj[:, 1024:1536]
    vb = proj[:, 1536:2048]
    uc = proj[:, 2048:2304]

    if not rope:
        nak_ref[...] = ka
        nav_ref[...] = va
        nbk_ref[...] = kb
        nbv_ref[...] = vb

    def slabs(a):
        return [a[:, s:s + LANES] for s in range(0, a.shape[1], LANES)]

    if rope:
        cos, sin = cos_ref[...], sin_ref[...]
        qa_s = [_rope(s, cos, sin) for s in slabs(qa)]
        ka_s = [_rope(ka, cos, sin)]
        qb_s = [_rope(s, cos, sin) for s in slabs(qb)]
        kb_s = [_rope(s, cos, sin) for s in slabs(kb)]
    else:
        qa_s, ka_s, qb_s, kb_s = slabs(qa), [ka], slabs(qb), slabs(kb)

    scale = LOG2E * HEAD_DIM ** -0.5
    low = lax.broadcasted_iota(jnp.int32, (tm, LANES), 1) < HEAD_DIM
    zero = jnp.zeros((tm, LANES), F32)
    g0, g1 = qa_s[0] * scale, qa_s[1] * scale
    qa_pad = [jnp.where(low, g0, zero), jnp.where(low, pltpu.roll(g0, HEAD_DIM, 1), zero),
              jnp.where(low, zero, pltpu.roll(g1, HEAD_DIM, 1)), jnp.where(low, zero, g1)]
    pieces = qa_pad + ka_s + [va] + [s * scale for s in qb_s] + kb_s + slabs(vb) + slabs(uc)
    for n, piece in enumerate(pieces):
        p_ref[:, n * LANES:(n + 1) * LANES] = piece.astype(BF16)


def _project(x, mod, layer, row0, w_in, gq, gk, pmean, rope_tabs):
    t = x.shape[0]
    rope = rope_tabs is not None
    nt = t // PROJ_TM
    in_specs = [
        pl.BlockSpec((PROJ_TM, D_MODEL), lambda i: (i, 0)),
        _mod_spec(layer, 3, row0, PROJ_TM, 1),
        _mod_spec(layer, 4, row0, PROJ_TM, 1),
        pl.BlockSpec((None, D_MODEL, IN_WIDTH), lambda i: (layer, 0, 0)),
        pl.BlockSpec((None, 1, 256), lambda i: (layer, 0, 0)),
        pl.BlockSpec((None, 1, LANES), lambda i: (layer, 0, 0)),
        pl.BlockSpec((256, 256), lambda i: (0, 0)),
    ]
    args = [x, mod, mod, w_in, gq, gk, pmean]
    out_shape = [jax.ShapeDtypeStruct((t, PROJ_W), BF16)]
    out_specs = [pl.BlockSpec((PROJ_TM, PROJ_W), lambda i: (i, 0))]
    if rope:
        per_seg = SEG_LEN // PROJ_TM
        in_specs += [pl.BlockSpec((PROJ_TM, LANES), lambda i: (i % per_seg, 0))] * 2
        args += list(rope_tabs)
    else:
        for width in (LANES, LANES, 512, 512):
            out_shape.append(jax.ShapeDtypeStruct((t, width), F32))
            out_specs.append(pl.BlockSpec((PROJ_TM, width), lambda i: (i, 0)))
    return pl.pallas_call(
        functools.partial(_proj_kernel, rope),
        out_shape=out_shape,
        grid=(nt,),
        in_specs=in_specs,
        out_specs=out_specs,
        scratch_shapes=[pltpu.VMEM((D_MODEL, IN_WIDTH), BF16)],
        compiler_params=_cparams("arbitrary"),
        name="project_latent" if rope else "project_context",
    )(*args)


def _ctx_attn_kernel(lam_init, p_ref, lp_ref, gs_ref, oa_ref, ob_ref, s_ref):
    ka = p_ref.at[:, KA_OFF:KA_OFF + LANES]
    va = p_ref.at[:, VA_OFF:VA_OFF + LANES]
    units = []
    for g in range(A_KV_HEADS):
        units.append((_gqa_query(p_ref[:, QA_OFF + 256 * g:QA_OFF + 256 * (g + 1)]), [(ka, va)]))
    for h in range(B_HEADS):
        k = p_ref.at[:, KB_OFF + h * LANES:KB_OFF + (h + 1) * LANES]
        v = p_ref.at[:, VB_OFF + h * LANES:VB_OFF + (h + 1) * LANES]
        units.append((_diff_query(p_ref[:, QB_OFF + h * LANES:QB_OFF + (h + 1) * LANES]), [(k, v)]))
    outs = _softmax_pv_units(units, s_ref)
    lam = _lambda(lp_ref[...]) + lam_init
    for g in range(A_KV_HEADS):
        oa_ref[:, g * LANES:(g + 1) * LANES] = _gqa_output(outs[g], g).astype(BF16)
    for h in range(B_HEADS):
        o = _diff_output(outs[A_KV_HEADS + h], lam, gs_ref[...], lam_init)
        ob_ref[:, h * LANES:(h + 1) * LANES] = o.astype(BF16)


def _context_attention(p, layer, lam_params, g_subln, lam_init, seq):
    t = p.shape[0]
    return pl.pallas_call(
        functools.partial(_ctx_attn_kernel, lam_init),
        out_shape=[jax.ShapeDtypeStruct((t, 256), BF16), jax.ShapeDtypeStruct((t, 512), BF16)],
        grid=(t // seq,),
        in_specs=[
            pl.BlockSpec((seq, PROJ_W), lambda i: (i, 0)),
            pl.BlockSpec((None, 4, B_QK_DIM), lambda i: (layer, 0, 0)),
            pl.BlockSpec((None, 1, B_V_DIM), lambda i: (layer, 0, 0)),
        ],
        out_specs=[pl.BlockSpec((seq, 256), lambda i: (i, 0)), pl.BlockSpec((seq, 512), lambda i: (i, 0))],
        scratch_shapes=[pltpu.VMEM((A_KV_HEADS + B_HEADS, 2 * seq, seq), F32)],
        compiler_params=_cparams("parallel"),
        name="context_attention",
    )(p, lam_params, g_subln)


def _lat_gqa_kernel(q_ref, kn_ref, vn_ref, kc_ref, vc_ref, o_ref, s_ref):
    pieces = [(kc_ref, vc_ref), (kn_ref, vn_ref)]
    units = [(_gqa_query(q_ref[u * ATT_TQ:(u + 1) * ATT_TQ, :]), pieces) for u in range(ATT_UNITS)]
    outs = _softmax_pv_units(units, s_ref)
    for u in range(ATT_UNITS):
        o_ref[u * ATT_TQ:(u + 1) * ATT_TQ, :] = _gqa_output(outs[u], pl.program_id(1)).astype(BF16)


def _latent_gqa(p, cache_k, cache_v, layer, batch, length):
    tile = ATT_TQ * ATT_UNITS
    nq = length // tile
    past = cache_k.shape[2]
    return pl.pallas_call(
        _lat_gqa_kernel,
        out_shape=jax.ShapeDtypeStruct((batch * length, 256), BF16),
        grid=(batch, A_KV_HEADS, nq),
        in_specs=[
            pl.BlockSpec((tile, 256), lambda b, g, i: (b * nq + i, QA_OFF // 256 + g)),
            pl.BlockSpec((length, LANES), lambda b, g, i: (b, KA_OFF // LANES)),
            pl.BlockSpec((length, LANES), lambda b, g, i: (b, VA_OFF // LANES)),
            pl.BlockSpec((None, None, past, LANES), lambda b, g, i: (b, layer, 0, 0)),
            pl.BlockSpec((None, None, past, LANES), lambda b, g, i: (b, layer, 0, 0)),
        ],
        out_specs=pl.BlockSpec((tile, LANES), lambda b, g, i: (b * nq + i, g)),
        scratch_shapes=[pltpu.VMEM((ATT_UNITS, 2 * ATT_TQ, past + length), F32)],
        compiler_params=_cparams("parallel", "parallel", "parallel"),
        name="latent_gqa",
    )(p, p, p, cache_k, cache_v)


def _lat_diff_kernel(lam_init, q_ref, kn_ref, vn_ref, kc_ref, vc_ref, lp_ref, gs_ref, o_ref, s_ref):
    pieces = [(kc_ref, vc_ref), (kn_ref, vn_ref)]
    units = [(_diff_query(q_ref[u * ATT_TQ:(u + 1) * ATT_TQ, :]), pieces) for u in range(ATT_UNITS)]
    outs = _softmax_pv_units(units, s_ref)
    lam = _lambda(lp_ref[...]) + lam_init
    for u in range(ATT_UNITS):
        o = _diff_output(outs[u], lam, gs_ref[...], lam_init)
        o_ref[u * ATT_TQ:(u + 1) * ATT_TQ, :] = o.astype(BF16)


def _latent_diff(p, cache_k, cache_v, layer, lam_params, g_subln, lam_init, batch, length):
    tile = ATT_TQ * ATT_UNITS
    nq = length // tile
    past = cache_k.shape[2]
    return pl.pallas_call(
        functools.partial(_lat_diff_kernel, lam_init),
        out_shape=jax.ShapeDtypeStruct((batch * length, 512), BF16),
        grid=(batch, B_HEADS, nq),
        in_specs=[
            pl.BlockSpec((tile, LANES), lambda b, h, i: (b * nq + i, QB_OFF // LANES + h)),
            pl.BlockSpec((length, LANES), lambda b, h, i: (b, KB_OFF // LANES + h)),
            pl.BlockSpec((length, LANES), lambda b, h, i: (b, VB_OFF // LANES + h)),
            pl.BlockSpec((None, None, past, LANES), lambda b, h, i: (b, layer, 0, h)),
            pl.BlockSpec((None, None, past, LANES), lambda b, h, i: (b, layer, 0, h)),
            pl.BlockSpec((None, 4, B_QK_DIM), lambda b, h, i: (layer, 0, 0)),
            pl.BlockSpec((None, 1, B_V_DIM), lambda b, h, i: (layer, 0, 0)),
        ],
        out_specs=pl.BlockSpec((tile, LANES), lambda b, h, i: (b * nq + i, h)),
        scratch_shapes=[pltpu.VMEM((ATT_UNITS, 2 * ATT_TQ, past + length), F32)],
        compiler_params=_cparams("parallel", "parallel", "parallel"),
        name="latent_diff",
    )(p, p, p, cache_k, cache_v, lam_params, g_subln)


def _ctx_fourier_kernel(norm, u_ref, bc_ref, bs_ref, cs_ref, o_ref):
    u = u_ref[...]
    rhs = jnp.concatenate([_dot(u, bc_ref[...]), _dot(u, bs_ref[...])], axis=0).astype(BF16)
    o_ref[...] = (_dot(cs_ref[...], rhs) * norm).astype(BF16)


def _context_fourier(p, bc, bs, cs, seq):
    t = p.shape[0]
    norm = 1.0 / math.sqrt(seq * C_GROUP_DIM)
    return pl.pallas_call(
        functools.partial(_ctx_fourier_kernel, norm),
        out_shape=jax.ShapeDtypeStruct((t, C_WIDTH), BF16),
        grid=(t // seq,),
        in_specs=[
            pl.BlockSpec((seq, C_WIDTH), lambda i: (i, UC_OFF // C_WIDTH)),
            pl.BlockSpec((C_WIDTH, C_WIDTH), lambda i: (0, 0)),
            pl.BlockSpec((C_WIDTH, C_WIDTH), lambda i: (0, 0)),
            pl.BlockSpec((seq, 2 * seq), lambda i: (0, 0)),
        ],
        out_specs=pl.BlockSpec((seq, C_WIDTH), lambda i: (i, 0)),
        compiler_params=_cparams("parallel"),
        name="context_fourier",
    )(p, bc, bs, cs)


def _lat_fourier_kernel(norm, u0_ref, u1_ref, bc_ref, bs_ref, cs_ref, o_ref, rhs_ref):
    length = u0_ref.shape[0]

    @pl.when(pl.program_id(0) == 0)
    def _():
        for b, u_ref in enumerate((u0_ref, u1_ref)):
            u = u_ref[...]
            cols = slice(b * C_WIDTH, (b + 1) * C_WIDTH)
            rhs_ref[0:length, cols] = _dot(u, bc_ref[...]).astype(BF16)
            rhs_ref[length:2 * length, cols] = _dot(u, bs_ref[...]).astype(BF16)

    f = _dot(cs_ref[...], rhs_ref[...]) * norm
    o_ref[0] = f[:, :C_WIDTH].astype(BF16)
    o_ref[1] = f[:, C_WIDTH:].astype(BF16)


def _latent_fourier(p, bc, bs, cs, length):
    norm = 1.0 / math.sqrt(length * C_GROUP_DIM)
    out = pl.pallas_call(
        functools.partial(_lat_fourier_kernel, norm),
        out_shape=jax.ShapeDtypeStruct((2, length, C_WIDTH), BF16),
        grid=(length // FOUR_TK,),
        in_specs=[
            pl.BlockSpec((length, C_WIDTH), lambda i: (0, UC_OFF // C_WIDTH)),
            pl.BlockSpec((length, C_WIDTH), lambda i: (1, UC_OFF // C_WIDTH)),
            pl.BlockSpec((C_WIDTH, C_WIDTH), lambda i: (0, 0)),
            pl.BlockSpec((C_WIDTH, C_WIDTH), lambda i: (0, 0)),
            pl.BlockSpec((FOUR_TK, 2 * length), lambda i: (i, 0)),
        ],
        out_specs=pl.BlockSpec((2, FOUR_TK, C_WIDTH), lambda i: (0, i, 0)),
        scratch_shapes=[pltpu.VMEM((2 * length, 2 * C_WIDTH), BF16)],
        compiler_params=_cparams("arbitrary"),
        name="latent_fourier",
    )(p, p, bc, bs, cs)
    return out.reshape(2 * length, C_WIDTH)


def _out_kernel(x_ref, gt_ref, oa_ref, ob_ref, f_ref, wf_ref, wo_ref, lng_ref, lnb_ref, o_ref, wob_ref):
    @pl.when(pl.program_id(0) == 0)
    def _():
        wob_ref[...] = wo_ref[...].astype(BF16)

    oc = _dot(f_ref[...], wf_ref[...].astype(BF16)).astype(BF16)
    cat = jnp.concatenate([oa_ref[...], ob_ref[...], oc], axis=1)
    mo = _dot(cat, wob_ref[...])
    y = ALPHA * x_ref[...] + gt_ref[...] * mo
    o_ref[...] = _layer_norm(y, lng_ref[...], lnb_ref[...])


def _out_project(x, mod, layer, row0, oa, ob, f, w_fourier, w_out, ln_g, ln_b):
    t = x.shape[0]
    return pl.pallas_call(
        _out_kernel,
        out_shape=jax.ShapeDtypeStruct((t, D_MODEL), F32),
        grid=(t // OUT_TM,),
        in_specs=[
            pl.BlockSpec((OUT_TM, D_MODEL), lambda i: (i, 0)),
            _mod_spec(layer, 5, row0, OUT_TM, 1),
            pl.BlockSpec((OUT_TM, 256), lambda i: (i, 0)),
            pl.BlockSpec((OUT_TM, 512), lambda i: (i, 0)),
            pl.BlockSpec((OUT_TM, 256), lambda i: (i, 0)),
            pl.BlockSpec((None, C_WIDTH, C_WIDTH), lambda i: (layer, 0, 0)),
            pl.BlockSpec((None, D_MODEL, D_MODEL), lambda i: (layer, 0, 0)),
            pl.BlockSpec((None, None, 1, D_MODEL), lambda i: (layer, 1, 0, 0)),
            pl.BlockSpec((None, None, 1, D_MODEL), lambda i: (layer, 1, 0, 0)),
        ],
        out_specs=pl.BlockSpec((OUT_TM, D_MODEL), lambda i: (i, 0)),
        scratch_shapes=[pltpu.VMEM((D_MODEL, D_MODEL), BF16)],
        compiler_params=_cparams("arbitrary"),
        name="out_project",
    )(x, mod, oa, ob, f, w_fourier, w_out, ln_g, ln_b)


def kernel(x_prompt, x_sample, cache_a_k, cache_a_v, cache_b_k, cache_b_v, c, c_ctx, w_mod, b_mod, w_in, g_qa, g_ka, lam_q1, lam_k1, lam_q2, lam_k2, g_subln, w_fourier, w_out, w_ffn1_gu, w_ffn1_down, w_ffn2_gu, w_ffn2_down, ln_g, ln_b):
    batch, seq, d = x_prompt.shape
    dec_batch, dec_seq, _ = x_sample.shape
    past = cache_a_k.shape[2]
    assert d == D_MODEL and dec_seq == SEG_LEN and batch * seq == SEG_LEN and dec_batch == 2

    cvec = jnp.concatenate([c_ctx[None, :], c, jnp.zeros((MOD_ROWS - 1 - dec_batch, d), F32)], axis=0)
    mod = _modulation(cvec, w_mod, b_mod)

    gq = jnp.tile(g_qa, (1, A_HEADS)).reshape(DEPTH, 1, A_HEADS * HEAD_DIM)
    gk = jnp.tile(g_ka, (1, A_KV_HEADS)).reshape(DEPTH, 1, A_KV_HEADS * HEAD_DIM)
    gs = g_subln.reshape(DEPTH, 1, B_V_DIM)
    lam_params = jnp.stack([lam_q1, lam_k1, lam_q2, lam_k2], axis=1)
    ln_g4 = ln_g.reshape(DEPTH, 3, 1, d)
    ln_b4 = ln_b.reshape(DEPTH, 3, 1, d)
    pmean = _group_mean_matrix(256, HEAD_DIM)
    rope_tabs = _rope_tables(dec_seq)
    bc, bs = (m.astype(BF16) for m in _channel_dft())
    cs_ctx = _position_dft_small(seq).astype(BF16)
    cs_lat = _position_dft_large(dec_seq)
    ck_a = cache_a_k.reshape(dec_batch, DEPTH, past, A_KV_HEADS * HEAD_DIM)
    cv_a = cache_a_v.reshape(dec_batch, DEPTH, past, A_KV_HEADS * HEAD_DIM)
    ck_b = cache_b_k.reshape(dec_batch, DEPTH, past, B_HEADS * 2 * B_QK_DIM)
    cv_b = cache_b_v.reshape(dec_batch, DEPTH, past, B_HEADS * B_V_DIM)

    xp = x_prompt.reshape(batch * seq, d)
    xs = x_sample.reshape(dec_batch * dec_seq, d)
    new = []
    for layer in range(DEPTH):
        lam_init = 0.8 - 0.6 * math.exp(-0.3 * layer)
        xp = _ffn(xp, mod, layer, 0, 0, w_ffn1_gu, w_ffn1_down, ln_g4, ln_b4)
        pp, nak, nav, nbk, nbv = _project(xp, mod, layer, 0, w_in, gq, gk, pmean, None)
        new.append((nak, nav, nbk, nbv))
        oa, ob = _context_attention(pp, layer, lam_params, gs, lam_init, seq)
        fo = _context_fourier(pp, bc, bs, cs_ctx, seq)
        xp = _out_project(xp, mod, layer, 0, oa, ob, fo, w_fourier, w_out, ln_g4, ln_b4)
        xp = _ffn(xp, mod, layer, 2, 0, w_ffn2_gu, w_ffn2_down, ln_g4, ln_b4)
        xs = _ffn(xs, mod, layer, 0, 1, w_ffn1_gu, w_ffn1_down, ln_g4, ln_b4)
        (ps,) = _project(xs, mod, layer, 1, w_in, gq, gk, pmean, rope_tabs)
        oa = _latent_gqa(ps, ck_a, cv_a, layer, dec_batch, dec_seq)
        ob = _latent_diff(ps, ck_b, cv_b, layer, lam_params, gs, lam_init, dec_batch, dec_seq)
        fo = _latent_fourier(ps, bc, bs, cs_lat, dec_seq)
        xs = _out_project(xs, mod, layer, 1, oa, ob, fo, w_fourier, w_out, ln_g4, ln_b4)
        xs = _ffn(xs, mod, layer, 2, 1, w_ffn2_gu, w_ffn2_down, ln_g4, ln_b4)

    def stack(idx, heads, width):
        return jnp.stack([n[idx].reshape(batch, seq, heads, width) for n in new], axis=1)

    return (xp.reshape(batch, seq, d), xs.reshape(dec_batch, dec_seq, d),
            stack(0, A_KV_HEADS, HEAD_DIM), stack(1, A_KV_HEADS, HEAD_DIM),
            stack(2, B_HEADS, 2 * B_QK_DIM), stack(3, B_HEADS, B_V_DIM))
```

```python
import functools
import math

import numpy as np
import jax
import jax.numpy as jnp
from jax import lax
from jax.experimental import pallas as pl
from jax.experimental.pallas import tpu as pltpu

F32 = jnp.float32
BF16 = jnp.bfloat16

D_MODEL = 1024
DEPTH = 2
GRID_W = 64
HEAD_DIM = 64
ROPE_THETA = 10000.0
A_HEADS = 4
A_KV_HEADS = 2
B_HEADS = 4
B_QK_DIM = 64
B_V_DIM = 128
C_GROUPS = 4
C_GROUP_DIM = 64
C_WIDTH = 256
FFN_DIM = 2816
N_MOD = 9
IN_WIDTH = 2304
ALPHA = (2.0 * DEPTH) ** 0.25
LN_EPS = 1e-5
RMS_EPS = 1e-6

LANES = 128
MOD_ROWS = 8
SEG_LEN = 4096
VMEM_LIMIT = 52 << 20

QA_OFF, KA_OFF, VA_OFF, QB_OFF, KB_OFF, VB_OFF = 0, 512, 640, 768, 1280, 1792
PROJ_W = 2304

FFN_TM = 512
FFN_TF = 256
PROJ_TM = 512
OUT_TM = 512
ATT_TQ = 256
ATT_UNITS = 2
ATT_KC = 512
LOG2E = math.log2(math.e)


def _cparams(*sem):
    return pltpu.CompilerParams(dimension_semantics=sem, vmem_limit_bytes=VMEM_LIMIT)


def _rope_tables(length):
    rows = length // GRID_W
    row = np.repeat(np.arange(rows), GRID_W).astype(np.float64)
    col = np.tile(np.arange(GRID_W), rows).astype(np.float64)
    half = HEAD_DIM // 2
    inv = 1.0 / (ROPE_THETA ** (np.arange(0, half, 2, dtype=np.float64) / half))
    ar = row[:, None] * inv
    ac = col[:, None] * inv
    cos = np.concatenate([np.cos(ar), np.cos(ar), np.cos(ac), np.cos(ac)], -1)
    sin = np.concatenate([-np.sin(ar), np.sin(ar), -np.sin(ac), np.sin(ac)], -1)
    return (jnp.asarray(np.tile(cos, (1, 2)), F32), jnp.asarray(np.tile(sin, (1, 2)), F32))


def _group_mean_matrix(width, group):
    m = np.kron(np.eye(width // group), np.full((group, group), 1.0 / group))
    return jnp.asarray(m, F32)


def _channel_dft():
    c = np.arange(C_GROUP_DIM)
    ang = 2.0 * np.pi * np.outer(c, c) / C_GROUP_DIM
    eye = np.eye(C_GROUPS)
    return jnp.asarray(np.concatenate([np.kron(eye, np.cos(ang)), np.kron(eye, np.sin(ang))], 0), F32)


def _position_dft_small(length):
    n = np.arange(length)
    ang = 2.0 * np.pi * ((np.outer(n, n)) % length) / length
    return jnp.asarray(np.concatenate([np.cos(ang), -np.sin(ang)], 1), F32)


def _two_stage_dft(n1):
    i = np.arange(n1)
    ang = 2.0 * np.pi * np.outer(i, i) / n1
    c, s = np.cos(ang), np.sin(ang)
    m1 = np.concatenate([c, -s], 0)
    m2 = np.block([[c, s], [-s, c]])
    tw = 2.0 * np.pi * np.outer(i, i).reshape(-1) / (n1 * n1)
    tc = np.repeat(np.cos(tw)[:, None], LANES, 1)
    ts = np.repeat(np.sin(tw)[:, None], LANES, 1)
    return tuple(jnp.asarray(a, F32) for a in (m1, m2, tc, ts))


def _layer_norm(y, g, b):
    mu = jnp.mean(y, axis=-1, keepdims=True)
    yc = y - mu
    var = jnp.mean(yc * yc, axis=-1, keepdims=True)
    return yc * lax.rsqrt(var + LN_EPS) * g + b


def _dot(a, b):
    return jnp.dot(a, b, preferred_element_type=F32)


def _dot_nt(a, b):
    return lax.dot_general(a, b, (((1,), (1,)), ((), ())), preferred_element_type=F32)


def _dot_exact(a, b):
    return jnp.dot(a, b, preferred_element_type=F32, precision=lax.Precision.HIGHEST)


def _softmax_pv_units(units, s_ref):
    rows = units[0][0].shape[0]
    chunk_lists = []
    for _, pieces in units:
        chunks, off = [], 0
        for k_ref, v_ref in pieces:
            for st in range(0, k_ref.shape[0], ATT_KC):
                sz = min(ATT_KC, k_ref.shape[0] - st)
                chunks.append((k_ref, v_ref, st, sz, off))
                off += sz
        chunk_lists.append(chunks)
    nchunk = len(chunk_lists[0])
    run_max = [None] * len(units)
    row_max = [None] * len(units)
    acc = [None] * len(units)

    def scores(u, ci):
        k_ref, _, st, sz, off = chunk_lists[u][ci]
        s = _dot_nt(units[u][0], k_ref[st:st + sz, :].astype(BF16))
        s_ref[u, :, off:off + sz] = s
        part = functools.reduce(jnp.maximum, [s[:, j:j + LANES] for j in range(0, sz, LANES)])
        run_max[u] = part if run_max[u] is None else jnp.maximum(run_max[u], part)

    def finish_max(u):
        m = jnp.max(run_max[u], axis=1, keepdims=True)
        row_max[u] = jnp.broadcast_to(m, (rows, LANES))

    def weighted(u, ci):
        _, v_ref, st, sz, off = chunk_lists[u][ci]
        e = jnp.concatenate([jnp.exp2(s_ref[u, :, off + j:off + j + LANES] - row_max[u])
                             for j in range(0, sz, LANES)], axis=1).astype(BF16)
        v1 = jnp.concatenate([v_ref[st:st + sz, :].astype(BF16), jnp.ones((sz, LANES), BF16)], axis=1)
        d = _dot(e, v1)
        acc[u] = d if acc[u] is None else acc[u] + d

    for ci in range(nchunk):
        scores(0, ci)
    finish_max(0)
    for u in range(1, len(units)):
        for ci in range(nchunk):
            weighted(u - 1, ci)
            scores(u, ci)
        finish_max(u)
    for ci in range(nchunk):
        weighted(len(units) - 1, ci)
    return [a[:, :LANES] / a[:, LANES:] for a in acc]


def _gqa_query(qpad):
    return jnp.concatenate([qpad[:, :LANES], qpad[:, LANES:]], axis=0)


def _gqa_output(o, group):
    tq = o.shape[0] // 2
    o0, o1 = o[:tq], o[tq:]
    low = lax.broadcasted_iota(jnp.int32, (tq, LANES), 1) < HEAD_DIM
    first = group == 0
    r0, r1 = pltpu.roll(o0, HEAD_DIM, 1), pltpu.roll(o1, HEAD_DIM, 1)
    return jnp.where(low, jnp.where(first, o0, r0), jnp.where(first, r1, o1))


def _diff_query(q):
    low = lax.broadcasted_iota(jnp.int32, q.shape, 1) < B_QK_DIM
    zero = jnp.zeros_like(q)
    return jnp.concatenate([jnp.where(low, q, zero), jnp.where(low, zero, q)], axis=0)


def _diff_output(o, lam, g_sub, lam_init):
    tq = o.shape[0] // 2
    w = o[:tq] - lam * o[tq:]
    y = w * lax.rsqrt(jnp.mean(w * w, axis=-1, keepdims=True) + RMS_EPS)
    return y * g_sub * (1.0 - lam_init)


def _lambda(lp):
    s1 = jnp.sum(lp[0:1] * lp[1:2], axis=1, keepdims=True)
    s2 = jnp.sum(lp[2:3] * lp[3:4], axis=1, keepdims=True)
    return jnp.exp(s1) - jnp.exp(s2)


def _mod_kernel(c_ref, w_ref, b_ref, o_ref):
    c = c_ref[...]
    s = (c * jax.nn.sigmoid(c)).astype(BF16)
    o_ref[...] = _dot(s, w_ref[...].astype(BF16)) + b_ref[...]


def _modulation(cvec, w_mod, b_mod):
    out = pl.pallas_call(
        _mod_kernel,
        out_shape=jax.ShapeDtypeStruct((DEPTH, N_MOD, MOD_ROWS, D_MODEL), F32),
        grid=(DEPTH, N_MOD),
        in_specs=[
            pl.BlockSpec((MOD_ROWS, D_MODEL), lambda l, n: (0, 0)),
            pl.BlockSpec((None, D_MODEL, D_MODEL), lambda l, n: (l, 0, n)),
            pl.BlockSpec((None, None, 1, D_MODEL), lambda l, n: (l, n, 0, 0)),
        ],
        out_specs=pl.BlockSpec((None, None, MOD_ROWS, D_MODEL), lambda l, n: (l, n, 0, 0)),
        compiler_params=_cparams("parallel", "parallel"),
        name="modulation",
    )(cvec, w_mod, b_mod.reshape(DEPTH, N_MOD, 1, D_MODEL))
    return out.reshape(DEPTH, N_MOD, MOD_ROWS, 1, D_MODEL)


def _mod_spec(layer, chunk, row0, tm, ngrid):
    def seg(i):
        return row0 + (i * tm) // SEG_LEN
    if ngrid == 1:
        return pl.BlockSpec((None, None, None, 1, D_MODEL), lambda i: (layer, chunk, seg(i), 0, 0))
    return pl.BlockSpec((None, None, None, 1, D_MODEL), lambda i, j: (layer, chunk, seg(i), 0, 0))


def _ffn_kernel(layer, x_ref, sh_ref, sc_ref, gt_ref, wgu_hbm, wd_hbm, lng_ref, lnb_ref, o_ref,
                wg_ref, wu_ref, wd_ref, stage_gu, stage_d, sem, h_ref, acc_ref):
    nf = FFN_DIM // FFN_TF
    first_step = pl.program_id(0) == 0

    def chunk_copies(c, slot):
        col = c * FFN_TF if isinstance(c, int) else pl.multiple_of(c * FFN_TF, FFN_TF)
        return (
            pltpu.make_async_copy(wgu_hbm.at[layer, :, pl.ds(col, FFN_TF)], stage_gu.at[slot, 0], sem.at[slot, 0]),
            pltpu.make_async_copy(wgu_hbm.at[layer, :, pl.ds(FFN_DIM + col, FFN_TF)], stage_gu.at[slot, 1],
                                  sem.at[slot, 1]),
            pltpu.make_async_copy(wd_hbm.at[layer, pl.ds(col, FFN_TF), :], stage_d.at[slot], sem.at[slot, 2]),
        )

    @pl.when(first_step)
    def _():
        for cp in chunk_copies(0, 0):
            cp.start()

    h_ref[...] = (x_ref[...] * (1.0 + sc_ref[...]) + sh_ref[...]).astype(BF16)
    acc_ref[...] = jnp.zeros_like(acc_ref)

    def body(c, carry):
        @pl.when(first_step)
        def _():
            slot = c & 1

            @pl.when(c + 1 < nf)
            def _():
                for cp in chunk_copies(c + 1, 1 - slot):
                    cp.start()

            for cp in chunk_copies(c, slot):
                cp.wait()
            wg_ref[c] = stage_gu[slot, 0].astype(BF16)
            wu_ref[c] = stage_gu[slot, 1].astype(BF16)
            wd_ref[c] = stage_d[slot].astype(BF16)

        h = h_ref[...]
        g = _dot(h, wg_ref[c])
        u = _dot(h, wu_ref[c])
        a = (g * jax.nn.sigmoid(g) * u).astype(BF16)
        acc_ref[...] += _dot(a, wd_ref[c])
        return carry

    lax.fori_loop(0, nf, body, 0)
    y = ALPHA * x_ref[...] + (0.5 * gt_ref[...]) * acc_ref[...]
    o_ref[...] = _layer_norm(y, lng_ref[...], lnb_ref[...])


def _ffn(x, mod, layer, sub, row0, w_gu, w_down, ln_g, ln_b):
    t = x.shape[0]
    nf = FFN_DIM // FFN_TF
    return pl.pallas_call(
        functools.partial(_ffn_kernel, layer),
        out_shape=jax.ShapeDtypeStruct((t, D_MODEL), F32),
        grid=(t // FFN_TM,),
        in_specs=[
            pl.BlockSpec((FFN_TM, D_MODEL), lambda i: (i, 0)),
            _mod_spec(layer, 3 * sub, row0, FFN_TM, 1),
            _mod_spec(layer, 3 * sub + 1, row0, FFN_TM, 1),
            _mod_spec(layer, 3 * sub + 2, row0, FFN_TM, 1),
            pl.BlockSpec(memory_space=pl.ANY),
            pl.BlockSpec(memory_space=pl.ANY),
            pl.BlockSpec((None, None, 1, D_MODEL), lambda i: (layer, sub, 0, 0)),
            pl.BlockSpec((None, None, 1, D_MODEL), lambda i: (layer, sub, 0, 0)),
        ],
        out_specs=pl.BlockSpec((FFN_TM, D_MODEL), lambda i: (i, 0)),
        scratch_shapes=[
            pltpu.VMEM((nf, D_MODEL, FFN_TF), BF16),
            pltpu.VMEM((nf, D_MODEL, FFN_TF), BF16),
            pltpu.VMEM((nf, FFN_TF, D_MODEL), BF16),
            pltpu.VMEM((2, 2, D_MODEL, FFN_TF), F32),
            pltpu.VMEM((2, FFN_TF, D_MODEL), F32),
            pltpu.SemaphoreType.DMA((2, 3)),
            pltpu.VMEM((FFN_TM, D_MODEL), BF16),
            pltpu.VMEM((FFN_TM, D_MODEL), F32),
        ],
        compiler_params=_cparams("arbitrary"),
        name="ffn",
    )(x, mod, mod, mod, w_gu, w_down, ln_g, ln_b)


def _rope(x, cos, sin):
    quarter = HEAD_DIM // 4
    lane = lax.broadcasted_iota(jnp.int32, x.shape, 1)
    first = (lane % (2 * quarter)) < quarter
    rot = jnp.where(first, pltpu.roll(x, LANES - quarter, 1), pltpu.roll(x, quarter, 1))
    return x * cos + rot * sin


def _proj_kernel(rope, x_ref, sh_ref, sc_ref, w_ref, gq_ref, gk_ref, pm_ref, *rest):
    if rope:
        cos_ref, sin_ref, p_ref, uc_ref, wb_ref = rest
    else:
        p_ref, uc_ref, nak_ref, nav_ref, nbk_ref, nbv_ref, wb_ref = rest

    @pl.when(pl.program_id(0) == 0)
    def _():
        wb_ref[...] = w_ref[...].astype(BF16)

    h = (x_ref[...] * (1.0 + sc_ref[...]) + sh_ref[...]).astype(BF16)
    proj = _dot(h, wb_ref[...])
    tm = proj.shape[0]

    qa = proj[:, 0:256]
    qa = qa * lax.rsqrt(_dot_exact(qa * qa, pm_ref[...]) + RMS_EPS) * gq_ref[...]
    ka = proj[:, 256:384]
    ka = ka * lax.rsqrt(_dot_exact(ka * ka, pm_ref[0:LANES, 0:LANES]) + RMS_EPS) * gk_ref[...]
    va = proj[:, 384:512]
    qb = proj[:, 512:1024]
    kb = proj[:, 1024:1536]
    vb = proj[:, 1536:2048]
    uc_ref[0] = proj[:, 2048:2048 + LANES]
    uc_ref[1] = proj[:, 2048 + LANES:2304]

    if not rope:
        nak_ref[...] = ka
        nav_ref[...] = va
        nbk_ref[...] = kb
        nbv_ref[...] = vb

    def slabs(a):
        return [a[:, s:s + LANES] for s in range(0, a.shape[1], LANES)]

    if rope:
        cos, sin = cos_ref[...], sin_ref[...]
        qa_s = [_rope(s, cos, sin) for s in slabs(qa)]
        ka_s = [_rope(ka, cos, sin)]
        qb_s = [_rope(s, cos, sin) for s in slabs(qb)]
        kb_s = [_rope(s, cos, sin) for s in slabs(kb)]
    else:
        qa_s, ka_s, qb_s, kb_s = slabs(qa), [ka], slabs(qb), slabs(kb)

    scale = LOG2E * HEAD_DIM ** -0.5
    low = lax.broadcasted_iota(jnp.int32, (tm, LANES), 1) < HEAD_DIM
    zero = jnp.zeros((tm, LANES), F32)
    g0, g1 = qa_s[0] * scale, qa_s[1] * scale
    qa_pad = [jnp.where(low, g0, zero), jnp.where(low, pltpu.roll(g0, HEAD_DIM, 1), zero),
              jnp.where(low, zero, pltpu.roll(g1, HEAD_DIM, 1)), jnp.where(low, zero, g1)]
    pieces = qa_pad + ka_s + [va] + [s * scale for s in qb_s] + kb_s + slabs(vb)
    for n, piece in enumerate(pieces):
        p_ref[:, n * LANES:(n + 1) * LANES] = piece.astype(BF16)


def _project(x, mod, layer, row0, w_in, gq, gk, pmean, rope_tabs):
    t = x.shape[0]
    rope = rope_tabs is not None
    nt = t // PROJ_TM
    in_specs = [
        pl.BlockSpec((PROJ_TM, D_MODEL), lambda i: (i, 0)),
        _mod_spec(layer, 3, row0, PROJ_TM, 1),
        _mod_spec(layer, 4, row0, PROJ_TM, 1),
        pl.BlockSpec((None, D_MODEL, IN_WIDTH), lambda i: (layer, 0, 0)),
        pl.BlockSpec((None, 1, 256), lambda i: (layer, 0, 0)),
        pl.BlockSpec((None, 1, LANES), lambda i: (layer, 0, 0)),
        pl.BlockSpec((256, 256), lambda i: (0, 0)),
    ]
    args = [x, mod, mod, w_in, gq, gk, pmean]
    out_shape = [jax.ShapeDtypeStruct((t, PROJ_W), BF16), jax.ShapeDtypeStruct((2, t, LANES), F32)]
    out_specs = [pl.BlockSpec((PROJ_TM, PROJ_W), lambda i: (i, 0)),
                 pl.BlockSpec((2, PROJ_TM, LANES), lambda i: (0, i, 0))]
    if rope:
        per_seg = SEG_LEN // PROJ_TM
        in_specs += [pl.BlockSpec((PROJ_TM, LANES), lambda i: (i % per_seg, 0))] * 2
        args += list(rope_tabs)
    else:
        for width in (LANES, LANES, 512, 512):
            out_shape.append(jax.ShapeDtypeStruct((t, width), F32))
            out_specs.append(pl.BlockSpec((PROJ_TM, width), lambda i: (i, 0)))
    return pl.pallas_call(
        functools.partial(_proj_kernel, rope),
        out_shape=out_shape,
        grid=(nt,),
        in_specs=in_specs,
        out_specs=out_specs,
        scratch_shapes=[pltpu.VMEM((D_MODEL, IN_WIDTH), BF16)],
        compiler_params=_cparams("arbitrary"),
        name="project_latent" if rope else "project_context",
    )(*args)


def _ctx_attn_kernel(lam_init, p_ref, lp_ref, gs_ref, oa_ref, ob_ref, s_ref):
    ka = p_ref.at[:, KA_OFF:KA_OFF + LANES]
    va = p_ref.at[:, VA_OFF:VA_OFF + LANES]
    units = []
    for g in range(A_KV_HEADS):
        units.append((_gqa_query(p_ref[:, QA_OFF + 256 * g:QA_OFF + 256 * (g + 1)]), [(ka, va)]))
    for h in range(B_HEADS):
        k = p_ref.at[:, KB_OFF + h * LANES:KB_OFF + (h + 1) * LANES]
        v = p_ref.at[:, VB_OFF + h * LANES:VB_OFF + (h + 1) * LANES]
        units.append((_diff_query(p_ref[:, QB_OFF + h * LANES:QB_OFF + (h + 1) * LANES]), [(k, v)]))
    outs = _softmax_pv_units(units, s_ref)
    lam = _lambda(lp_ref[...]) + lam_init
    for g in range(A_KV_HEADS):
        oa_ref[:, g * LANES:(g + 1) * LANES] = _gqa_output(outs[g], g).astype(BF16)
    for h in range(B_HEADS):
        o = _diff_output(outs[A_KV_HEADS + h], lam, gs_ref[...], lam_init)
        ob_ref[:, h * LANES:(h + 1) * LANES] = o.astype(BF16)


def _context_attention(p, layer, lam_params, g_subln, lam_init, seq):
    t = p.shape[0]
    return pl.pallas_call(
        functools.partial(_ctx_attn_kernel, lam_init),
        out_shape=[jax.ShapeDtypeStruct((t, 256), BF16), jax.ShapeDtypeStruct((t, 512), BF16)],
        grid=(t // seq,),
        in_specs=[
            pl.BlockSpec((seq, PROJ_W), lambda i: (i, 0)),
            pl.BlockSpec((None, 4, B_QK_DIM), lambda i: (layer, 0, 0)),
            pl.BlockSpec((None, 1, B_V_DIM), lambda i: (layer, 0, 0)),
        ],
        out_specs=[pl.BlockSpec((seq, 256), lambda i: (i, 0)), pl.BlockSpec((seq, 512), lambda i: (i, 0))],
        scratch_shapes=[pltpu.VMEM((A_KV_HEADS + B_HEADS, 2 * seq, seq), F32)],
        compiler_params=_cparams("parallel"),
        name="context_attention",
    )(p, lam_params, g_subln)


def _lat_gqa_kernel(q_ref, kn_ref, vn_ref, kc_ref, vc_ref, o_ref, s_ref):
    pieces = [(kc_ref, vc_ref), (kn_ref, vn_ref)]
    units = [(_gqa_query(q_ref[u * ATT_TQ:(u + 1) * ATT_TQ, :]), pieces) for u in range(ATT_UNITS)]
    outs = _softmax_pv_units(units, s_ref)
    for u in range(ATT_UNITS):
        o_ref[u * ATT_TQ:(u + 1) * ATT_TQ, :] = _gqa_output(outs[u], pl.program_id(1)).astype(BF16)


def _latent_gqa(p, cache_k, cache_v, layer, batch, length):
    tile = ATT_TQ * ATT_UNITS
    nq = length // tile
    past = cache_k.shape[2]
    return pl.pallas_call(
        _lat_gqa_kernel,
        out_shape=jax.ShapeDtypeStruct((batch * length, 256), BF16),
        grid=(batch, A_KV_HEADS, nq),
        in_specs=[
            pl.BlockSpec((tile, 256), lambda b, g, i: (b * nq + i, QA_OFF // 256 + g)),
            pl.BlockSpec((length, LANES), lambda b, g, i: (b, KA_OFF // LANES)),
            pl.BlockSpec((length, LANES), lambda b, g, i: (b, VA_OFF // LANES)),
            pl.BlockSpec((None, None, past, LANES), lambda b, g, i: (b, layer, 0, 0)),
            pl.BlockSpec((None, None, past, LANES), lambda b, g, i: (b, layer, 0, 0)),
        ],
        out_specs=pl.BlockSpec((tile, LANES), lambda b, g, i: (b * nq + i, g)),
        scratch_shapes=[pltpu.VMEM((ATT_UNITS, 2 * ATT_TQ, past + length), F32)],
        compiler_params=_cparams("parallel", "parallel", "parallel"),
        name="latent_gqa",
    )(p, p, p, cache_k, cache_v)


def _lat_diff_kernel(lam_init, q_ref, kn_ref, vn_ref, kc_ref, vc_ref, lp_ref, gs_ref, o_ref, s_ref):
    pieces = [(kc_ref, vc_ref), (kn_ref, vn_ref)]
    units = [(_diff_query(q_ref[u * ATT_TQ:(u + 1) * ATT_TQ, :]), pieces) for u in range(ATT_UNITS)]
    outs = _softmax_pv_units(units, s_ref)
    lam = _lambda(lp_ref[...]) + lam_init
    for u in range(ATT_UNITS):
        o = _diff_output(outs[u], lam, gs_ref[...], lam_init)
        o_ref[u * ATT_TQ:(u + 1) * ATT_TQ, :] = o.astype(BF16)


def _latent_diff(p, cache_k, cache_v, layer, lam_params, g_subln, lam_init, batch, length):
    tile = ATT_TQ * ATT_UNITS
    nq = length // tile
    past = cache_k.shape[2]
    return pl.pallas_call(
        functools.partial(_lat_diff_kernel, lam_init),
        out_shape=jax.ShapeDtypeStruct((batch * length, 512), BF16),
        grid=(batch, B_HEADS, nq),
        in_specs=[
            pl.BlockSpec((tile, LANES), lambda b, h, i: (b * nq + i, QB_OFF // LANES + h)),
            pl.BlockSpec((length, LANES), lambda b, h, i: (b, KB_OFF // LANES + h)),
            pl.BlockSpec((length, LANES), lambda b, h, i: (b, VB_OFF // LANES + h)),
            pl.BlockSpec((None, None, past, LANES), lambda b, h, i: (b, layer, 0, h)),
            pl.BlockSpec((None, None, past, LANES), lambda b, h, i: (b, layer, 0, h)),
            pl.BlockSpec((None, 4, B_QK_DIM), lambda b, h, i: (layer, 0, 0)),
            pl.BlockSpec((None, 1, B_V_DIM), lambda b, h, i: (layer, 0, 0)),
        ],
        out_specs=pl.BlockSpec((tile, LANES), lambda b, h, i: (b * nq + i, h)),
        scratch_shapes=[pltpu.VMEM((ATT_UNITS, 2 * ATT_TQ, past + length), F32)],
        compiler_params=_cparams("parallel", "parallel", "parallel"),
        name="latent_diff",
    )(p, p, p, cache_k, cache_v, lam_params, g_subln)


def _ctx_fourier_kernel(norm, u_ref, bcs_ref, cs_ref, o_ref):
    u = jnp.concatenate([u_ref[0], u_ref[1]], axis=1).astype(BF16)
    bcs = bcs_ref[...]
    rhs = jnp.concatenate([_dot(u, bcs[:C_WIDTH]), _dot(u, bcs[C_WIDTH:])], axis=0).astype(BF16)
    f = _dot(cs_ref[...], rhs) * norm
    o_ref[0] = f[:, :LANES]
    o_ref[1] = f[:, LANES:]


def _context_fourier(uc, bcs, cs, seq):
    t = uc.shape[1]
    norm = 1.0 / math.sqrt(seq * C_GROUP_DIM)
    return pl.pallas_call(
        functools.partial(_ctx_fourier_kernel, norm),
        out_shape=jax.ShapeDtypeStruct((2, t, LANES), F32),
        grid=(t // seq,),
        in_specs=[
            pl.BlockSpec((2, seq, LANES), lambda i: (0, i, 0)),
            pl.BlockSpec((2 * C_WIDTH, C_WIDTH), lambda i: (0, 0)),
            pl.BlockSpec((seq, 2 * seq), lambda i: (0, 0)),
        ],
        out_specs=pl.BlockSpec((2, seq, LANES), lambda i: (0, i, 0)),
        compiler_params=_cparams("parallel"),
        name="context_fourier",
    )(uc, bcs, cs)


def _lat_fourier_kernel(norm, u_ref, m1_ref, m2_ref, bcs_ref, tc_ref, ts_ref, o_ref, z_ref):
    n1 = GRID_W

    def strided(ref, plane, start):
        rows = pl.ds(start, n1, stride=n1)
        return jnp.concatenate([ref[(*plane, 0, rows, slice(None))], ref[(*plane, 1, rows, slice(None))]], axis=1)

    m1 = m1_ref[...]
    for b in range(n1):
        ub = strided(u_ref, (), b).astype(BF16)
        y = _dot(m1, ub)
        yr, yi = y[:n1], y[n1:]
        tc = jnp.concatenate([tc_ref[b * n1:(b + 1) * n1, :]] * 2, axis=1)
        ts = jnp.concatenate([ts_ref[b * n1:(b + 1) * n1, :]] * 2, axis=1)
        zr = yr * tc + yi * ts
        zi = yi * tc - yr * ts
        for half in range(2):
            z_ref[0, half, b * n1:(b + 1) * n1, :] = zr[:, half * LANES:(half + 1) * LANES]
            z_ref[1, half, b * n1:(b + 1) * n1, :] = zi[:, half * LANES:(half + 1) * LANES]
    m2 = m2_ref[...]
    bcs = bcs_ref[...]
    for q in range(n1):
        z = jnp.concatenate([strided(z_ref, (0,), q), strided(z_ref, (1,), q)], axis=0).astype(BF16)
        g = _dot(m2, z)
        gcat = jnp.concatenate([g[:n1], g[n1:]], axis=1).astype(BF16)
        f = _dot(gcat, bcs) * norm
        o_ref[0, pl.ds(q, n1, stride=n1), :] = f[:, :LANES]
        o_ref[1, pl.ds(q, n1, stride=n1), :] = f[:, LANES:]


def _latent_fourier(uc, m1, m2, bcs, tc, ts, length):
    t = uc.shape[1]
    norm = 1.0 / math.sqrt(length * C_GROUP_DIM)
    return pl.pallas_call(
        functools.partial(_lat_fourier_kernel, norm),
        out_shape=jax.ShapeDtypeStruct((2, t, LANES), F32),
        grid=(t // length,),
        in_specs=[
            pl.BlockSpec((2, length, LANES), lambda i: (0, i, 0)),
            pl.BlockSpec((2 * GRID_W, GRID_W), lambda i: (0, 0)),
            pl.BlockSpec((2 * GRID_W, 2 * GRID_W), lambda i: (0, 0)),
            pl.BlockSpec((2 * C_WIDTH, C_WIDTH), lambda i: (0, 0)),
            pl.BlockSpec((length, LANES), lambda i: (0, 0)),
            pl.BlockSpec((length, LANES), lambda i: (0, 0)),
        ],
        out_specs=pl.BlockSpec((2, length, LANES), lambda i: (0, i, 0)),
        scratch_shapes=[pltpu.VMEM((2, 2, length, LANES), F32)],
        compiler_params=_cparams("parallel"),
        name="latent_fourier",
    )(uc, m1, m2, bcs, tc, ts)


def _out_kernel(x_ref, gt_ref, oa_ref, ob_ref, f_ref, wf_ref, wo_ref, lng_ref, lnb_ref, o_ref, wob_ref):
    @pl.when(pl.program_id(0) == 0)
    def _():
        wob_ref[...] = wo_ref[...].astype(BF16)

    f = jnp.concatenate([f_ref[0], f_ref[1]], axis=1).astype(BF16)
    oc = _dot(f, wf_ref[...].astype(BF16)).astype(BF16)
    cat = jnp.concatenate([oa_ref[...], ob_ref[...], oc], axis=1)
    mo = _dot(cat, wob_ref[...])
    y = ALPHA * x_ref[...] + gt_ref[...] * mo
    o_ref[...] = _layer_norm(y, lng_ref[...], lnb_ref[...])


def _out_project(x, mod, layer, row0, oa, ob, f, w_fourier, w_out, ln_g, ln_b):
    t = x.shape[0]
    return pl.pallas_call(
        _out_kernel,
        out_shape=jax.ShapeDtypeStruct((t, D_MODEL), F32),
        grid=(t // OUT_TM,),
        in_specs=[
            pl.BlockSpec((OUT_TM, D_MODEL), lambda i: (i, 0)),
            _mod_spec(layer, 5, row0, OUT_TM, 1),
            pl.BlockSpec((OUT_TM, 256), lambda i: (i, 0)),
            pl.BlockSpec((OUT_TM, 512), lambda i: (i, 0)),
            pl.BlockSpec((2, OUT_TM, LANES), lambda i: (0, i, 0)),
            pl.BlockSpec((None, C_WIDTH, C_WIDTH), lambda i: (layer, 0, 0)),
            pl.BlockSpec((None, D_MODEL, D_MODEL), lambda i: (layer, 0, 0)),
            pl.BlockSpec((None, None, 1, D_MODEL), lambda i: (layer, 1, 0, 0)),
            pl.BlockSpec((None, None, 1, D_MODEL), lambda i: (layer, 1, 0, 0)),
        ],
        out_specs=pl.BlockSpec((OUT_TM, D_MODEL), lambda i: (i, 0)),
        scratch_shapes=[pltpu.VMEM((D_MODEL, D_MODEL), BF16)],
        compiler_params=_cparams("arbitrary"),
        name="out_project",
    )(x, mod, oa, ob, f, w_fourier, w_out, ln_g, ln_b)


def kernel(x_prompt, x_sample, cache_a_k, cache_a_v, cache_b_k, cache_b_v, c, c_ctx, w_mod, b_mod, w_in, g_qa, g_ka, lam_q1, lam_k1, lam_q2, lam_k2, g_subln, w_fourier, w_out, w_ffn1_gu, w_ffn1_down, w_ffn2_gu, w_ffn2_down, ln_g, ln_b):
    batch, seq, d = x_prompt.shape
    dec_batch, dec_seq, _ = x_sample.shape
    past = cache_a_k.shape[2]
    assert d == D_MODEL and dec_seq == SEG_LEN and batch * seq == SEG_LEN and dec_batch == 2

    cvec = jnp.concatenate([c_ctx[None, :], c, jnp.zeros((MOD_ROWS - 1 - dec_batch, d), F32)], axis=0)
    mod = _modulation(cvec, w_mod, b_mod)

    gq = jnp.tile(g_qa, (1, A_HEADS)).reshape(DEPTH, 1, A_HEADS * HEAD_DIM)
    gk = jnp.tile(g_ka, (1, A_KV_HEADS)).reshape(DEPTH, 1, A_KV_HEADS * HEAD_DIM)
    gs = g_subln.reshape(DEPTH, 1, B_V_DIM)
    lam_params = jnp.stack([lam_q1, lam_k1, lam_q2, lam_k2], axis=1)
    ln_g4 = ln_g.reshape(DEPTH, 3, 1, d)
    ln_b4 = ln_b.reshape(DEPTH, 3, 1, d)
    pmean = _group_mean_matrix(256, HEAD_DIM)
    rope_tabs = _rope_tables(dec_seq)
    bcs = _channel_dft().astype(BF16)
    cs_ctx = _position_dft_small(seq).astype(BF16)
    m1, m2, tw_cos, tw_sin = _two_stage_dft(GRID_W)
    m1, m2 = m1.astype(BF16), m2.astype(BF16)
    ck_a = cache_a_k.reshape(dec_batch, DEPTH, past, A_KV_HEADS * HEAD_DIM)
    cv_a = cache_a_v.reshape(dec_batch, DEPTH, past, A_KV_HEADS * HEAD_DIM)
    ck_b = cache_b_k.reshape(dec_batch, DEPTH, past, B_HEADS * 2 * B_QK_DIM)
    cv_b = cache_b_v.reshape(dec_batch, DEPTH, past, B_HEADS * B_V_DIM)

    xp = x_prompt.reshape(batch * seq, d)
    xs = x_sample.reshape(dec_batch * dec_seq, d)
    new = []
    for layer in range(DEPTH):
        lam_init = 0.8 - 0.6 * math.exp(-0.3 * layer)
        xp = _ffn(xp, mod, layer, 0, 0, w_ffn1_gu, w_ffn1_down, ln_g4, ln_b4)
        pp, ucp, nak, nav, nbk, nbv = _project(xp, mod, layer, 0, w_in, gq, gk, pmean, None)
        new.append((nak, nav, nbk, nbv))
        oa, ob = _context_attention(pp, layer, lam_params, gs, lam_init, seq)
        fo = _context_fourier(ucp, bcs, cs_ctx, seq)
        xp = _out_project(xp, mod, layer, 0, oa, ob, fo, w_fourier, w_out, ln_g4, ln_b4)
        xp = _ffn(xp, mod, layer, 2, 0, w_ffn2_gu, w_ffn2_down, ln_g4, ln_b4)
        xs = _ffn(xs, mod, layer, 0, 1, w_ffn1_gu, w_ffn1_down, ln_g4, ln_b4)
        ps, ucs = _project(xs, mod, layer, 1, w_in, gq, gk, pmean, rope_tabs)
        oa = _latent_gqa(ps, ck_a, cv_a, layer, dec_batch, dec_seq)
        ob = _latent_diff(ps, ck_b, cv_b, layer, lam_params, gs, lam_init, dec_batch, dec_seq)
        fo = _latent_fourier(ucs, m1, m2, bcs, tw_cos, tw_sin, dec_seq)
        xs = _out_project(xs, mod, layer, 1, oa, ob, fo, w_fourier, w_out, ln_g4, ln_b4)
        xs = _ffn(xs, mod, layer, 2, 1, w_ffn2_gu, w_ffn2_down, ln_g4, ln_b4)

    def stack(idx, heads, width):
        return jnp.stack([n[idx].reshape(batch, seq, heads, width) for n in new], axis=1)

    return (xp.reshape(batch, seq, d), xs.reshape(dec_batch, dec_seq, d),
            stack(0, A_KV_HEADS, HEAD_DIM), stack(1, A_KV_HEADS, HEAD_DIM),
            stack(2, B_HEADS, 2 * B_QK_DIM), stack(3, B_HEADS, B_V_DIM))
```

```python
import functools
import math

import numpy as np
import jax
import jax.numpy as jnp
from jax import lax
from jax.experimental import pallas as pl
from jax.experimental.pallas import tpu as pltpu

F32 = jnp.float32
BF16 = jnp.bfloat16

D_MODEL = 1024
DEPTH = 2
GRID_W = 64
HEAD_DIM = 64
ROPE_THETA = 10000.0
A_HEADS = 4
A_KV_HEADS = 2
B_HEADS = 4
B_QK_DIM = 64
B_V_DIM = 128
C_GROUPS = 4
C_GROUP_DIM = 64
C_WIDTH = 256
FFN_DIM = 2816
N_MOD = 9
IN_WIDTH = 2304
ALPHA = (2.0 * DEPTH) ** 0.25
LN_EPS = 1e-5
RMS_EPS = 1e-6

LANES = 128
MOD_ROWS = 8
SEG_LEN = 4096
VMEM_LIMIT = 52 << 20

QA_OFF, KA_OFF, VA_OFF, QB_OFF, KB_OFF, VB_OFF = 0, 512, 640, 768, 1280, 1792
PROJ_W = 2304

FFN_TM = 512
FFN_TF = 256
PROJ_TM = 512
OUT_TM = 1024
ATT_TQ = 256
ATT_UNITS = 2
ATT_KC = 512
LOG2E = math.log2(math.e)


def _cparams(*sem):
    return pltpu.CompilerParams(dimension_semantics=sem, vmem_limit_bytes=VMEM_LIMIT)


def _rope_tables(length):
    rows = length // GRID_W
    row = np.repeat(np.arange(rows), GRID_W).astype(np.float64)
    col = np.tile(np.arange(GRID_W), rows).astype(np.float64)
    half = HEAD_DIM // 2
    inv = 1.0 / (ROPE_THETA ** (np.arange(0, half, 2, dtype=np.float64) / half))
    ar = row[:, None] * inv
    ac = col[:, None] * inv
    cos = np.concatenate([np.cos(ar), np.cos(ar), np.cos(ac), np.cos(ac)], -1)
    sin = np.concatenate([-np.sin(ar), np.sin(ar), -np.sin(ac), np.sin(ac)], -1)
    return (jnp.asarray(np.tile(cos, (1, 2)), F32), jnp.asarray(np.tile(sin, (1, 2)), F32))


def _group_mean_matrix(width, group):
    m = np.kron(np.eye(width // group), np.full((group, group), 1.0 / group))
    return jnp.asarray(m, F32)


def _channel_dft():
    c = np.arange(C_GROUP_DIM)
    ang = 2.0 * np.pi * np.outer(c, c) / C_GROUP_DIM
    eye = np.eye(C_GROUPS)
    return jnp.asarray(np.concatenate([np.kron(eye, np.cos(ang)), np.kron(eye, np.sin(ang))], 0), F32)


def _position_dft_small(length):
    n = np.arange(length)
    ang = 2.0 * np.pi * ((np.outer(n, n)) % length) / length
    return jnp.asarray(np.concatenate([np.cos(ang), -np.sin(ang)], 1), F32)


def _two_stage_dft(n1):
    i = np.arange(n1)
    ang = 2.0 * np.pi * np.outer(i, i) / n1
    c, s = np.cos(ang), np.sin(ang)
    m1 = np.concatenate([c, -s], 0)
    m2 = np.block([[c, s], [-s, c]])
    tw = 2.0 * np.pi * np.outer(i, i).reshape(-1) / (n1 * n1)
    tc = np.repeat(np.cos(tw)[:, None], LANES, 1)
    ts = np.repeat(np.sin(tw)[:, None], LANES, 1)
    return tuple(jnp.asarray(a, F32) for a in (m1, m2, tc, ts))


def _layer_norm(y, g, b):
    mu = jnp.mean(y, axis=-1, keepdims=True)
    yc = y - mu
    var = jnp.mean(yc * yc, axis=-1, keepdims=True)
    return yc * lax.rsqrt(var + LN_EPS) * g + b


def _dot(a, b):
    return jnp.dot(a, b, preferred_element_type=F32)


def _dot_nt(a, b):
    return lax.dot_general(a, b, (((1,), (1,)), ((), ())), preferred_element_type=F32)


def _dot_split(a, b):
    hi = a.astype(BF16)
    lo = (a - hi.astype(F32)).astype(BF16)
    return _dot(hi, b) + _dot(lo, b)


def _softmax_pv_units(units, s_ref):
    rows = units[0][0].shape[0]
    chunk_lists = []
    for _, pieces in units:
        chunks, off = [], 0
        for k_ref, v_ref in pieces:
            for st in range(0, k_ref.shape[0], ATT_KC):
                sz = min(ATT_KC, k_ref.shape[0] - st)
                chunks.append((k_ref, v_ref, st, sz, off))
                off += sz
        chunk_lists.append(chunks)
    nchunk = len(chunk_lists[0])
    run_max = [None] * len(units)
    row_max = [None] * len(units)
    acc = [None] * len(units)

    def scores(u, ci):
        k_ref, _, st, sz, off = chunk_lists[u][ci]
        s = _dot_nt(units[u][0], k_ref[st:st + sz, :].astype(BF16))
        s_ref[u, :, off:off + sz] = s
        part = functools.reduce(jnp.maximum, [s[:, j:j + LANES] for j in range(0, sz, LANES)])
        run_max[u] = part if run_max[u] is None else jnp.maximum(run_max[u], part)

    def finish_max(u):
        m = jnp.max(run_max[u], axis=1, keepdims=True)
        row_max[u] = jnp.broadcast_to(m, (rows, LANES))

    def weighted(u, ci):
        _, v_ref, st, sz, off = chunk_lists[u][ci]
        e = jnp.concatenate([jnp.exp2(s_ref[u, :, off + j:off + j + LANES] - row_max[u])
                             for j in range(0, sz, LANES)], axis=1).astype(BF16)
        v1 = jnp.concatenate([v_ref[st:st + sz, :].astype(BF16), jnp.ones((sz, LANES), BF16)], axis=1)
        d = _dot(e, v1)
        acc[u] = d if acc[u] is None else acc[u] + d

    for ci in range(nchunk):
        scores(0, ci)
    finish_max(0)
    for u in range(1, len(units)):
        for ci in range(nchunk):
            weighted(u - 1, ci)
            scores(u, ci)
        finish_max(u)
    for ci in range(nchunk):
        weighted(len(units) - 1, ci)
    return [a[:, :LANES] / a[:, LANES:] for a in acc]


def _gqa_query(qpad):
    return jnp.concatenate([qpad[:, :LANES], qpad[:, LANES:]], axis=0)


def _gqa_output(o, group):
    tq = o.shape[0] // 2
    o0, o1 = o[:tq], o[tq:]
    low = lax.broadcasted_iota(jnp.int32, (tq, LANES), 1) < HEAD_DIM
    first = group == 0
    r0, r1 = pltpu.roll(o0, HEAD_DIM, 1), pltpu.roll(o1, HEAD_DIM, 1)
    return jnp.where(low, jnp.where(first, o0, r0), jnp.where(first, r1, o1))


def _diff_query(q):
    low = lax.broadcasted_iota(jnp.int32, q.shape, 1) < B_QK_DIM
    zero = jnp.zeros_like(q)
    return jnp.concatenate([jnp.where(low, q, zero), jnp.where(low, zero, q)], axis=0)


def _diff_output(o, lam, g_sub, lam_init):
    tq = o.shape[0] // 2
    w = o[:tq] - lam * o[tq:]
    y = w * lax.rsqrt(jnp.mean(w * w, axis=-1, keepdims=True) + RMS_EPS)
    return y * g_sub * (1.0 - lam_init)


def _lambda(lp):
    s1 = jnp.sum(lp[0:1] * lp[1:2], axis=1, keepdims=True)
    s2 = jnp.sum(lp[2:3] * lp[3:4], axis=1, keepdims=True)
    return jnp.exp(s1) - jnp.exp(s2)


def _mod_kernel(c_ref, w_ref, b_ref, o_ref):
    c = c_ref[...]
    s = (c * jax.nn.sigmoid(c)).astype(BF16)
    o_ref[...] = _dot(s, w_ref[...].astype(BF16)) + b_ref[...]


def _modulation(cvec, w_mod, b_mod):
    out = pl.pallas_call(
        _mod_kernel,
        out_shape=jax.ShapeDtypeStruct((DEPTH, N_MOD, MOD_ROWS, D_MODEL), F32),
        grid=(DEPTH, N_MOD),
        in_specs=[
            pl.BlockSpec((MOD_ROWS, D_MODEL), lambda l, n: (0, 0)),
            pl.BlockSpec((None, D_MODEL, D_MODEL), lambda l, n: (l, 0, n)),
            pl.BlockSpec((None, None, 1, D_MODEL), lambda l, n: (l, n, 0, 0)),
        ],
        out_specs=pl.BlockSpec((None, None, MOD_ROWS, D_MODEL), lambda l, n: (l, n, 0, 0)),
        compiler_params=_cparams("parallel", "parallel"),
        name="modulation",
    )(cvec, w_mod, b_mod.reshape(DEPTH, N_MOD, 1, D_MODEL))
    return out.reshape(DEPTH, N_MOD, MOD_ROWS, 1, D_MODEL)


def _mod_spec(layer, chunk, row0, tm, ngrid):
    def seg(i):
        return row0 + (i * tm) // SEG_LEN
    if ngrid == 1:
        return pl.BlockSpec((None, None, None, 1, D_MODEL), lambda i: (layer, chunk, seg(i), 0, 0))
    return pl.BlockSpec((None, None, None, 1, D_MODEL), lambda i, j: (layer, chunk, seg(i), 0, 0))


def _ffn_kernel(layer, x_ref, sh_ref, sc_ref, gt_ref, wgu_hbm, wd_hbm, lng_ref, lnb_ref, o_ref,
                wg_ref, wu_ref, wd_ref, stage_gu, stage_d, sem, h_ref, acc_ref):
    nf = FFN_DIM // FFN_TF
    first_step = pl.program_id(0) == 0

    def chunk_copies(c, slot):
        col = c * FFN_TF if isinstance(c, int) else pl.multiple_of(c * FFN_TF, FFN_TF)
        return (
            pltpu.make_async_copy(wgu_hbm.at[layer, :, pl.ds(col, FFN_TF)], stage_gu.at[slot, 0], sem.at[slot, 0]),
            pltpu.make_async_copy(wgu_hbm.at[layer, :, pl.ds(FFN_DIM + col, FFN_TF)], stage_gu.at[slot, 1],
                                  sem.at[slot, 1]),
            pltpu.make_async_copy(wd_hbm.at[layer, pl.ds(col, FFN_TF), :], stage_d.at[slot], sem.at[slot, 2]),
        )

    def modulated():
        return (x_ref[...] * (1.0 + sc_ref[...]) + sh_ref[...]).astype(BF16)

    def chunk(h, c):
        g = _dot(h, wg_ref[c])
        u = _dot(h, wu_ref[c])
        a = (g * jax.nn.sigmoid(g) * u).astype(BF16)
        return _dot(a, wd_ref[c])

    def finish(acc):
        y = ALPHA * x_ref[...] + (0.5 * gt_ref[...]) * acc
        o_ref[...] = _layer_norm(y, lng_ref[...], lnb_ref[...])

    @pl.when(first_step)
    def _():
        for cp in chunk_copies(0, 0):
            cp.start()
        h_ref[...] = modulated()
        acc_ref[...] = jnp.zeros_like(acc_ref)

        def body(c, carry):
            slot = c & 1

            @pl.when(c + 1 < nf)
            def _():
                for cp in chunk_copies(c + 1, 1 - slot):
                    cp.start()

            for cp in chunk_copies(c, slot):
                cp.wait()
            wg_ref[c] = stage_gu[slot, 0].astype(BF16)
            wu_ref[c] = stage_gu[slot, 1].astype(BF16)
            wd_ref[c] = stage_d[slot].astype(BF16)
            acc_ref[...] += chunk(h_ref[...], c)
            return carry

        lax.fori_loop(0, nf, body, 0)
        finish(acc_ref[...])

    @pl.when(jnp.logical_not(first_step))
    def _():
        h = modulated()
        acc = chunk(h, 0)
        for c in range(1, nf):
            acc = acc + chunk(h, c)
        finish(acc)


def _ffn(x, mod, layer, sub, row0, w_gu, w_down, ln_g, ln_b):
    t = x.shape[0]
    nf = FFN_DIM // FFN_TF
    return pl.pallas_call(
        functools.partial(_ffn_kernel, layer),
        out_shape=jax.ShapeDtypeStruct((t, D_MODEL), F32),
        grid=(t // FFN_TM,),
        in_specs=[
            pl.BlockSpec((FFN_TM, D_MODEL), lambda i: (i, 0)),
            _mod_spec(layer, 3 * sub, row0, FFN_TM, 1),
            _mod_spec(layer, 3 * sub + 1, row0, FFN_TM, 1),
            _mod_spec(layer, 3 * sub + 2, row0, FFN_TM, 1),
            pl.BlockSpec(memory_space=pl.ANY),
            pl.BlockSpec(memory_space=pl.ANY),
            pl.BlockSpec((None, None, 1, D_MODEL), lambda i: (layer, sub, 0, 0)),
            pl.BlockSpec((None, None, 1, D_MODEL), lambda i: (layer, sub, 0, 0)),
        ],
        out_specs=pl.BlockSpec((FFN_TM, D_MODEL), lambda i: (i, 0)),
        scratch_shapes=[
            pltpu.VMEM((nf, D_MODEL, FFN_TF), BF16),
            pltpu.VMEM((nf, D_MODEL, FFN_TF), BF16),
            pltpu.VMEM((nf, FFN_TF, D_MODEL), BF16),
            pltpu.VMEM((2, 2, D_MODEL, FFN_TF), F32),
            pltpu.VMEM((2, FFN_TF, D_MODEL), F32),
            pltpu.SemaphoreType.DMA((2, 3)),
            pltpu.VMEM((FFN_TM, D_MODEL), BF16),
            pltpu.VMEM((FFN_TM, D_MODEL), F32),
        ],
        compiler_params=_cparams("arbitrary"),
        name="ffn",
    )(x, mod, mod, mod, w_gu, w_down, ln_g, ln_b)


def _rope(x, cos, sin):
    quarter = HEAD_DIM // 4
    lane = lax.broadcasted_iota(jnp.int32, x.shape, 1)
    first = (lane % (2 * quarter)) < quarter
    rot = jnp.where(first, pltpu.roll(x, LANES - quarter, 1), pltpu.roll(x, quarter, 1))
    return x * cos + rot * sin


def _proj_kernel(rope, x_ref, sh_ref, sc_ref, w_ref, gq_ref, gk_ref, pm_ref, *rest):
    if rope:
        cos_ref, sin_ref, p_ref, uc_ref, wb_ref = rest
    else:
        p_ref, uc_ref, nak_ref, nav_ref, nbk_ref, nbv_ref, wb_ref = rest

    @pl.when(pl.program_id(0) == 0)
    def _():
        wb_ref[...] = w_ref[...].astype(BF16)

    h = (x_ref[...] * (1.0 + sc_ref[...]) + sh_ref[...]).astype(BF16)
    proj = _dot(h, wb_ref[...])
    tm = proj.shape[0]

    qa = proj[:, 0:256]
    qa = qa * lax.rsqrt(_dot_split(qa * qa, pm_ref[...]) + RMS_EPS) * gq_ref[...]
    ka = proj[:, 256:384]
    ka = ka * lax.rsqrt(_dot_split(ka * ka, pm_ref[0:LANES, 0:LANES]) + RMS_EPS) * gk_ref[...]
    va = proj[:, 384:512]
    qb = proj[:, 512:1024]
    kb = proj[:, 1024:1536]
    vb = proj[:, 1536:2048]
    uc_ref[0] = proj[:, 2048:2048 + LANES]
    uc_ref[1] = proj[:, 2048 + LANES:2304]

    if not rope:
        nak_ref[...] = ka
        nav_ref[...] = va
        nbk_ref[...] = kb
        nbv_ref[...] = vb

    def slabs(a):
        return [a[:, s:s + LANES] for s in range(0, a.shape[1], LANES)]

    if rope:
        cos, sin = cos_ref[...], sin_ref[...]
        qa_s = [_rope(s, cos, sin) for s in slabs(qa)]
        ka_s = [_rope(ka, cos, sin)]
        qb_s = [_rope(s, cos, sin) for s in slabs(qb)]
        kb_s = [_rope(s, cos, sin) for s in slabs(kb)]
    else:
        qa_s, ka_s, qb_s, kb_s = slabs(qa), [ka], slabs(qb), slabs(kb)

    scale = LOG2E * HEAD_DIM ** -0.5
    low = lax.broadcasted_iota(jnp.int32, (tm, LANES), 1) < HEAD_DIM
    zero = jnp.zeros((tm, LANES), F32)
    g0, g1 = qa_s[0] * scale, qa_s[1] * scale
    qa_pad = [jnp.where(low, g0, zero), jnp.where(low, pltpu.roll(g0, HEAD_DIM, 1), zero),
              jnp.where(low, zero, pltpu.roll(g1, HEAD_DIM, 1)), jnp.where(low, zero, g1)]
    pieces = qa_pad + ka_s + [va] + [s * scale for s in qb_s] + kb_s + slabs(vb)
    for n, piece in enumerate(pieces):
        p_ref[:, n * LANES:(n + 1) * LANES] = piece.astype(BF16)


def _project(x, mod, layer, row0, w_in, gq, gk, pmean, rope_tabs):
    t = x.shape[0]
    rope = rope_tabs is not None
    nt = t // PROJ_TM
    in_specs = [
        pl.BlockSpec((PROJ_TM, D_MODEL), lambda i: (i, 0)),
        _mod_spec(layer, 3, row0, PROJ_TM, 1),
        _mod_spec(layer, 4, row0, PROJ_TM, 1),
        pl.BlockSpec((None, D_MODEL, IN_WIDTH), lambda i: (layer, 0, 0)),
        pl.BlockSpec((None, 1, 256), lambda i: (layer, 0, 0)),
        pl.BlockSpec((None, 1, LANES), lambda i: (layer, 0, 0)),
        pl.BlockSpec((256, 256), lambda i: (0, 0)),
    ]
    args = [x, mod, mod, w_in, gq, gk, pmean]
    out_shape = [jax.ShapeDtypeStruct((t, PROJ_W), BF16), jax.ShapeDtypeStruct((2, t, LANES), F32)]
    out_specs = [pl.BlockSpec((PROJ_TM, PROJ_W), lambda i: (i, 0)),
                 pl.BlockSpec((2, PROJ_TM, LANES), lambda i: (0, i, 0))]
    if rope:
        per_seg = SEG_LEN // PROJ_TM
        in_specs += [pl.BlockSpec((PROJ_TM, LANES), lambda i: (i % per_seg, 0))] * 2
        args += list(rope_tabs)
    else:
        for width in (LANES, LANES, 512, 512):
            out_shape.append(jax.ShapeDtypeStruct((t, width), F32))
            out_specs.append(pl.BlockSpec((PROJ_TM, width), lambda i: (i, 0)))
    return pl.pallas_call(
        functools.partial(_proj_kernel, rope),
        out_shape=out_shape,
        grid=(nt,),
        in_specs=in_specs,
        out_specs=out_specs,
        scratch_shapes=[pltpu.VMEM((D_MODEL, IN_WIDTH), BF16)],
        compiler_params=_cparams("arbitrary"),
        name="project_latent" if rope else "project_context",
    )(*args)


def _ctx_attn_kernel(lam_init, p_ref, lp_ref, gs_ref, oa_ref, ob_ref, s_ref):
    ka = p_ref.at[:, KA_OFF:KA_OFF + LANES]
    va = p_ref.at[:, VA_OFF:VA_OFF + LANES]
    units = []
    for g in range(A_KV_HEADS):
        units.append((_gqa_query(p_ref[:, QA_OFF + 256 * g:QA_OFF + 256 * (g + 1)]), [(ka, va)]))
    for h in range(B_HEADS):
        k = p_ref.at[:, KB_OFF + h * LANES:KB_OFF + (h + 1) * LANES]
        v = p_ref.at[:, VB_OFF + h * LANES:VB_OFF + (h + 1) * LANES]
        units.append((_diff_query(p_ref[:, QB_OFF + h * LANES:QB_OFF + (h + 1) * LANES]), [(k, v)]))
    outs = _softmax_pv_units(units, s_ref)
    lam = _lambda(lp_ref[...]) + lam_init
    for g in range(A_KV_HEADS):
        oa_ref[:, g * LANES:(g + 1) * LANES] = _gqa_output(outs[g], g).astype(BF16)
    for h in range(B_HEADS):
        o = _diff_output(outs[A_KV_HEADS + h], lam, gs_ref[...], lam_init)
        ob_ref[:, h * LANES:(h + 1) * LANES] = o.astype(BF16)


def _context_attention(p, layer, lam_params, g_subln, lam_init, seq):
    t = p.shape[0]
    return pl.pallas_call(
        functools.partial(_ctx_attn_kernel, lam_init),
        out_shape=[jax.ShapeDtypeStruct((t, 256), BF16), jax.ShapeDtypeStruct((t, 512), BF16)],
        grid=(t // seq,),
        in_specs=[
            pl.BlockSpec((seq, PROJ_W), lambda i: (i, 0)),
            pl.BlockSpec((None, 4, B_QK_DIM), lambda i: (layer, 0, 0)),
            pl.BlockSpec((None, 1, B_V_DIM), lambda i: (layer, 0, 0)),
        ],
        out_specs=[pl.BlockSpec((seq, 256), lambda i: (i, 0)), pl.BlockSpec((seq, 512), lambda i: (i, 0))],
        scratch_shapes=[pltpu.VMEM((A_KV_HEADS + B_HEADS, 2 * seq, seq), F32)],
        compiler_params=_cparams("parallel"),
        name="context_attention",
    )(p, lam_params, g_subln)


def _lat_gqa_kernel(q_ref, kn_ref, vn_ref, kc_ref, vc_ref, o_ref, s_ref):
    pieces = [(kc_ref, vc_ref), (kn_ref, vn_ref)]
    units = [(_gqa_query(q_ref[u * ATT_TQ:(u + 1) * ATT_TQ, :]), pieces) for u in range(ATT_UNITS)]
    outs = _softmax_pv_units(units, s_ref)
    for u in range(ATT_UNITS):
        o_ref[u * ATT_TQ:(u + 1) * ATT_TQ, :] = _gqa_output(outs[u], pl.program_id(1)).astype(BF16)


def _latent_gqa(p, cache_k, cache_v, layer, batch, length):
    tile = ATT_TQ * ATT_UNITS
    nq = length // tile
    past = cache_k.shape[2]
    return pl.pallas_call(
        _lat_gqa_kernel,
        out_shape=jax.ShapeDtypeStruct((batch * length, 256), BF16),
        grid=(batch, A_KV_HEADS, nq),
        in_specs=[
            pl.BlockSpec((tile, 256), lambda b, g, i: (b * nq + i, QA_OFF // 256 + g)),
            pl.BlockSpec((length, LANES), lambda b, g, i: (b, KA_OFF // LANES)),
            pl.BlockSpec((length, LANES), lambda b, g, i: (b, VA_OFF // LANES)),
            pl.BlockSpec((None, None, past, LANES), lambda b, g, i: (b, layer, 0, 0)),
            pl.BlockSpec((None, None, past, LANES), lambda b, g, i: (b, layer, 0, 0)),
        ],
        out_specs=pl.BlockSpec((tile, LANES), lambda b, g, i: (b * nq + i, g)),
        scratch_shapes=[pltpu.VMEM((ATT_UNITS, 2 * ATT_TQ, past + length), F32)],
        compiler_params=_cparams("parallel", "parallel", "parallel"),
        name="latent_gqa",
    )(p, p, p, cache_k, cache_v)


def _lat_diff_kernel(lam_init, q_ref, kn_ref, vn_ref, kc_ref, vc_ref, lp_ref, gs_ref, o_ref, s_ref):
    pieces = [(kc_ref, vc_ref), (kn_ref, vn_ref)]
    units = [(_diff_query(q_ref[u * ATT_TQ:(u + 1) * ATT_TQ, :]), pieces) for u in range(ATT_UNITS)]
    outs = _softmax_pv_units(units, s_ref)
    lam = _lambda(lp_ref[...]) + lam_init
    for u in range(ATT_UNITS):
        o = _diff_output(outs[u], lam, gs_ref[...], lam_init)
        o_ref[u * ATT_TQ:(u + 1) * ATT_TQ, :] = o.astype(BF16)


def _latent_diff(p, cache_k, cache_v, layer, lam_params, g_subln, lam_init, batch, length):
    tile = ATT_TQ * ATT_UNITS
    nq = length // tile
    past = cache_k.shape[2]
    return pl.pallas_call(
        functools.partial(_lat_diff_kernel, lam_init),
        out_shape=jax.ShapeDtypeStruct((batch * length, 512), BF16),
        grid=(batch, B_HEADS, nq),
        in_specs=[
            pl.BlockSpec((tile, LANES), lambda b, h, i: (b * nq + i, QB_OFF // LANES + h)),
            pl.BlockSpec((length, LANES), lambda b, h, i: (b, KB_OFF // LANES + h)),
            pl.BlockSpec((length, LANES), lambda b, h, i: (b, VB_OFF // LANES + h)),
            pl.BlockSpec((None, None, past, LANES), lambda b, h, i: (b, layer, 0, h)),
            pl.BlockSpec((None, None, past, LANES), lambda b, h, i: (b, layer, 0, h)),
            pl.BlockSpec((None, 4, B_QK_DIM), lambda b, h, i: (layer, 0, 0)),
            pl.BlockSpec((None, 1, B_V_DIM), lambda b, h, i: (layer, 0, 0)),
        ],
        out_specs=pl.BlockSpec((tile, LANES), lambda b, h, i: (b * nq + i, h)),
        scratch_shapes=[pltpu.VMEM((ATT_UNITS, 2 * ATT_TQ, past + length), F32)],
        compiler_params=_cparams("parallel", "parallel", "parallel"),
        name="latent_diff",
    )(p, p, p, cache_k, cache_v, lam_params, g_subln)


def _ctx_fourier_kernel(norm, u_ref, bcs_ref, cs_ref, o_ref):
    u = jnp.concatenate([u_ref[0], u_ref[1]], axis=1).astype(BF16)
    bcs = bcs_ref[...]
    rhs = jnp.concatenate([_dot(u, bcs[:C_WIDTH]), _dot(u, bcs[C_WIDTH:])], axis=0).astype(BF16)
    f = _dot(cs_ref[...], rhs) * norm
    o_ref[0] = f[:, :LANES]
    o_ref[1] = f[:, LANES:]


def _context_fourier(uc, bcs, cs, seq):
    t = uc.shape[1]
    norm = 1.0 / math.sqrt(seq * C_GROUP_DIM)
    return pl.pallas_call(
        functools.partial(_ctx_fourier_kernel, norm),
        out_shape=jax.ShapeDtypeStruct((2, t, LANES), F32),
        grid=(t // seq,),
        in_specs=[
            pl.BlockSpec((2, seq, LANES), lambda i: (0, i, 0)),
            pl.BlockSpec((2 * C_WIDTH, C_WIDTH), lambda i: (0, 0)),
            pl.BlockSpec((seq, 2 * seq), lambda i: (0, 0)),
        ],
        out_specs=pl.BlockSpec((2, seq, LANES), lambda i: (0, i, 0)),
        compiler_params=_cparams("parallel"),
        name="context_fourier",
    )(uc, bcs, cs)


def _lat_fourier_kernel(norm, u_ref, m1_ref, m2_ref, bcs_ref, tc_ref, ts_ref, o_ref, z_ref):
    n1 = GRID_W

    def strided(ref, plane, start):
        rows = pl.ds(start, n1, stride=n1)
        return jnp.concatenate([ref[(*plane, 0, rows, slice(None))], ref[(*plane, 1, rows, slice(None))]], axis=1)

    m1 = m1_ref[...]
    for b in range(n1):
        ub = strided(u_ref, (), b).astype(BF16)
        y = _dot(m1, ub)
        yr, yi = y[:n1], y[n1:]
        tc = jnp.concatenate([tc_ref[b * n1:(b + 1) * n1, :]] * 2, axis=1)
        ts = jnp.concatenate([ts_ref[b * n1:(b + 1) * n1, :]] * 2, axis=1)
        zr = yr * tc + yi * ts
        zi = yi * tc - yr * ts
        for half in range(2):
            z_ref[0, half, b * n1:(b + 1) * n1, :] = zr[:, half * LANES:(half + 1) * LANES]
            z_ref[1, half, b * n1:(b + 1) * n1, :] = zi[:, half * LANES:(half + 1) * LANES]
    m2 = m2_ref[...]
    bcs = bcs_ref[...]
    for q in range(n1):
        z = jnp.concatenate([strided(z_ref, (0,), q), strided(z_ref, (1,), q)], axis=0).astype(BF16)
        g = _dot(m2, z)
        gcat = jnp.concatenate([g[:n1], g[n1:]], axis=1).astype(BF16)
        f = _dot(gcat, bcs) * norm
        o_ref[0, pl.ds(q, n1, stride=n1), :] = f[:, :LANES]
        o_ref[1, pl.ds(q, n1, stride=n1), :] = f[:, LANES:]


def _latent_fourier(uc, m1, m2, bcs, tc, ts, length):
    t = uc.shape[1]
    norm = 1.0 / math.sqrt(length * C_GROUP_DIM)
    return pl.pallas_call(
        functools.partial(_lat_fourier_kernel, norm),
        out_shape=jax.ShapeDtypeStruct((2, t, LANES), F32),
        grid=(t // length,),
        in_specs=[
            pl.BlockSpec((2, length, LANES), lambda i: (0, i, 0)),
            pl.BlockSpec((2 * GRID_W, GRID_W), lambda i: (0, 0)),
            pl.BlockSpec((2 * GRID_W, 2 * GRID_W), lambda i: (0, 0)),
            pl.BlockSpec((2 * C_WIDTH, C_WIDTH), lambda i: (0, 0)),
            pl.BlockSpec((length, LANES), lambda i: (0, 0)),
            pl.BlockSpec((length, LANES), lambda i: (0, 0)),
        ],
        out_specs=pl.BlockSpec((2, length, LANES), lambda i: (0, i, 0)),
        scratch_shapes=[pltpu.VMEM((2, 2, length, LANES), F32)],
        compiler_params=_cparams("parallel"),
        name="latent_fourier",
    )(uc, m1, m2, bcs, tc, ts)


def _out_kernel(x_ref, gt_ref, oa_ref, ob_ref, f_ref, wf_ref, wo_ref, lng_ref, lnb_ref, o_ref, wob_ref):
    @pl.when(pl.program_id(0) == 0)
    def _():
        wob_ref[...] = wo_ref[...].astype(BF16)

    wf = wf_ref[...].astype(BF16)
    half = OUT_TM // 2
    for r in range(0, OUT_TM, half):
        rows = slice(r, r + half)
        f = jnp.concatenate([f_ref[0, rows, :], f_ref[1, rows, :]], axis=1).astype(BF16)
        oc = _dot(f, wf).astype(BF16)
        cat = jnp.concatenate([oa_ref[rows, :], ob_ref[rows, :], oc], axis=1)
        mo = _dot(cat, wob_ref[...])
        y = ALPHA * x_ref[rows, :] + gt_ref[...] * mo
        o_ref[rows, :] = _layer_norm(y, lng_ref[...], lnb_ref[...])


def _out_project(x, mod, layer, row0, oa, ob, f, w_fourier, w_out, ln_g, ln_b):
    t = x.shape[0]
    return pl.pallas_call(
        _out_kernel,
        out_shape=jax.ShapeDtypeStruct((t, D_MODEL), F32),
        grid=(t // OUT_TM,),
        in_specs=[
            pl.BlockSpec((OUT_TM, D_MODEL), lambda i: (i, 0)),
            _mod_spec(layer, 5, row0, OUT_TM, 1),
            pl.BlockSpec((OUT_TM, 256), lambda i: (i, 0)),
            pl.BlockSpec((OUT_TM, 512), lambda i: (i, 0)),
            pl.BlockSpec((2, OUT_TM, LANES), lambda i: (0, i, 0)),
            pl.BlockSpec((None, C_WIDTH, C_WIDTH), lambda i: (layer, 0, 0)),
            pl.BlockSpec((None, D_MODEL, D_MODEL), lambda i: (layer, 0, 0)),
            pl.BlockSpec((None, None, 1, D_MODEL), lambda i: (layer, 1, 0, 0)),
            pl.BlockSpec((None, None, 1, D_MODEL), lambda i: (layer, 1, 0, 0)),
        ],
        out_specs=pl.BlockSpec((OUT_TM, D_MODEL), lambda i: (i, 0)),
        scratch_shapes=[pltpu.VMEM((D_MODEL, D_MODEL), BF16)],
        compiler_params=_cparams("arbitrary"),
        name="out_project",
    )(x, mod, oa, ob, f, w_fourier, w_out, ln_g, ln_b)


def kernel(x_prompt, x_sample, cache_a_k, cache_a_v, cache_b_k, cache_b_v, c, c_ctx, w_mod, b_mod, w_in, g_qa, g_ka, lam_q1, lam_k1, lam_q2, lam_k2, g_subln, w_fourier, w_out, w_ffn1_gu, w_ffn1_down, w_ffn2_gu, w_ffn2_down, ln_g, ln_b):
    batch, seq, d = x_prompt.shape
    dec_batch, dec_seq, _ = x_sample.shape
    past = cache_a_k.shape[2]
    assert d == D_MODEL and dec_seq == SEG_LEN and batch * seq == SEG_LEN and dec_batch == 2

    cvec = jnp.concatenate([c_ctx[None, :], c, jnp.zeros((MOD_ROWS - 1 - dec_batch, d), F32)], axis=0)
    mod = _modulation(cvec, w_mod, b_mod)

    gq = jnp.tile(g_qa, (1, A_HEADS)).reshape(DEPTH, 1, A_HEADS * HEAD_DIM)
    gk = jnp.tile(g_ka, (1, A_KV_HEADS)).reshape(DEPTH, 1, A_KV_HEADS * HEAD_DIM)
    gs = g_subln.reshape(DEPTH, 1, B_V_DIM)
    lam_params = jnp.stack([lam_q1, lam_k1, lam_q2, lam_k2], axis=1)
    ln_g4 = ln_g.reshape(DEPTH, 3, 1, d)
    ln_b4 = ln_b.reshape(DEPTH, 3, 1, d)
    pmean = _group_mean_matrix(256, HEAD_DIM).astype(BF16)
    rope_tabs = _rope_tables(dec_seq)
    bcs = _channel_dft().astype(BF16)
    cs_ctx = _position_dft_small(seq).astype(BF16)
    m1, m2, tw_cos, tw_sin = _two_stage_dft(GRID_W)
    m1, m2 = m1.astype(BF16), m2.astype(BF16)
    ck_a = cache_a_k.reshape(dec_batch, DEPTH, past, A_KV_HEADS * HEAD_DIM)
    cv_a = cache_a_v.reshape(dec_batch, DEPTH, past, A_KV_HEADS * HEAD_DIM)
    ck_b = cache_b_k.reshape(dec_batch, DEPTH, past, B_HEADS * 2 * B_QK_DIM)
    cv_b = cache_b_v.reshape(dec_batch, DEPTH, past, B_HEADS * B_V_DIM)

    xp = x_prompt.reshape(batch * seq, d)
    xs = x_sample.reshape(dec_batch * dec_seq, d)
    new = []
    for layer in range(DEPTH):
        lam_init = 0.8 - 0.6 * math.exp(-0.3 * layer)
        xp = _ffn(xp, mod, layer, 0, 0, w_ffn1_gu, w_ffn1_down, ln_g4, ln_b4)
        pp, ucp, nak, nav, nbk, nbv = _project(xp, mod, layer, 0, w_in, gq, gk, pmean, None)
        new.append((nak, nav, nbk, nbv))
        oa, ob = _context_attention(pp, layer, lam_params, gs, lam_init, seq)
        fo = _context_fourier(ucp, bcs, cs_ctx, seq)
        xp = _out_project(xp, mod, layer, 0, oa, ob, fo, w_fourier, w_out, ln_g4, ln_b4)
        xp = _ffn(xp, mod, layer, 2, 0, w_ffn2_gu, w_ffn2_down, ln_g4, ln_b4)
        xs = _ffn(xs, mod, layer, 0, 1, w_ffn1_gu, w_ffn1_down, ln_g4, ln_b4)
        ps, ucs = _project(xs, mod, layer, 1, w_in, gq, gk, pmean, rope_tabs)
        oa = _latent_gqa(ps, ck_a, cv_a, layer, dec_batch, dec_seq)
        ob = _latent_diff(ps, ck_b, cv_b, layer, lam_params, gs, lam_init, dec_batch, dec_seq)
        fo = _latent_fourier(ucs, m1, m2, bcs, tw_cos, tw_sin, dec_seq)
        xs = _out_project(xs, mod, layer, 1, oa, ob, fo, w_fourier, w_out, ln_g4, ln_b4)
        xs = _ffn(xs, mod, layer, 2, 1, w_ffn2_gu, w_ffn2_down, ln_g4, ln_b4)

    def stack(idx, heads, width):
        return jnp.stack([n[idx].reshape(batch, seq, heads, width) for n in new], axis=1)

    return (xp.reshape(batch, seq, d), xs.reshape(dec_batch, dec_seq, d),
            stack(0, A_KV_HEADS, HEAD_DIM), stack(1, A_KV_HEADS, HEAD_DIM),
            stack(2, B_HEADS, 2 * B_QK_DIM), stack(3, B_HEADS, B_V_DIM))
```

```python
import functools
import math

import numpy as np
import jax
import jax.numpy as jnp
from jax import lax
from jax.experimental import pallas as pl
from jax.experimental.pallas import tpu as pltpu

F32 = jnp.float32
BF16 = jnp.bfloat16

D_MODEL = 1024
DEPTH = 2
GRID_W = 64
HEAD_DIM = 64
ROPE_THETA = 10000.0
A_HEADS = 4
A_KV_HEADS = 2
B_HEADS = 4
B_QK_DIM = 64
B_V_DIM = 128
C_GROUPS = 4
C_GROUP_DIM = 64
C_WIDTH = 256
FFN_DIM = 2816
N_MOD = 9
IN_WIDTH = 2304
ALPHA = (2.0 * DEPTH) ** 0.25
LN_EPS = 1e-5
RMS_EPS = 1e-6

LANES = 128
MOD_ROWS = 8
SEG_LEN = 4096
VMEM_LIMIT = 52 << 20

QA_OFF, KA_OFF, VA_OFF, QB_OFF, KB_OFF, VB_OFF = 0, 512, 640, 768, 1280, 1792
PROJ_W = 2304
CACHE_WIDTHS = (128, 128, 512, 512)

FFN_TM = 512
FFN_TF = 256
PROJ_TM = 512
OUT_TM = 1024
ATT_TQ = 256
ATT_UNITS = 4
ATT_SLOTS = 2
ATT_KC = 512
LOG2E = math.log2(math.e)


def _cparams(*sem):
    return pltpu.CompilerParams(dimension_semantics=sem, vmem_limit_bytes=VMEM_LIMIT)


def _rope_tables(length):
    rows = length // GRID_W
    row = np.repeat(np.arange(rows), GRID_W).astype(np.float64)
    col = np.tile(np.arange(GRID_W), rows).astype(np.float64)
    half = HEAD_DIM // 2
    inv = 1.0 / (ROPE_THETA ** (np.arange(0, half, 2, dtype=np.float64) / half))
    ar = row[:, None] * inv
    ac = col[:, None] * inv
    cos = np.concatenate([np.cos(ar), np.cos(ar), np.cos(ac), np.cos(ac)], -1)
    sin = np.concatenate([-np.sin(ar), np.sin(ar), -np.sin(ac), np.sin(ac)], -1)
    return (jnp.asarray(np.tile(cos, (1, 2)), F32), jnp.asarray(np.tile(sin, (1, 2)), F32))


def _group_mean_matrix(width, group):
    m = np.kron(np.eye(width // group), np.full((group, group), 1.0 / group))
    return jnp.asarray(m, F32)


def _channel_dft():
    c = np.arange(C_GROUP_DIM)
    ang = 2.0 * np.pi * np.outer(c, c) / C_GROUP_DIM
    eye = np.eye(C_GROUPS)
    return jnp.asarray(np.concatenate([np.kron(eye, np.cos(ang)), np.kron(eye, np.sin(ang))], 0), F32)


def _position_dft_small(length):
    n = np.arange(length)
    ang = 2.0 * np.pi * ((np.outer(n, n)) % length) / length
    return jnp.asarray(np.concatenate([np.cos(ang), -np.sin(ang)], 1), F32)


def _two_stage_dft(n1):
    i = np.arange(n1)
    ang = 2.0 * np.pi * np.outer(i, i) / n1
    c, s = np.cos(ang), np.sin(ang)
    m1 = np.concatenate([c, -s], 0)
    m2 = np.block([[c, s], [-s, c]])
    tw = 2.0 * np.pi * np.outer(i, i).reshape(-1) / (n1 * n1)
    tc = np.repeat(np.cos(tw)[:, None], LANES, 1)
    ts = np.repeat(np.sin(tw)[:, None], LANES, 1)
    return tuple(jnp.asarray(a, F32) for a in (m1, m2, tc, ts))


def _layer_norm(y, g, b):
    mu = jnp.mean(y, axis=-1, keepdims=True)
    yc = y - mu
    var = jnp.mean(yc * yc, axis=-1, keepdims=True)
    return yc * lax.rsqrt(var + LN_EPS) * g + b


def _dot(a, b):
    return jnp.dot(a, b, preferred_element_type=F32)


def _dot_nt(a, b):
    return lax.dot_general(a, b, (((1,), (1,)), ((), ())), preferred_element_type=F32)


def _dot_split(a, b):
    hi = a.astype(BF16)
    lo = (a - hi.astype(F32)).astype(BF16)
    return _dot(hi, b) + _dot(lo, b)


def _softmax_pv_units(units, s_ref):
    rows = units[0][0].shape[0]
    chunk_lists = []
    for _, pieces in units:
        chunks, off = [], 0
        for k_ref, v_ref in pieces:
            for st in range(0, k_ref.shape[0], ATT_KC):
                sz = min(ATT_KC, k_ref.shape[0] - st)
                chunks.append((k_ref, v_ref, st, sz, off))
                off += sz
        chunk_lists.append(chunks)
    nchunk = len(chunk_lists[0])
    run_max = [None] * len(units)
    row_max = [None] * len(units)
    acc = [None] * len(units)

    def scores(u, ci):
        k_ref, _, st, sz, off = chunk_lists[u][ci]
        s = _dot_nt(units[u][0], k_ref[st:st + sz, :].astype(BF16))
        s_ref[u % ATT_SLOTS, :, off:off + sz] = s
        part = functools.reduce(jnp.maximum, [s[:, j:j + LANES] for j in range(0, sz, LANES)])
        run_max[u] = part if run_max[u] is None else jnp.maximum(run_max[u], part)

    def finish_max(u):
        m = jnp.max(run_max[u], axis=1, keepdims=True)
        row_max[u] = jnp.broadcast_to(m, (rows, LANES))

    def weighted(u, ci):
        _, v_ref, st, sz, off = chunk_lists[u][ci]
        e = jnp.concatenate([jnp.exp2(s_ref[u % ATT_SLOTS, :, off + j:off + j + LANES] - row_max[u])
                             for j in range(0, sz, LANES)], axis=1).astype(BF16)
        v1 = jnp.concatenate([v_ref[st:st + sz, :].astype(BF16), jnp.ones((sz, LANES), BF16)], axis=1)
        d = _dot(e, v1)
        acc[u] = d if acc[u] is None else acc[u] + d

    for ci in range(nchunk):
        scores(0, ci)
    finish_max(0)
    for u in range(1, len(units)):
        for ci in range(nchunk):
            weighted(u - 1, ci)
            scores(u, ci)
        finish_max(u)
    for ci in range(nchunk):
        weighted(len(units) - 1, ci)
    return [a[:, :LANES] / a[:, LANES:] for a in acc]


def _gqa_query(qpad):
    return jnp.concatenate([qpad[:, :LANES], qpad[:, LANES:]], axis=0)


def _gqa_output(o, group):
    tq = o.shape[0] // 2
    o0, o1 = o[:tq], o[tq:]
    low = lax.broadcasted_iota(jnp.int32, (tq, LANES), 1) < HEAD_DIM
    first = group == 0
    r0, r1 = pltpu.roll(o0, HEAD_DIM, 1), pltpu.roll(o1, HEAD_DIM, 1)
    return jnp.where(low, jnp.where(first, o0, r0), jnp.where(first, r1, o1))


def _diff_query(q):
    low = lax.broadcasted_iota(jnp.int32, q.shape, 1) < B_QK_DIM
    zero = jnp.zeros_like(q)
    return jnp.concatenate([jnp.where(low, q, zero), jnp.where(low, zero, q)], axis=0)


def _diff_output(o, lam, g_sub, lam_init):
    tq = o.shape[0] // 2
    w = o[:tq] - lam * o[tq:]
    y = w * lax.rsqrt(jnp.mean(w * w, axis=-1, keepdims=True) + RMS_EPS)
    return y * g_sub * (1.0 - lam_init)


def _lambda(lp):
    s1 = jnp.sum(lp[0:1] * lp[1:2], axis=1, keepdims=True)
    s2 = jnp.sum(lp[2:3] * lp[3:4], axis=1, keepdims=True)
    return jnp.exp(s1) - jnp.exp(s2)


def _mod_kernel(c_ref, w_ref, b_ref, o_ref):
    c = c_ref[...]
    s = (c * jax.nn.sigmoid(c)).astype(BF16)
    o_ref[...] = _dot(s, w_ref[...].astype(BF16)) + b_ref[...]


def _modulation(cvec, w_mod, b_mod):
    out = pl.pallas_call(
        _mod_kernel,
        out_shape=jax.ShapeDtypeStruct((DEPTH, N_MOD, MOD_ROWS, D_MODEL), F32),
        grid=(DEPTH, N_MOD),
        in_specs=[
            pl.BlockSpec((MOD_ROWS, D_MODEL), lambda l, n: (0, 0)),
            pl.BlockSpec((None, D_MODEL, D_MODEL), lambda l, n: (l, 0, n)),
            pl.BlockSpec((None, None, 1, D_MODEL), lambda l, n: (l, n, 0, 0)),
        ],
        out_specs=pl.BlockSpec((None, None, MOD_ROWS, D_MODEL), lambda l, n: (l, n, 0, 0)),
        compiler_params=_cparams("parallel", "parallel"),
        name="modulation",
    )(cvec, w_mod, b_mod.reshape(DEPTH, N_MOD, 1, D_MODEL))
    return out.reshape(DEPTH, N_MOD, MOD_ROWS, 1, D_MODEL)


def _mod_spec(layer, chunk, row0, tm, ngrid):
    def seg(i):
        return row0 + (i * tm) // SEG_LEN
    if ngrid == 1:
        return pl.BlockSpec((None, None, None, 1, D_MODEL), lambda i: (layer, chunk, seg(i), 0, 0))
    return pl.BlockSpec((None, None, None, 1, D_MODEL), lambda i, j: (layer, chunk, seg(i), 0, 0))


def _ffn_kernel(layer, x_ref, sh_ref, sc_ref, gt_ref, wgu_hbm, wd_hbm, lng_ref, lnb_ref, o_ref,
                wg_ref, wu_ref, wd_ref, stage_gu, stage_d, sem, h_ref, acc_ref):
    nf = FFN_DIM // FFN_TF
    first_step = pl.program_id(0) == 0

    def chunk_copies(c, slot):
        col = c * FFN_TF if isinstance(c, int) else pl.multiple_of(c * FFN_TF, FFN_TF)
        return (
            pltpu.make_async_copy(wgu_hbm.at[layer, :, pl.ds(col, FFN_TF)], stage_gu.at[slot, 0], sem.at[slot, 0]),
            pltpu.make_async_copy(wgu_hbm.at[layer, :, pl.ds(FFN_DIM + col, FFN_TF)], stage_gu.at[slot, 1],
                                  sem.at[slot, 1]),
            pltpu.make_async_copy(wd_hbm.at[layer, pl.ds(col, FFN_TF), :], stage_d.at[slot], sem.at[slot, 2]),
        )

    def modulated():
        return (x_ref[...] * (1.0 + sc_ref[...]) + sh_ref[...]).astype(BF16)

    def chunk(h, c):
        g = _dot(h, wg_ref[c])
        u = _dot(h, wu_ref[c])
        a = (g * jax.nn.sigmoid(g) * u).astype(BF16)
        return _dot(a, wd_ref[c])

    def finish(acc):
        y = ALPHA * x_ref[...] + (0.5 * gt_ref[...]) * acc
        o_ref[...] = _layer_norm(y, lng_ref[...], lnb_ref[...])

    @pl.when(first_step)
    def _():
        for cp in chunk_copies(0, 0):
            cp.start()
        h_ref[...] = modulated()
        acc_ref[...] = jnp.zeros_like(acc_ref)

        def body(c, carry):
            slot = c & 1

            @pl.when(c + 1 < nf)
            def _():
                for cp in chunk_copies(c + 1, 1 - slot):
                    cp.start()

            for cp in chunk_copies(c, slot):
                cp.wait()
            wg_ref[c] = stage_gu[slot, 0].astype(BF16)
            wu_ref[c] = stage_gu[slot, 1].astype(BF16)
            wd_ref[c] = stage_d[slot].astype(BF16)
            acc_ref[...] += chunk(h_ref[...], c)
            return carry

        lax.fori_loop(0, nf, body, 0)
        finish(acc_ref[...])

    @pl.when(jnp.logical_not(first_step))
    def _():
        h = modulated()
        acc = chunk(h, 0)
        for c in range(1, nf):
            acc = acc + chunk(h, c)
        finish(acc)


def _ffn(x, mod, layer, sub, row0, w_gu, w_down, ln_g, ln_b):
    t = x.shape[0]
    nf = FFN_DIM // FFN_TF
    return pl.pallas_call(
        functools.partial(_ffn_kernel, layer),
        out_shape=jax.ShapeDtypeStruct((t, D_MODEL), F32),
        grid=(t // FFN_TM,),
        in_specs=[
            pl.BlockSpec((FFN_TM, D_MODEL), lambda i: (i, 0)),
            _mod_spec(layer, 3 * sub, row0, FFN_TM, 1),
            _mod_spec(layer, 3 * sub + 1, row0, FFN_TM, 1),
            _mod_spec(layer, 3 * sub + 2, row0, FFN_TM, 1),
            pl.BlockSpec(memory_space=pl.ANY),
            pl.BlockSpec(memory_space=pl.ANY),
            pl.BlockSpec((None, None, 1, D_MODEL), lambda i: (layer, sub, 0, 0)),
            pl.BlockSpec((None, None, 1, D_MODEL), lambda i: (layer, sub, 0, 0)),
        ],
        out_specs=pl.BlockSpec((FFN_TM, D_MODEL), lambda i: (i, 0)),
        scratch_shapes=[
            pltpu.VMEM((nf, D_MODEL, FFN_TF), BF16),
            pltpu.VMEM((nf, D_MODEL, FFN_TF), BF16),
            pltpu.VMEM((nf, FFN_TF, D_MODEL), BF16),
            pltpu.VMEM((2, 2, D_MODEL, FFN_TF), F32),
            pltpu.VMEM((2, FFN_TF, D_MODEL), F32),
            pltpu.SemaphoreType.DMA((2, 3)),
            pltpu.VMEM((FFN_TM, D_MODEL), BF16),
            pltpu.VMEM((FFN_TM, D_MODEL), F32),
        ],
        compiler_params=_cparams("arbitrary"),
        name="ffn",
    )(x, mod, mod, mod, w_gu, w_down, ln_g, ln_b)


def _rope(x, cos, sin):
    quarter = HEAD_DIM // 4
    lane = lax.broadcasted_iota(jnp.int32, x.shape, 1)
    first = (lane % (2 * quarter)) < quarter
    rot = jnp.where(first, pltpu.roll(x, LANES - quarter, 1), pltpu.roll(x, quarter, 1))
    return x * cos + rot * sin


def _proj_kernel(rope, n_earlier, x_ref, sh_ref, sc_ref, w_ref, gq_ref, gk_ref, pm_ref, *rest):
    if rope:
        cos_ref, sin_ref, p_ref, uc_ref, wb_ref = rest
    else:
        earlier, rest = rest[:n_earlier], rest[n_earlier:]
        p_ref, uc_ref, nak_ref, nav_ref, nbk_ref, nbv_ref, wb_ref = rest

    @pl.when(pl.program_id(0) == 0)
    def _():
        wb_ref[...] = w_ref[...].astype(BF16)

    h = (x_ref[...] * (1.0 + sc_ref[...]) + sh_ref[...]).astype(BF16)
    proj = _dot(h, wb_ref[...])
    tm = proj.shape[0]

    qa = proj[:, 0:256]
    qa = qa * lax.rsqrt(_dot_split(qa * qa, pm_ref[...]) + RMS_EPS) * gq_ref[...]
    ka = proj[:, 256:384]
    ka = ka * lax.rsqrt(_dot_split(ka * ka, pm_ref[0:LANES, 0:LANES]) + RMS_EPS) * gk_ref[...]
    va = proj[:, 384:512]
    qb = proj[:, 512:1024]
    kb = proj[:, 1024:1536]
    vb = proj[:, 1536:2048]
    uc_ref[0] = proj[:, 2048:2048 + LANES]
    uc_ref[1] = proj[:, 2048 + LANES:2304]

    if not rope:
        for n, (ref, val) in enumerate(((nak_ref, ka), (nav_ref, va), (nbk_ref, kb), (nbv_ref, vb))):
            if n_earlier:
                layers = [e[...] for e in earlier[n::len(CACHE_WIDTHS)]] + [val]
                for l, v in enumerate(layers):
                    ref[:, l] = v.reshape(ref.shape[0], ref.shape[2], ref.shape[3])
            else:
                ref[...] = val

    def slabs(a):
        return [a[:, s:s + LANES] for s in range(0, a.shape[1], LANES)]

    if rope:
        cos, sin = cos_ref[...], sin_ref[...]
        qa_s = [_rope(s, cos, sin) for s in slabs(qa)]
        ka_s = [_rope(ka, cos, sin)]
        qb_s = [_rope(s, cos, sin) for s in slabs(qb)]
        kb_s = [_rope(s, cos, sin) for s in slabs(kb)]
    else:
        qa_s, ka_s, qb_s, kb_s = slabs(qa), [ka], slabs(qb), slabs(kb)

    scale = LOG2E * HEAD_DIM ** -0.5
    low = lax.broadcasted_iota(jnp.int32, (tm, LANES), 1) < HEAD_DIM
    zero = jnp.zeros((tm, LANES), F32)
    g0, g1 = qa_s[0] * scale, qa_s[1] * scale
    qa_pad = [jnp.where(low, g0, zero), jnp.where(low, pltpu.roll(g0, HEAD_DIM, 1), zero),
              jnp.where(low, zero, pltpu.roll(g1, HEAD_DIM, 1)), jnp.where(low, zero, g1)]
    pieces = qa_pad + ka_s + [va] + [s * scale for s in qb_s] + kb_s + slabs(vb)
    for n, piece in enumerate(pieces):
        p_ref[:, n * LANES:(n + 1) * LANES] = piece.astype(BF16)


def _project(x, mod, layer, row0, w_in, gq, gk, pmean, rope_tabs, seq=None, earlier=()):
    t = x.shape[0]
    rope = rope_tabs is not None
    nt = t // PROJ_TM
    in_specs = [
        pl.BlockSpec((PROJ_TM, D_MODEL), lambda i: (i, 0)),
        _mod_spec(layer, 3, row0, PROJ_TM, 1),
        _mod_spec(layer, 4, row0, PROJ_TM, 1),
        pl.BlockSpec((None, D_MODEL, IN_WIDTH), lambda i: (layer, 0, 0)),
        pl.BlockSpec((None, 1, 256), lambda i: (layer, 0, 0)),
        pl.BlockSpec((None, 1, LANES), lambda i: (layer, 0, 0)),
        pl.BlockSpec((256, 256), lambda i: (0, 0)),
    ]
    args = [x, mod, mod, w_in, gq, gk, pmean]
    out_shape = [jax.ShapeDtypeStruct((t, PROJ_W), BF16), jax.ShapeDtypeStruct((2, t, LANES), F32)]
    out_specs = [pl.BlockSpec((PROJ_TM, PROJ_W), lambda i: (i, 0)),
                 pl.BlockSpec((2, PROJ_TM, LANES), lambda i: (0, i, 0))]
    if rope:
        per_seg = SEG_LEN // PROJ_TM
        in_specs += [pl.BlockSpec((PROJ_TM, LANES), lambda i: (i % per_seg, 0))] * 2
        args += list(rope_tabs)
    else:
        for e, width in zip(earlier, CACHE_WIDTHS * (len(earlier) // len(CACHE_WIDTHS))):
            in_specs.append(pl.BlockSpec((PROJ_TM, width), lambda i: (i, 0)))
            args.append(e)
        for width in CACHE_WIDTHS:
            if earlier:
                out_shape.append(jax.ShapeDtypeStruct((t // seq, DEPTH, seq, width), F32))
                out_specs.append(pl.BlockSpec((PROJ_TM // seq, DEPTH, seq, width), lambda i: (i, 0, 0, 0)))
            else:
                out_shape.append(jax.ShapeDtypeStruct((t, width), F32))
                out_specs.append(pl.BlockSpec((PROJ_TM, width), lambda i: (i, 0)))
    return pl.pallas_call(
        functools.partial(_proj_kernel, rope, len(earlier)),
        out_shape=out_shape,
        grid=(nt,),
        in_specs=in_specs,
        out_specs=out_specs,
        scratch_shapes=[pltpu.VMEM((D_MODEL, IN_WIDTH), BF16)],
        compiler_params=_cparams("arbitrary"),
        name="project_latent" if rope else "project_context",
    )(*args)


def _ctx_attn_kernel(lam_init, p_ref, lp_ref, gs_ref, oa_ref, ob_ref, s_ref):
    ka = p_ref.at[:, KA_OFF:KA_OFF + LANES]
    va = p_ref.at[:, VA_OFF:VA_OFF + LANES]
    units = []
    for g in range(A_KV_HEADS):
        units.append((_gqa_query(p_ref[:, QA_OFF + 256 * g:QA_OFF + 256 * (g + 1)]), [(ka, va)]))
    for h in range(B_HEADS):
        k = p_ref.at[:, KB_OFF + h * LANES:KB_OFF + (h + 1) * LANES]
        v = p_ref.at[:, VB_OFF + h * LANES:VB_OFF + (h + 1) * LANES]
        units.append((_diff_query(p_ref[:, QB_OFF + h * LANES:QB_OFF + (h + 1) * LANES]), [(k, v)]))
    outs = _softmax_pv_units(units, s_ref)
    lam = _lambda(lp_ref[...]) + lam_init
    for g in range(A_KV_HEADS):
        oa_ref[:, g * LANES:(g + 1) * LANES] = _gqa_output(outs[g], g).astype(BF16)
    for h in range(B_HEADS):
        o = _diff_output(outs[A_KV_HEADS + h], lam, gs_ref[...], lam_init)
        ob_ref[:, h * LANES:(h + 1) * LANES] = o.astype(BF16)


def _context_attention(p, layer, lam_params, g_subln, lam_init, seq):
    t = p.shape[0]
    return pl.pallas_call(
        functools.partial(_ctx_attn_kernel, lam_init),
        out_shape=[jax.ShapeDtypeStruct((t, 256), BF16), jax.ShapeDtypeStruct((t, 512), BF16)],
        grid=(t // seq,),
        in_specs=[
            pl.BlockSpec((seq, PROJ_W), lambda i: (i, 0)),
            pl.BlockSpec((None, 4, B_QK_DIM), lambda i: (layer, 0, 0)),
            pl.BlockSpec((None, 1, B_V_DIM), lambda i: (layer, 0, 0)),
        ],
        out_specs=[pl.BlockSpec((seq, 256), lambda i: (i, 0)), pl.BlockSpec((seq, 512), lambda i: (i, 0))],
        scratch_shapes=[pltpu.VMEM((ATT_SLOTS, 2 * seq, seq), F32)],
        compiler_params=_cparams("parallel"),
        name="context_attention",
    )(p, lam_params, g_subln)


def _lat_gqa_kernel(q_ref, kn_ref, vn_ref, kc_ref, vc_ref, o_ref, s_ref):
    pieces = [(kc_ref, vc_ref), (kn_ref, vn_ref)]
    units = [(_gqa_query(q_ref[u * ATT_TQ:(u + 1) * ATT_TQ, :]), pieces) for u in range(ATT_UNITS)]
    outs = _softmax_pv_units(units, s_ref)
    for u in range(ATT_UNITS):
        o_ref[u * ATT_TQ:(u + 1) * ATT_TQ, :] = _gqa_output(outs[u], pl.program_id(1)).astype(BF16)


def _latent_gqa(p, cache_k, cache_v, layer, batch, length):
    tile = ATT_TQ * ATT_UNITS
    nq = length // tile
    past = cache_k.shape[2]
    return pl.pallas_call(
        _lat_gqa_kernel,
        out_shape=jax.ShapeDtypeStruct((batch * length, 256), BF16),
        grid=(batch, A_KV_HEADS, nq),
        in_specs=[
            pl.BlockSpec((tile, 256), lambda b, g, i: (b * nq + i, QA_OFF // 256 + g)),
            pl.BlockSpec((length, LANES), lambda b, g, i: (b, KA_OFF // LANES)),
            pl.BlockSpec((length, LANES), lambda b, g, i: (b, VA_OFF // LANES)),
            pl.BlockSpec((None, None, past, LANES), lambda b, g, i: (b, layer, 0, 0)),
            pl.BlockSpec((None, None, past, LANES), lambda b, g, i: (b, layer, 0, 0)),
        ],
        out_specs=pl.BlockSpec((tile, LANES), lambda b, g, i: (b * nq + i, g)),
        scratch_shapes=[pltpu.VMEM((ATT_SLOTS, 2 * ATT_TQ, past + length), F32)],
        compiler_params=_cparams("parallel", "parallel", "parallel"),
        name="latent_gqa",
    )(p, p, p, cache_k, cache_v)


def _lat_diff_kernel(lam_init, q_ref, kn_ref, vn_ref, kc_ref, vc_ref, lp_ref, gs_ref, o_ref, s_ref):
    pieces = [(kc_ref, vc_ref), (kn_ref, vn_ref)]
    units = [(_diff_query(q_ref[u * ATT_TQ:(u + 1) * ATT_TQ, :]), pieces) for u in range(ATT_UNITS)]
    outs = _softmax_pv_units(units, s_ref)
    lam = _lambda(lp_ref[...]) + lam_init
    for u in range(ATT_UNITS):
        o = _diff_output(outs[u], lam, gs_ref[...], lam_init)
        o_ref[u * ATT_TQ:(u + 1) * ATT_TQ, :] = o.astype(BF16)


def _latent_diff(p, cache_k, cache_v, layer, lam_params, g_subln, lam_init, batch, length):
    tile = ATT_TQ * ATT_UNITS
    nq = length // tile
    past = cache_k.shape[2]
    return pl.pallas_call(
        functools.partial(_lat_diff_kernel, lam_init),
        out_shape=jax.ShapeDtypeStruct((batch * length, 512), BF16),
        grid=(batch, B_HEADS, nq),
        in_specs=[
            pl.BlockSpec((tile, LANES), lambda b, h, i: (b * nq + i, QB_OFF // LANES + h)),
            pl.BlockSpec((length, LANES), lambda b, h, i: (b, KB_OFF // LANES + h)),
            pl.BlockSpec((length, LANES), lambda b, h, i: (b, VB_OFF // LANES + h)),
            pl.BlockSpec((None, None, past, LANES), lambda b, h, i: (b, layer, 0, h)),
            pl.BlockSpec((None, None, past, LANES), lambda b, h, i: (b, layer, 0, h)),
            pl.BlockSpec((None, 4, B_QK_DIM), lambda b, h, i: (layer, 0, 0)),
            pl.BlockSpec((None, 1, B_V_DIM), lambda b, h, i: (layer, 0, 0)),
        ],
        out_specs=pl.BlockSpec((tile, LANES), lambda b, h, i: (b * nq + i, h)),
        scratch_shapes=[pltpu.VMEM((ATT_SLOTS, 2 * ATT_TQ, past + length), F32)],
        compiler_params=_cparams("parallel", "parallel", "parallel"),
        name="latent_diff",
    )(p, p, p, cache_k, cache_v, lam_params, g_subln)


def _ctx_fourier_kernel(norm, u_ref, bcs_ref, cs_ref, o_ref):
    u = jnp.concatenate([u_ref[0], u_ref[1]], axis=1).astype(BF16)
    bcs = bcs_ref[...]
    rhs = jnp.concatenate([_dot(u, bcs[:C_WIDTH]), _dot(u, bcs[C_WIDTH:])], axis=0).astype(BF16)
    f = _dot(cs_ref[...], rhs) * norm
    o_ref[0] = f[:, :LANES]
    o_ref[1] = f[:, LANES:]


def _context_fourier(uc, bcs, cs, seq):
    t = uc.shape[1]
    norm = 1.0 / math.sqrt(seq * C_GROUP_DIM)
    return pl.pallas_call(
        functools.partial(_ctx_fourier_kernel, norm),
        out_shape=jax.ShapeDtypeStruct((2, t, LANES), F32),
        grid=(t // seq,),
        in_specs=[
            pl.BlockSpec((2, seq, LANES), lambda i: (0, i, 0)),
            pl.BlockSpec((2 * C_WIDTH, C_WIDTH), lambda i: (0, 0)),
            pl.BlockSpec((seq, 2 * seq), lambda i: (0, 0)),
        ],
        out_specs=pl.BlockSpec((2, seq, LANES), lambda i: (0, i, 0)),
        compiler_params=_cparams("parallel"),
        name="context_fourier",
    )(uc, bcs, cs)


def _lat_fourier_kernel(norm, u_ref, m1_ref, m2_ref, bcs_ref, tc_ref, ts_ref, o_ref, z_ref):
    n1 = GRID_W

    def strided(ref, plane, start):
        rows = pl.ds(start, n1, stride=n1)
        return jnp.concatenate([ref[(*plane, 0, rows, slice(None))], ref[(*plane, 1, rows, slice(None))]], axis=1)

    m1 = m1_ref[...]
    for b in range(n1):
        ub = strided(u_ref, (), b).astype(BF16)
        y = _dot(m1, ub)
        yr, yi = y[:n1], y[n1:]
        tc = jnp.concatenate([tc_ref[b * n1:(b + 1) * n1, :]] * 2, axis=1)
        ts = jnp.concatenate([ts_ref[b * n1:(b + 1) * n1, :]] * 2, axis=1)
        zr = yr * tc + yi * ts
        zi = yi * tc - yr * ts
        for half in range(2):
            z_ref[0, half, b * n1:(b + 1) * n1, :] = zr[:, half * LANES:(half + 1) * LANES]
            z_ref[1, half, b * n1:(b + 1) * n1, :] = zi[:, half * LANES:(half + 1) * LANES]
    m2 = m2_ref[...]
    bcs = bcs_ref[...]
    for q in range(n1):
        z = jnp.concatenate([strided(z_ref, (0,), q), strided(z_ref, (1,), q)], axis=0).astype(BF16)
        g = _dot(m2, z)
        gcat = jnp.concatenate([g[:n1], g[n1:]], axis=1).astype(BF16)
        f = _dot(gcat, bcs) * norm
        o_ref[0, pl.ds(q, n1, stride=n1), :] = f[:, :LANES]
        o_ref[1, pl.ds(q, n1, stride=n1), :] = f[:, LANES:]


def _latent_fourier(uc, m1, m2, bcs, tc, ts, length):
    t = uc.shape[1]
    norm = 1.0 / math.sqrt(length * C_GROUP_DIM)
    return pl.pallas_call(
        functools.partial(_lat_fourier_kernel, norm),
        out_shape=jax.ShapeDtypeStruct((2, t, LANES), F32),
        grid=(t // length,),
        in_specs=[
            pl.BlockSpec((2, length, LANES), lambda i: (0, i, 0)),
            pl.BlockSpec((2 * GRID_W, GRID_W), lambda i: (0, 0)),
            pl.BlockSpec((2 * GRID_W, 2 * GRID_W), lambda i: (0, 0)),
            pl.BlockSpec((2 * C_WIDTH, C_WIDTH), lambda i: (0, 0)),
            pl.BlockSpec((length, LANES), lambda i: (0, 0)),
            pl.BlockSpec((length, LANES), lambda i: (0, 0)),
        ],
        out_specs=pl.BlockSpec((2, length, LANES), lambda i: (0, i, 0)),
        scratch_shapes=[pltpu.VMEM((2, 2, length, LANES), F32)],
        compiler_params=_cparams("parallel"),
        name="latent_fourier",
    )(uc, m1, m2, bcs, tc, ts)


def _out_kernel(x_ref, gt_ref, oa_ref, ob_ref, f_ref, wf_ref, wo_ref, lng_ref, lnb_ref, o_ref, wob_ref):
    @pl.when(pl.program_id(0) == 0)
    def _():
        wob_ref[...] = wo_ref[...].astype(BF16)

    wf = wf_ref[...].astype(BF16)
    half = OUT_TM // 2
    for r in range(0, OUT_TM, half):
        rows = slice(r, r + half)
        f = jnp.concatenate([f_ref[0, rows, :], f_ref[1, rows, :]], axis=1).astype(BF16)
        oc = _dot(f, wf).astype(BF16)
        cat = jnp.concatenate([oa_ref[rows, :], ob_ref[rows, :], oc], axis=1)
        mo = _dot(cat, wob_ref[...])
        y = ALPHA * x_ref[rows, :] + gt_ref[...] * mo
        o_ref[rows, :] = _layer_norm(y, lng_ref[...], lnb_ref[...])


def _out_project(x, mod, layer, row0, oa, ob, f, w_fourier, w_out, ln_g, ln_b):
    t = x.shape[0]
    return pl.pallas_call(
        _out_kernel,
        out_shape=jax.ShapeDtypeStruct((t, D_MODEL), F32),
        grid=(t // OUT_TM,),
        in_specs=[
            pl.BlockSpec((OUT_TM, D_MODEL), lambda i: (i, 0)),
            _mod_spec(layer, 5, row0, OUT_TM, 1),
            pl.BlockSpec((OUT_TM, 256), lambda i: (i, 0)),
            pl.BlockSpec((OUT_TM, 512), lambda i: (i, 0)),
            pl.BlockSpec((2, OUT_TM, LANES), lambda i: (0, i, 0)),
            pl.BlockSpec((None, C_WIDTH, C_WIDTH), lambda i: (layer, 0, 0)),
            pl.BlockSpec((None, D_MODEL, D_MODEL), lambda i: (layer, 0, 0)),
            pl.BlockSpec((None, None, 1, D_MODEL), lambda i: (layer, 1, 0, 0)),
            pl.BlockSpec((None, None, 1, D_MODEL), lambda i: (layer, 1, 0, 0)),
        ],
        out_specs=pl.BlockSpec((OUT_TM, D_MODEL), lambda i: (i, 0)),
        scratch_shapes=[pltpu.VMEM((D_MODEL, D_MODEL), BF16)],
        compiler_params=_cparams("arbitrary"),
        name="out_project",
    )(x, mod, oa, ob, f, w_fourier, w_out, ln_g, ln_b)


def kernel(x_prompt, x_sample, cache_a_k, cache_a_v, cache_b_k, cache_b_v, c, c_ctx, w_mod, b_mod, w_in, g_qa, g_ka, lam_q1, lam_k1, lam_q2, lam_k2, g_subln, w_fourier, w_out, w_ffn1_gu, w_ffn1_down, w_ffn2_gu, w_ffn2_down, ln_g, ln_b):
    batch, seq, d = x_prompt.shape
    dec_batch, dec_seq, _ = x_sample.shape
    past = cache_a_k.shape[2]
    assert d == D_MODEL and dec_seq == SEG_LEN and batch * seq == SEG_LEN and dec_batch == 2

    cvec = jnp.concatenate([c_ctx[None, :], c, jnp.zeros((MOD_ROWS - 1 - dec_batch, d), F32)], axis=0)
    mod = _modulation(cvec, w_mod, b_mod)

    gq = jnp.tile(g_qa, (1, A_HEADS)).reshape(DEPTH, 1, A_HEADS * HEAD_DIM)
    gk = jnp.tile(g_ka, (1, A_KV_HEADS)).reshape(DEPTH, 1, A_KV_HEADS * HEAD_DIM)
    gs = g_subln.reshape(DEPTH, 1, B_V_DIM)
    lam_params = jnp.stack([lam_q1, lam_k1, lam_q2, lam_k2], axis=1)
    ln_g4 = ln_g.reshape(DEPTH, 3, 1, d)
    ln_b4 = ln_b.reshape(DEPTH, 3, 1, d)
    pmean = _group_mean_matrix(256, HEAD_DIM).astype(BF16)
    rope_tabs = _rope_tables(dec_seq)
    bcs = _channel_dft().astype(BF16)
    cs_ctx = _position_dft_small(seq).astype(BF16)
    m1, m2, tw_cos, tw_sin = _two_stage_dft(GRID_W)
    m1, m2 = m1.astype(BF16), m2.astype(BF16)
    ck_a = cache_a_k.reshape(dec_batch, DEPTH, past, A_KV_HEADS * HEAD_DIM)
    cv_a = cache_a_v.reshape(dec_batch, DEPTH, past, A_KV_HEADS * HEAD_DIM)
    ck_b = cache_b_k.reshape(dec_batch, DEPTH, past, B_HEADS * 2 * B_QK_DIM)
    cv_b = cache_b_v.reshape(dec_batch, DEPTH, past, B_HEADS * B_V_DIM)

    xp = x_prompt.reshape(batch * seq, d)
    xs = x_sample.reshape(dec_batch * dec_seq, d)
    new = []
    for layer in range(DEPTH):
        lam_init = 0.8 - 0.6 * math.exp(-0.3 * layer)
        xp = _ffn(xp, mod, layer, 0, 0, w_ffn1_gu, w_ffn1_down, ln_g4, ln_b4)
        earlier = tuple(new) if layer == DEPTH - 1 else ()
        pp, ucp, *caches = _project(xp, mod, layer, 0, w_in, gq, gk, pmean, None, seq, earlier)
        new = caches if layer == DEPTH - 1 else new + caches
        oa, ob = _context_attention(pp, layer, lam_params, gs, lam_init, seq)
        fo = _context_fourier(ucp, bcs, cs_ctx, seq)
        xp = _out_project(xp, mod, layer, 0, oa, ob, fo, w_fourier, w_out, ln_g4, ln_b4)
        xp = _ffn(xp, mod, layer, 2, 0, w_ffn2_gu, w_ffn2_down, ln_g4, ln_b4)
        xs = _ffn(xs, mod, layer, 0, 1, w_ffn1_gu, w_ffn1_down, ln_g4, ln_b4)
        ps, ucs = _project(xs, mod, layer, 1, w_in, gq, gk, pmean, rope_tabs)
        oa = _latent_gqa(ps, ck_a, cv_a, layer, dec_batch, dec_seq)
        ob = _latent_diff(ps, ck_b, cv_b, layer, lam_params, gs, lam_init, dec_batch, dec_seq)
        fo = _latent_fourier(ucs, m1, m2, bcs, tw_cos, tw_sin, dec_seq)
        xs = _out_project(xs, mod, layer, 1, oa, ob, fo, w_fourier, w_out, ln_g4, ln_b4)
        xs = _ffn(xs, mod, layer, 2, 1, w_ffn2_gu, w_ffn2_down, ln_g4, ln_b4)

    new_a_k, new_a_v, new_b_k, new_b_v = new
    return (xp.reshape(batch, seq, d), xs.reshape(dec_batch, dec_seq, d),
            new_a_k.reshape(batch, DEPTH, seq, A_KV_HEADS, HEAD_DIM),
            new_a_v.reshape(batch, DEPTH, seq, A_KV_HEADS, HEAD_DIM),
            new_b_k.reshape(batch, DEPTH, seq, B_HEADS, 2 * B_QK_DIM),
            new_b_v.reshape(batch, DEPTH, seq, B_HEADS, B_V_DIM))
```

```python
import functools
import math

import numpy as np
import jax
import jax.numpy as jnp
from jax import lax
from jax.experimental import pallas as pl
from jax.experimental.pallas import tpu as pltpu

F32 = jnp.float32
BF16 = jnp.bfloat16

D_MODEL = 1024
DEPTH = 2
GRID_W = 64
HEAD_DIM = 64
ROPE_THETA = 10000.0
A_HEADS = 4
A_KV_HEADS = 2
B_HEADS = 4
B_QK_DIM = 64
B_V_DIM = 128
C_GROUPS = 4
C_GROUP_DIM = 64
C_WIDTH = 256
FFN_DIM = 2816
N_MOD = 9
IN_WIDTH = 2304
ALPHA = (2.0 * DEPTH) ** 0.25
LN_EPS = 1e-5
RMS_EPS = 1e-6

LANES = 128
MOD_ROWS = 8
SEG_LEN = 4096
VMEM_LIMIT = 52 << 20

QA_OFF, KA_OFF, VA_OFF, QB_OFF, KB_OFF, VB_OFF = 0, 512, 640, 768, 1280, 1792
PROJ_W = 2304
CACHE_WIDTHS = (128, 128, 512, 512)

FFN_TM = 512
FFN_TF = 256
FFN_STAGES = 3
PROJ_TM = 512
OUT_TM = 1024
ATT_TQ = 256
ATT_UNITS = 4
ATT_SLOTS = 2
ATT_KC = 512
LOG2E = math.log2(math.e)


def _cparams(*sem):
    return pltpu.CompilerParams(dimension_semantics=sem, vmem_limit_bytes=VMEM_LIMIT)


def _rope_tables(length):
    rows = length // GRID_W
    row = np.repeat(np.arange(rows), GRID_W).astype(np.float64)
    col = np.tile(np.arange(GRID_W), rows).astype(np.float64)
    half = HEAD_DIM // 2
    inv = 1.0 / (ROPE_THETA ** (np.arange(0, half, 2, dtype=np.float64) / half))
    ar = row[:, None] * inv
    ac = col[:, None] * inv
    cos = np.concatenate([np.cos(ar), np.cos(ar), np.cos(ac), np.cos(ac)], -1)
    sin = np.concatenate([-np.sin(ar), np.sin(ar), -np.sin(ac), np.sin(ac)], -1)
    return (jnp.asarray(np.tile(cos, (1, 2)), F32), jnp.asarray(np.tile(sin, (1, 2)), F32))


def _group_mean_matrix(width, group):
    m = np.kron(np.eye(width // group), np.full((group, group), 1.0 / group))
    return jnp.asarray(m, F32)


def _channel_dft():
    c = np.arange(C_GROUP_DIM)
    ang = 2.0 * np.pi * np.outer(c, c) / C_GROUP_DIM
    eye = np.eye(C_GROUPS)
    return jnp.asarray(np.concatenate([np.kron(eye, np.cos(ang)), np.kron(eye, np.sin(ang))], 0), F32)


def _position_dft_small(length):
    n = np.arange(length)
    ang = 2.0 * np.pi * ((np.outer(n, n)) % length) / length
    return jnp.asarray(np.concatenate([np.cos(ang), -np.sin(ang)], 1), F32)


def _two_stage_dft(n1):
    i = np.arange(n1)
    ang = 2.0 * np.pi * np.outer(i, i) / n1
    c, s = np.cos(ang), np.sin(ang)
    m1 = np.concatenate([c, -s], 0)
    m2 = np.block([[c, s], [-s, c]])
    tw = 2.0 * np.pi * np.outer(i, i).reshape(-1) / (n1 * n1)
    tc = np.repeat(np.cos(tw)[:, None], LANES, 1)
    ts = np.repeat(np.sin(tw)[:, None], LANES, 1)
    return tuple(jnp.asarray(a, F32) for a in (m1, m2, tc, ts))


def _layer_norm(y, g, b):
    mu = jnp.mean(y, axis=-1, keepdims=True)
    yc = y - mu
    var = jnp.mean(yc * yc, axis=-1, keepdims=True)
    return yc * lax.rsqrt(var + LN_EPS) * g + b


def _dot(a, b):
    return jnp.dot(a, b, preferred_element_type=F32)


def _dot_nt(a, b):
    return lax.dot_general(a, b, (((1,), (1,)), ((), ())), preferred_element_type=F32)


def _dot_split(a, b):
    hi = a.astype(BF16)
    lo = (a - hi.astype(F32)).astype(BF16)
    return _dot(hi, b) + _dot(lo, b)


def _softmax_pv_units(units, s_ref):
    rows = units[0][0].shape[0]
    chunk_lists = []
    for _, pieces in units:
        chunks, off = [], 0
        for k_ref, v_ref in pieces:
            for st in range(0, k_ref.shape[0], ATT_KC):
                sz = min(ATT_KC, k_ref.shape[0] - st)
                chunks.append((k_ref, v_ref, st, sz, off))
                off += sz
        chunk_lists.append(chunks)
    nchunk = len(chunk_lists[0])
    run_max = [None] * len(units)
    row_max = [None] * len(units)
    acc = [None] * len(units)

    def scores(u, ci):
        k_ref, _, st, sz, off = chunk_lists[u][ci]
        s = _dot_nt(units[u][0], k_ref[st:st + sz, :].astype(BF16))
        s_ref[u % ATT_SLOTS, :, off:off + sz] = s
        part = functools.reduce(jnp.maximum, [s[:, j:j + LANES] for j in range(0, sz, LANES)])
        run_max[u] = part if run_max[u] is None else jnp.maximum(run_max[u], part)

    def finish_max(u):
        m = jnp.max(run_max[u], axis=1, keepdims=True)
        row_max[u] = jnp.broadcast_to(m, (rows, LANES))

    def weighted(u, ci):
        _, v_ref, st, sz, off = chunk_lists[u][ci]
        e = jnp.concatenate([jnp.exp2(s_ref[u % ATT_SLOTS, :, off + j:off + j + LANES] - row_max[u])
                             for j in range(0, sz, LANES)], axis=1).astype(BF16)
        v1 = jnp.concatenate([v_ref[st:st + sz, :].astype(BF16), jnp.ones((sz, LANES), BF16)], axis=1)
        d = _dot(e, v1)
        acc[u] = d if acc[u] is None else acc[u] + d

    for ci in range(nchunk):
        scores(0, ci)
    finish_max(0)
    for u in range(1, len(units)):
        for ci in range(nchunk):
            weighted(u - 1, ci)
            scores(u, ci)
        finish_max(u)
    for ci in range(nchunk):
        weighted(len(units) - 1, ci)
    return [a[:, :LANES] / a[:, LANES:] for a in acc]


def _gqa_query(qpad):
    return jnp.concatenate([qpad[:, :LANES], qpad[:, LANES:]], axis=0)


def _gqa_output(o, group):
    tq = o.shape[0] // 2
    o0, o1 = o[:tq], o[tq:]
    low = lax.broadcasted_iota(jnp.int32, (tq, LANES), 1) < HEAD_DIM
    first = group == 0
    r0, r1 = pltpu.roll(o0, HEAD_DIM, 1), pltpu.roll(o1, HEAD_DIM, 1)
    return jnp.where(low, jnp.where(first, o0, r0), jnp.where(first, r1, o1))


def _diff_query(q):
    low = lax.broadcasted_iota(jnp.int32, q.shape, 1) < B_QK_DIM
    zero = jnp.zeros_like(q)
    return jnp.concatenate([jnp.where(low, q, zero), jnp.where(low, zero, q)], axis=0)


def _diff_output(o, lam, g_sub, lam_init):
    tq = o.shape[0] // 2
    w = o[:tq] - lam * o[tq:]
    y = w * lax.rsqrt(jnp.mean(w * w, axis=-1, keepdims=True) + RMS_EPS)
    return y * g_sub * (1.0 - lam_init)


def _lambda(lp):
    s1 = jnp.sum(lp[0:1] * lp[1:2], axis=1, keepdims=True)
    s2 = jnp.sum(lp[2:3] * lp[3:4], axis=1, keepdims=True)
    return jnp.exp(s1) - jnp.exp(s2)


def _mod_kernel(c_ref, w_ref, b_ref, o_ref):
    c = c_ref[...]
    s = (c * jax.nn.sigmoid(c)).astype(BF16)
    o_ref[...] = _dot(s, w_ref[...].astype(BF16)) + b_ref[...]


def _modulation(cvec, w_mod, b_mod):
    out = pl.pallas_call(
        _mod_kernel,
        out_shape=jax.ShapeDtypeStruct((DEPTH, N_MOD, MOD_ROWS, D_MODEL), F32),
        grid=(DEPTH, N_MOD),
        in_specs=[
            pl.BlockSpec((MOD_ROWS, D_MODEL), lambda l, n: (0, 0)),
            pl.BlockSpec((None, D_MODEL, D_MODEL), lambda l, n: (l, 0, n)),
            pl.BlockSpec((None, None, 1, D_MODEL), lambda l, n: (l, n, 0, 0)),
        ],
        out_specs=pl.BlockSpec((None, None, MOD_ROWS, D_MODEL), lambda l, n: (l, n, 0, 0)),
        compiler_params=_cparams("parallel", "parallel"),
        name="modulation",
    )(cvec, w_mod, b_mod.reshape(DEPTH, N_MOD, 1, D_MODEL))
    return out.reshape(DEPTH, N_MOD, MOD_ROWS, 1, D_MODEL)


def _mod_spec(layer, chunk, row0, tm, ngrid):
    def seg(i):
        return row0 + (i * tm) // SEG_LEN
    if ngrid == 1:
        return pl.BlockSpec((None, None, None, 1, D_MODEL), lambda i: (layer, chunk, seg(i), 0, 0))
    return pl.BlockSpec((None, None, None, 1, D_MODEL), lambda i, j: (layer, chunk, seg(i), 0, 0))


def _ffn_kernel(layer, n_ctx, xp_ref, xs_ref, sh_ref, sc_ref, gt_ref, wgu_hbm, wd_hbm, lng_ref, lnb_ref,
                op_ref, os_ref, wg_ref, wu_ref, wd_ref, stage_gu, stage_d, sem, h_ref, acc_ref):
    nf = FFN_DIM // FFN_TF
    step = pl.program_id(0)
    first_step = step == 0
    is_ctx = step < n_ctx

    def chunk_copies(c, slot):
        col = c * FFN_TF if isinstance(c, int) else pl.multiple_of(c * FFN_TF, FFN_TF)
        return (
            pltpu.make_async_copy(wgu_hbm.at[layer, :, pl.ds(col, FFN_TF)], stage_gu.at[slot, 0], sem.at[slot, 0]),
            pltpu.make_async_copy(wgu_hbm.at[layer, :, pl.ds(FFN_DIM + col, FFN_TF)], stage_gu.at[slot, 1],
                                  sem.at[slot, 1]),
            pltpu.make_async_copy(wd_hbm.at[layer, pl.ds(col, FFN_TF), :], stage_d.at[slot], sem.at[slot, 2]),
        )

    def modulated(x):
        return (x * (1.0 + sc_ref[...]) + sh_ref[...]).astype(BF16)

    def chunk(h, c):
        g = _dot(h, wg_ref[c])
        u = _dot(h, wu_ref[c])
        a = (g * jax.nn.sigmoid(g) * u).astype(BF16)
        return _dot(a, wd_ref[c])

    def finish(x, acc):
        y = ALPHA * x + (0.5 * gt_ref[...]) * acc
        out = _layer_norm(y, lng_ref[...], lnb_ref[...])

        @pl.when(is_ctx)
        def _():
            op_ref[...] = out

        @pl.when(jnp.logical_not(is_ctx))
        def _():
            os_ref[...] = out

    @pl.when(first_step)
    def _():
        for c in range(FFN_STAGES - 1):
            for cp in chunk_copies(c, c):
                cp.start()
        x = xp_ref[...]
        h_ref[...] = modulated(x)
        acc_ref[...] = jnp.zeros_like(acc_ref)

        def body(c, carry):
            ahead = c + (FFN_STAGES - 1)

            @pl.when(ahead < nf)
            def _():
                for cp in chunk_copies(ahead, lax.rem(ahead, FFN_STAGES)):
                    cp.start()

            slot = lax.rem(c, FFN_STAGES)
            for cp in chunk_copies(c, slot):
                cp.wait()
            wg_ref[c] = stage_gu[slot, 0].astype(BF16)
            wu_ref[c] = stage_gu[slot, 1].astype(BF16)
            wd_ref[c] = stage_d[slot].astype(BF16)
            acc_ref[...] += chunk(h_ref[...], c)
            return carry

        lax.fori_loop(0, nf, body, 0)
        finish(x, acc_ref[...])

    @pl.when(jnp.logical_not(first_step))
    def _():
        x = jnp.where(is_ctx, xp_ref[...], xs_ref[...])
        h = modulated(x)
        acc = chunk(h, 0)
        for c in range(1, nf):
            acc = acc + chunk(h, c)
        finish(x, acc)


def _ffn(xp, xs, mod, layer, sub, w_gu, w_down, ln_g, ln_b):
    n_ctx = xp.shape[0] // FFN_TM
    n_lat = xs.shape[0] // FFN_TM
    nf = FFN_DIM // FFN_TF
    assert n_ctx >= 1 and n_lat >= 1 and nf >= FFN_STAGES

    def ctx_map(i):
        return (jnp.minimum(i, n_ctx - 1), 0)

    def lat_map(i):
        return (jnp.maximum(i - n_ctx, 0), 0)

    def mod_spec(chunk):
        def index(i):
            row = jnp.where(i < n_ctx, 0, 1 + (jnp.maximum(i - n_ctx, 0) * FFN_TM) // SEG_LEN)
            return (layer, chunk, row, 0, 0)
        return pl.BlockSpec((None, None, None, 1, D_MODEL), index)

    return pl.pallas_call(
        functools.partial(_ffn_kernel, layer, n_ctx),
        out_shape=[jax.ShapeDtypeStruct(xp.shape, F32), jax.ShapeDtypeStruct(xs.shape, F32)],
        grid=(n_ctx + n_lat,),
        in_specs=[
            pl.BlockSpec((FFN_TM, D_MODEL), ctx_map),
            pl.BlockSpec((FFN_TM, D_MODEL), lat_map),
            mod_spec(3 * sub), mod_spec(3 * sub + 1), mod_spec(3 * sub + 2),
            pl.BlockSpec(memory_space=pl.ANY),
            pl.BlockSpec(memory_space=pl.ANY),
            pl.BlockSpec((None, None, 1, D_MODEL), lambda i: (layer, sub, 0, 0)),
            pl.BlockSpec((None, None, 1, D_MODEL), lambda i: (layer, sub, 0, 0)),
        ],
        out_specs=[pl.BlockSpec((FFN_TM, D_MODEL), ctx_map), pl.BlockSpec((FFN_TM, D_MODEL), lat_map)],
        scratch_shapes=[
            pltpu.VMEM((nf, D_MODEL, FFN_TF), BF16),
            pltpu.VMEM((nf, D_MODEL, FFN_TF), BF16),
            pltpu.VMEM((nf, FFN_TF, D_MODEL), BF16),
            pltpu.VMEM((FFN_STAGES, 2, D_MODEL, FFN_TF), F32),
            pltpu.VMEM((FFN_STAGES, FFN_TF, D_MODEL), F32),
            pltpu.SemaphoreType.DMA((FFN_STAGES, 3)),
            pltpu.VMEM((FFN_TM, D_MODEL), BF16),
            pltpu.VMEM((FFN_TM, D_MODEL), F32),
        ],
        compiler_params=_cparams("arbitrary"),
        name="ffn",
    )(xp, xs, mod, mod, mod, w_gu, w_down, ln_g, ln_b)


def _rope(x, cos, sin):
    quarter = HEAD_DIM // 4
    lane = lax.broadcasted_iota(jnp.int32, x.shape, 1)
    first = (lane % (2 * quarter)) < quarter
    rot = jnp.where(first, pltpu.roll(x, LANES - quarter, 1), pltpu.roll(x, quarter, 1))
    return x * cos + rot * sin


def _proj_kernel(rope, n_earlier, x_ref, sh_ref, sc_ref, w_ref, gq_ref, gk_ref, pm_ref, *rest):
    if rope:
        cos_ref, sin_ref, p_ref, uc_ref, wb_ref = rest
    else:
        earlier, rest = rest[:n_earlier], rest[n_earlier:]
        p_ref, uc_ref, nak_ref, nav_ref, nbk_ref, nbv_ref, wb_ref = rest

    @pl.when(pl.program_id(0) == 0)
    def _():
        wb_ref[...] = w_ref[...].astype(BF16)

    h = (x_ref[...] * (1.0 + sc_ref[...]) + sh_ref[...]).astype(BF16)
    proj = _dot(h, wb_ref[...])
    tm = proj.shape[0]

    qa = proj[:, 0:256]
    qa = qa * lax.rsqrt(_dot_split(qa * qa, pm_ref[...]) + RMS_EPS) * gq_ref[...]
    ka = proj[:, 256:384]
    ka = ka * lax.rsqrt(_dot_split(ka * ka, pm_ref[0:LANES, 0:LANES]) + RMS_EPS) * gk_ref[...]
    va = proj[:, 384:512]
    qb = proj[:, 512:1024]
    kb = proj[:, 1024:1536]
    vb = proj[:, 1536:2048]
    uc_ref[0] = proj[:, 2048:2048 + LANES]
    uc_ref[1] = proj[:, 2048 + LANES:2304]

    if not rope:
        for n, (ref, val) in enumerate(((nak_ref, ka), (nav_ref, va), (nbk_ref, kb), (nbv_ref, vb))):
            if n_earlier:
                layers = [e[...] for e in earlier[n::len(CACHE_WIDTHS)]] + [val]
                for l, v in enumerate(layers):
                    ref[:, l] = v.reshape(ref.shape[0], ref.shape[2], ref.shape[3])
            else:
                ref[...] = val

    def slabs(a):
        return [a[:, s:s + LANES] for s in range(0, a.shape[1], LANES)]

    if rope:
        cos, sin = cos_ref[...], sin_ref[...]
        qa_s = [_rope(s, cos, sin) for s in slabs(qa)]
        ka_s = [_rope(ka, cos, sin)]
        qb_s = [_rope(s, cos, sin) for s in slabs(qb)]
        kb_s = [_rope(s, cos, sin) for s in slabs(kb)]
    else:
        qa_s, ka_s, qb_s, kb_s = slabs(qa), [ka], slabs(qb), slabs(kb)

    scale = LOG2E * HEAD_DIM ** -0.5
    low = lax.broadcasted_iota(jnp.int32, (tm, LANES), 1) < HEAD_DIM
    zero = jnp.zeros((tm, LANES), F32)
    g0, g1 = qa_s[0] * scale, qa_s[1] * scale
    qa_pad = [jnp.where(low, g0, zero), jnp.where(low, pltpu.roll(g0, HEAD_DIM, 1), zero),
              jnp.where(low, zero, pltpu.roll(g1, HEAD_DIM, 1)), jnp.where(low, zero, g1)]
    pieces = qa_pad + ka_s + [va] + [s * scale for s in qb_s] + kb_s + slabs(vb)
    for n, piece in enumerate(pieces):
        p_ref[:, n * LANES:(n + 1) * LANES] = piece.astype(BF16)


def _project(x, mod, layer, row0, w_in, gq, gk, pmean, rope_tabs, seq=None, earlier=()):
    t = x.shape[0]
    rope = rope_tabs is not None
    nt = t // PROJ_TM
    in_specs = [
        pl.BlockSpec((PROJ_TM, D_MODEL), lambda i: (i, 0)),
        _mod_spec(layer, 3, row0, PROJ_TM, 1),
        _mod_spec(layer, 4, row0, PROJ_TM, 1),
        pl.BlockSpec((None, D_MODEL, IN_WIDTH), lambda i: (layer, 0, 0)),
        pl.BlockSpec((None, 1, 256), lambda i: (layer, 0, 0)),
        pl.BlockSpec((None, 1, LANES), lambda i: (layer, 0, 0)),
        pl.BlockSpec((256, 256), lambda i: (0, 0)),
    ]
    args = [x, mod, mod, w_in, gq, gk, pmean]
    out_shape = [jax.ShapeDtypeStruct((t, PROJ_W), BF16), jax.ShapeDtypeStruct((2, t, LANES), F32)]
    out_specs = [pl.BlockSpec((PROJ_TM, PROJ_W), lambda i: (i, 0)),
                 pl.BlockSpec((2, PROJ_TM, LANES), lambda i: (0, i, 0))]
    if rope:
        per_seg = SEG_LEN // PROJ_TM
        in_specs += [pl.BlockSpec((PROJ_TM, LANES), lambda i: (i % per_seg, 0))] * 2
        args += list(rope_tabs)
    else:
        for e, width in zip(earlier, CACHE_WIDTHS * (len(earlier) // len(CACHE_WIDTHS))):
            in_specs.append(pl.BlockSpec((PROJ_TM, width), lambda i: (i, 0)))
            args.append(e)
        for width in CACHE_WIDTHS:
            if earlier:
                out_shape.append(jax.ShapeDtypeStruct((t // seq, DEPTH, seq, width), F32))
                out_specs.append(pl.BlockSpec((PROJ_TM // seq, DEPTH, seq, width), lambda i: (i, 0, 0, 0)))
            else:
                out_shape.append(jax.ShapeDtypeStruct((t, width), F32))
                out_specs.append(pl.BlockSpec((PROJ_TM, width), lambda i: (i, 0)))
    return pl.pallas_call(
        functools.partial(_proj_kernel, rope, len(earlier)),
        out_shape=out_shape,
        grid=(nt,),
        in_specs=in_specs,
        out_specs=out_specs,
        scratch_shapes=[pltpu.VMEM((D_MODEL, IN_WIDTH), BF16)],
        compiler_params=_cparams("arbitrary"),
        name="project_latent" if rope else "project_context",
    )(*args)


def _ctx_attn_kernel(lam_init, p_ref, lp_ref, gs_ref, oa_ref, ob_ref, s_ref):
    ka = p_ref.at[:, KA_OFF:KA_OFF + LANES]
    va = p_ref.at[:, VA_OFF:VA_OFF + LANES]
    units = []
    for g in range(A_KV_HEADS):
        units.append((_gqa_query(p_ref[:, QA_OFF + 256 * g:QA_OFF + 256 * (g + 1)]), [(ka, va)]))
    for h in range(B_HEADS):
        k = p_ref.at[:, KB_OFF + h * LANES:KB_OFF + (h + 1) * LANES]
        v = p_ref.at[:, VB_OFF + h * LANES:VB_OFF + (h + 1) * LANES]
        units.append((_diff_query(p_ref[:, QB_OFF + h * LANES:QB_OFF + (h + 1) * LANES]), [(k, v)]))
    outs = _softmax_pv_units(units, s_ref)
    lam = _lambda(lp_ref[...]) + lam_init
    for g in range(A_KV_HEADS):
        oa_ref[:, g * LANES:(g + 1) * LANES] = _gqa_output(outs[g], g).astype(BF16)
    for h in range(B_HEADS):
        o = _diff_output(outs[A_KV_HEADS + h], lam, gs_ref[...], lam_init)
        ob_ref[:, h * LANES:(h + 1) * LANES] = o.astype(BF16)


def _context_attention(p, layer, lam_params, g_subln, lam_init, seq):
    t = p.shape[0]
    return pl.pallas_call(
        functools.partial(_ctx_attn_kernel, lam_init),
        out_shape=[jax.ShapeDtypeStruct((t, 256), BF16), jax.ShapeDtypeStruct((t, 512), BF16)],
        grid=(t // seq,),
        in_specs=[
            pl.BlockSpec((seq, PROJ_W), lambda i: (i, 0)),
            pl.BlockSpec((None, 4, B_QK_DIM), lambda i: (layer, 0, 0)),
            pl.BlockSpec((None, 1, B_V_DIM), lambda i: (layer, 0, 0)),
        ],
        out_specs=[pl.BlockSpec((seq, 256), lambda i: (i, 0)), pl.BlockSpec((seq, 512), lambda i: (i, 0))],
        scratch_shapes=[pltpu.VMEM((ATT_SLOTS, 2 * seq, seq), F32)],
        compiler_params=_cparams("parallel"),
        name="context_attention",
    )(p, lam_params, g_subln)


def _lat_gqa_kernel(q_ref, kn_ref, vn_ref, kc_ref, vc_ref, o_ref, s_ref):
    pieces = [(kc_ref, vc_ref), (kn_ref, vn_ref)]
    units = [(_gqa_query(q_ref[u * ATT_TQ:(u + 1) * ATT_TQ, :]), pieces) for u in range(ATT_UNITS)]
    outs = _softmax_pv_units(units, s_ref)
    for u in range(ATT_UNITS):
        o_ref[u * ATT_TQ:(u + 1) * ATT_TQ, :] = _gqa_output(outs[u], pl.program_id(1)).astype(BF16)


def _latent_gqa(p, cache_k, cache_v, layer, batch, length):
    tile = ATT_TQ * ATT_UNITS
    nq = length // tile
    past = cache_k.shape[2]
    return pl.pallas_call(
        _lat_gqa_kernel,
        out_shape=jax.ShapeDtypeStruct((batch * length, 256), BF16),
        grid=(batch, A_KV_HEADS, nq),
        in_specs=[
            pl.BlockSpec((tile, 256), lambda b, g, i: (b * nq + i, QA_OFF // 256 + g)),
            pl.BlockSpec((length, LANES), lambda b, g, i: (b, KA_OFF // LANES)),
            pl.BlockSpec((length, LANES), lambda b, g, i: (b, VA_OFF // LANES)),
            pl.BlockSpec((None, None, past, LANES), lambda b, g, i: (b, layer, 0, 0)),
            pl.BlockSpec((None, None, past, LANES), lambda b, g, i: (b, layer, 0, 0)),
        ],
        out_specs=pl.BlockSpec((tile, LANES), lambda b, g, i: (b * nq + i, g)),
        scratch_shapes=[pltpu.VMEM((ATT_SLOTS, 2 * ATT_TQ, past + length), F32)],
        compiler_params=_cparams("parallel", "parallel", "parallel"),
        name="latent_gqa",
    )(p, p, p, cache_k, cache_v)


def _lat_diff_kernel(lam_init, q_ref, kn_ref, vn_ref, kc_ref, vc_ref, lp_ref, gs_ref, o_ref, s_ref):
    pieces = [(kc_ref, vc_ref), (kn_ref, vn_ref)]
    units = [(_diff_query(q_ref[u * ATT_TQ:(u + 1) * ATT_TQ, :]), pieces) for u in range(ATT_UNITS)]
    outs = _softmax_pv_units(units, s_ref)
    lam = _lambda(lp_ref[...]) + lam_init
    for u in range(ATT_UNITS):
        o = _diff_output(outs[u], lam, gs_ref[...], lam_init)
        o_ref[u * ATT_TQ:(u + 1) * ATT_TQ, :] = o.astype(BF16)


def _latent_diff(p, cache_k, cache_v, layer, lam_params, g_subln, lam_init, batch, length):
    tile = ATT_TQ * ATT_UNITS
    nq = length // tile
    past = cache_k.shape[2]
    return pl.pallas_call(
        functools.partial(_lat_diff_kernel, lam_init),
        out_shape=jax.ShapeDtypeStruct((batch * length, 512), BF16),
        grid=(batch, B_HEADS, nq),
        in_specs=[
            pl.BlockSpec((tile, LANES), lambda b, h, i: (b * nq + i, QB_OFF // LANES + h)),
            pl.BlockSpec((length, LANES), lambda b, h, i: (b, KB_OFF // LANES + h)),
            pl.BlockSpec((length, LANES), lambda b, h, i: (b, VB_OFF // LANES + h)),
            pl.BlockSpec((None, None, past, LANES), lambda b, h, i: (b, layer, 0, h)),
            pl.BlockSpec((None, None, past, LANES), lambda b, h, i: (b, layer, 0, h)),
            pl.BlockSpec((None, 4, B_QK_DIM), lambda b, h, i: (layer, 0, 0)),
            pl.BlockSpec((None, 1, B_V_DIM), lambda b, h, i: (layer, 0, 0)),
        ],
        out_specs=pl.BlockSpec((tile, LANES), lambda b, h, i: (b * nq + i, h)),
        scratch_shapes=[pltpu.VMEM((ATT_SLOTS, 2 * ATT_TQ, past + length), F32)],
        compiler_params=_cparams("parallel", "parallel", "parallel"),
        name="latent_diff",
    )(p, p, p, cache_k, cache_v, lam_params, g_subln)


def _ctx_fourier_kernel(norm, u_ref, bcs_ref, cs_ref, o_ref):
    u = jnp.concatenate([u_ref[0], u_ref[1]], axis=1).astype(BF16)
    bcs = bcs_ref[...]
    rhs = jnp.concatenate([_dot(u, bcs[:C_WIDTH]), _dot(u, bcs[C_WIDTH:])], axis=0).astype(BF16)
    f = _dot(cs_ref[...], rhs) * norm
    o_ref[0] = f[:, :LANES]
    o_ref[1] = f[:, LANES:]


def _context_fourier(uc, bcs, cs, seq):
    t = uc.shape[1]
    norm = 1.0 / math.sqrt(seq * C_GROUP_DIM)
    return pl.pallas_call(
        functools.partial(_ctx_fourier_kernel, norm),
        out_shape=jax.ShapeDtypeStruct((2, t, LANES), F32),
        grid=(t // seq,),
        in_specs=[
            pl.BlockSpec((2, seq, LANES), lambda i: (0, i, 0)),
            pl.BlockSpec((2 * C_WIDTH, C_WIDTH), lambda i: (0, 0)),
            pl.BlockSpec((seq, 2 * seq), lambda i: (0, 0)),
        ],
        out_specs=pl.BlockSpec((2, seq, LANES), lambda i: (0, i, 0)),
        compiler_params=_cparams("parallel"),
        name="context_fourier",
    )(uc, bcs, cs)


def _lat_fourier_kernel(norm, u_ref, m1_ref, m2_ref, bcs_ref, tc_ref, ts_ref, o_ref, z_ref):
    n1 = GRID_W

    def strided(ref, plane, start):
        rows = pl.ds(start, n1, stride=n1)
        return jnp.concatenate([ref[(*plane, 0, rows, slice(None))], ref[(*plane, 1, rows, slice(None))]], axis=1)

    m1 = m1_ref[...]
    for b in range(n1):
        ub = strided(u_ref, (), b).astype(BF16)
        y = _dot(m1, ub)
        yr, yi = y[:n1], y[n1:]
        tc = jnp.concatenate([tc_ref[b * n1:(b + 1) * n1, :]] * 2, axis=1)
        ts = jnp.concatenate([ts_ref[b * n1:(b + 1) * n1, :]] * 2, axis=1)
        zr = yr * tc + yi * ts
        zi = yi * tc - yr * ts
        for half in range(2):
            z_ref[0, half, b * n1:(b + 1) * n1, :] = zr[:, half * LANES:(half + 1) * LANES]
            z_ref[1, half, b * n1:(b + 1) * n1, :] = zi[:, half * LANES:(half + 1) * LANES]
    m2 = m2_ref[...]
    bcs = bcs_ref[...]
    for q in range(n1):
        z = jnp.concatenate([strided(z_ref, (0,), q), strided(z_ref, (1,), q)], axis=0).astype(BF16)
        g = _dot(m2, z)
        gcat = jnp.concatenate([g[:n1], g[n1:]], axis=1).astype(BF16)
        f = _dot(gcat, bcs) * norm
        o_ref[0, pl.ds(q, n1, stride=n1), :] = f[:, :LANES]
        o_ref[1, pl.ds(q, n1, stride=n1), :] = f[:, LANES:]


def _latent_fourier(uc, m1, m2, bcs, tc, ts, length):
    t = uc.shape[1]
    norm = 1.0 / math.sqrt(length * C_GROUP_DIM)
    return pl.pallas_call(
        functools.partial(_lat_fourier_kernel, norm),
        out_shape=jax.ShapeDtypeStruct((2, t, LANES), F32),
        grid=(t // length,),
        in_specs=[
            pl.BlockSpec((2, length, LANES), lambda i: (0, i, 0)),
            pl.BlockSpec((2 * GRID_W, GRID_W), lambda i: (0, 0)),
            pl.BlockSpec((2 * GRID_W, 2 * GRID_W), lambda i: (0, 0)),
            pl.BlockSpec((2 * C_WIDTH, C_WIDTH), lambda i: (0, 0)),
            pl.BlockSpec((length, LANES), lambda i: (0, 0)),
            pl.BlockSpec((length, LANES), lambda i: (0, 0)),
        ],
        out_specs=pl.BlockSpec((2, length, LANES), lambda i: (0, i, 0)),
        scratch_shapes=[pltpu.VMEM((2, 2, length, LANES), F32)],
        compiler_params=_cparams("parallel"),
        name="latent_fourier",
    )(uc, m1, m2, bcs, tc, ts)


def _out_kernel(x_ref, gt_ref, oa_ref, ob_ref, f_ref, wf_ref, wo_ref, lng_ref, lnb_ref, o_ref, wob_ref):
    @pl.when(pl.program_id(0) == 0)
    def _():
        wob_ref[...] = wo_ref[...].astype(BF16)

    wf = wf_ref[...].astype(BF16)
    half = OUT_TM // 2
    for r in range(0, OUT_TM, half):
        rows = slice(r, r + half)
        f = jnp.concatenate([f_ref[0, rows, :], f_ref[1, rows, :]], axis=1).astype(BF16)
        oc = _dot(f, wf).astype(BF16)
        cat = jnp.concatenate([oa_ref[rows, :], ob_ref[rows, :], oc], axis=1)
        mo = _dot(cat, wob_ref[...])
        y = ALPHA * x_ref[rows, :] + gt_ref[...] * mo
        o_ref[rows, :] = _layer_norm(y, lng_ref[...], lnb_ref[...])


def _out_project(x, mod, layer, row0, oa, ob, f, w_fourier, w_out, ln_g, ln_b):
    t = x.shape[0]
    return pl.pallas_call(
        _out_kernel,
        out_shape=jax.ShapeDtypeStruct((t, D_MODEL), F32),
        grid=(t // OUT_TM,),
        in_specs=[
            pl.BlockSpec((OUT_TM, D_MODEL), lambda i: (i, 0)),
            _mod_spec(layer, 5, row0, OUT_TM, 1),
            pl.BlockSpec((OUT_TM, 256), lambda i: (i, 0)),
            pl.BlockSpec((OUT_TM, 512), lambda i: (i, 0)),
            pl.BlockSpec((2, OUT_TM, LANES), lambda i: (0, i, 0)),
            pl.BlockSpec((None, C_WIDTH, C_WIDTH), lambda i: (layer, 0, 0)),
            pl.BlockSpec((None, D_MODEL, D_MODEL), lambda i: (layer, 0, 0)),
            pl.BlockSpec((None, None, 1, D_MODEL), lambda i: (layer, 1, 0, 0)),
            pl.BlockSpec((None, None, 1, D_MODEL), lambda i: (layer, 1, 0, 0)),
        ],
        out_specs=pl.BlockSpec((OUT_TM, D_MODEL), lambda i: (i, 0)),
        scratch_shapes=[pltpu.VMEM((D_MODEL, D_MODEL), BF16)],
        compiler_params=_cparams("arbitrary"),
        name="out_project",
    )(x, mod, oa, ob, f, w_fourier, w_out, ln_g, ln_b)


def kernel(x_prompt, x_sample, cache_a_k, cache_a_v, cache_b_k, cache_b_v, c, c_ctx, w_mod, b_mod, w_in, g_qa, g_ka, lam_q1, lam_k1, lam_q2, lam_k2, g_subln, w_fourier, w_out, w_ffn1_gu, w_ffn1_down, w_ffn2_gu, w_ffn2_down, ln_g, ln_b):
    batch, seq, d = x_prompt.shape
    dec_batch, dec_seq, _ = x_sample.shape
    past = cache_a_k.shape[2]
    assert d == D_MODEL and dec_seq == SEG_LEN and batch * seq == SEG_LEN and dec_batch == 2

    cvec = jnp.concatenate([c_ctx[None, :], c, jnp.zeros((MOD_ROWS - 1 - dec_batch, d), F32)], axis=0)
    mod = _modulation(cvec, w_mod, b_mod)

    gq = jnp.tile(g_qa, (1, A_HEADS)).reshape(DEPTH, 1, A_HEADS * HEAD_DIM)
    gk = jnp.tile(g_ka, (1, A_KV_HEADS)).reshape(DEPTH, 1, A_KV_HEADS * HEAD_DIM)
    gs = g_subln.reshape(DEPTH, 1, B_V_DIM)
    lam_params = jnp.stack([lam_q1, lam_k1, lam_q2, lam_k2], axis=1)
    ln_g4 = ln_g.reshape(DEPTH, 3, 1, d)
    ln_b4 = ln_b.reshape(DEPTH, 3, 1, d)
    pmean = _group_mean_matrix(256, HEAD_DIM).astype(BF16)
    rope_tabs = _rope_tables(dec_seq)
    bcs = _channel_dft().astype(BF16)
    cs_ctx = _position_dft_small(seq).astype(BF16)
    m1, m2, tw_cos, tw_sin = _two_stage_dft(GRID_W)
    m1, m2 = m1.astype(BF16), m2.astype(BF16)
    ck_a = cache_a_k.reshape(dec_batch, DEPTH, past, A_KV_HEADS * HEAD_DIM)
    cv_a = cache_a_v.reshape(dec_batch, DEPTH, past, A_KV_HEADS * HEAD_DIM)
    ck_b = cache_b_k.reshape(dec_batch, DEPTH, past, B_HEADS * 2 * B_QK_DIM)
    cv_b = cache_b_v.reshape(dec_batch, DEPTH, past, B_HEADS * B_V_DIM)

    xp = x_prompt.reshape(batch * seq, d)
    xs = x_sample.reshape(dec_batch * dec_seq, d)
    new = []
    for layer in range(DEPTH):
        lam_init = 0.8 - 0.6 * math.exp(-0.3 * layer)
        xp, xs = _ffn(xp, xs, mod, layer, 0, w_ffn1_gu, w_ffn1_down, ln_g4, ln_b4)
        earlier = tuple(new) if layer == DEPTH - 1 else ()
        pp, ucp, *caches = _project(xp, mod, layer, 0, w_in, gq, gk, pmean, None, seq, earlier)
        new = caches if layer == DEPTH - 1 else new + caches
        oa, ob = _context_attention(pp, layer, lam_params, gs, lam_init, seq)
        fo = _context_fourier(ucp, bcs, cs_ctx, seq)
        xp = _out_project(xp, mod, layer, 0, oa, ob, fo, w_fourier, w_out, ln_g4, ln_b4)
        ps, ucs = _project(xs, mod, layer, 1, w_in, gq, gk, pmean, rope_tabs)
        oa = _latent_gqa(ps, ck_a, cv_a, layer, dec_batch, dec_seq)
        ob = _latent_diff(ps, ck_b, cv_b, layer, lam_params, gs, lam_init, dec_batch, dec_seq)
        fo = _latent_fourier(ucs, m1, m2, bcs, tw_cos, tw_sin, dec_seq)
        xs = _out_project(xs, mod, layer, 1, oa, ob, fo, w_fourier, w_out, ln_g4, ln_b4)
        xp, xs = _ffn(xp, xs, mod, layer, 2, w_ffn2_gu, w_ffn2_down, ln_g4, ln_b4)

    new_a_k, new_a_v, new_b_k, new_b_v = new
    return (xp.reshape(batch, seq, d), xs.reshape(dec_batch, dec_seq, d),
            new_a_k.reshape(batch, DEPTH, seq, A_KV_HEADS, HEAD_DIM),
            new_a_v.reshape(batch, DEPTH, seq, A_KV_HEADS, HEAD_DIM),
            new_b_k.reshape(batch, DEPTH, seq, B_HEADS, 2 * B_QK_DIM),
            new_b_v.reshape(batch, DEPTH, seq, B_HEADS, B_V_DIM))
```

```python
import functools
import math

import numpy as np
import jax
import jax.numpy as jnp
from jax import lax
from jax.experimental import pallas as pl
from jax.experimental.pallas import tpu as pltpu

F32 = jnp.float32
BF16 = jnp.bfloat16

D_MODEL = 1024
DEPTH = 2
GRID_W = 64
HEAD_DIM = 64
ROPE_THETA = 10000.0
A_HEADS = 4
A_KV_HEADS = 2
B_HEADS = 4
B_QK_DIM = 64
B_V_DIM = 128
C_GROUPS = 4
C_GROUP_DIM = 64
C_WIDTH = 256
FFN_DIM = 2816
N_MOD = 9
IN_WIDTH = 2304
ALPHA = (2.0 * DEPTH) ** 0.25
LN_EPS = 1e-5
RMS_EPS = 1e-6

LANES = 128
MOD_ROWS = 8
SEG_LEN = 4096
VMEM_LIMIT = 52 << 20

QA_OFF, KA_OFF, VA_OFF, QB_OFF, KB_OFF, VB_OFF = 0, 512, 640, 768, 1280, 1792
PROJ_W = 2304
CACHE_WIDTHS = (128, 128, 512, 512)

FFN_TM = 512
FFN_TF = 256
FFN_STAGES = 3
PROJ_TM = 512
OUT_TM = 1024
ATT_TQ = 256
ATT_UNITS = 4
ATT_SLOTS = 2
ATT_KC = 512
LOG2E = math.log2(math.e)


def _cparams(*sem):
    return pltpu.CompilerParams(dimension_semantics=sem, vmem_limit_bytes=VMEM_LIMIT)


def _rope_tables(length):
    rows = length // GRID_W
    row = np.repeat(np.arange(rows), GRID_W).astype(np.float64)
    col = np.tile(np.arange(GRID_W), rows).astype(np.float64)
    half = HEAD_DIM // 2
    inv = 1.0 / (ROPE_THETA ** (np.arange(0, half, 2, dtype=np.float64) / half))
    ar = row[:, None] * inv
    ac = col[:, None] * inv
    cos = np.concatenate([np.cos(ar), np.cos(ar), np.cos(ac), np.cos(ac)], -1)
    sin = np.concatenate([-np.sin(ar), np.sin(ar), -np.sin(ac), np.sin(ac)], -1)
    return (jnp.asarray(np.tile(cos, (1, 2)), F32), jnp.asarray(np.tile(sin, (1, 2)), F32))


def _group_mean_matrix(width, group):
    m = np.kron(np.eye(width // group), np.full((group, group), 1.0 / group))
    return jnp.asarray(m, F32)


def _channel_dft():
    c = np.arange(C_GROUP_DIM)
    ang = 2.0 * np.pi * np.outer(c, c) / C_GROUP_DIM
    eye = np.eye(C_GROUPS)
    return jnp.asarray(np.concatenate([np.kron(eye, np.cos(ang)), np.kron(eye, np.sin(ang))], 0), F32)


def _position_dft_small(length):
    n = np.arange(length)
    ang = 2.0 * np.pi * ((np.outer(n, n)) % length) / length
    return jnp.asarray(np.concatenate([np.cos(ang), -np.sin(ang)], 1), F32)


def _two_stage_dft(n1):
    i = np.arange(n1)
    ang = 2.0 * np.pi * np.outer(i, i) / n1
    c, s = np.cos(ang), np.sin(ang)
    m1 = np.concatenate([c, -s], 0)
    m2 = np.block([[c, s], [-s, c]])
    tw = 2.0 * np.pi * np.outer(i, i).reshape(-1) / (n1 * n1)
    tc = np.repeat(np.cos(tw)[:, None], LANES, 1)
    ts = np.repeat(np.sin(tw)[:, None], LANES, 1)
    return tuple(jnp.asarray(a, F32) for a in (m1, m2, tc, ts))


def _layer_norm(y, g, b):
    mu = jnp.mean(y, axis=-1, keepdims=True)
    yc = y - mu
    var = jnp.mean(yc * yc, axis=-1, keepdims=True)
    return yc * lax.rsqrt(var + LN_EPS) * g + b


def _dot(a, b):
    return jnp.dot(a, b, preferred_element_type=F32)


def _dot_nt(a, b):
    return lax.dot_general(a, b, (((1,), (1,)), ((), ())), preferred_element_type=F32)


def _dot_split(a, b):
    hi = a.astype(BF16)
    lo = (a - hi.astype(F32)).astype(BF16)
    return _dot(hi, b) + _dot(lo, b)


def _softmax_pv_units(units, s_ref):
    rows = units[0][0].shape[0]
    chunk_lists = []
    for _, pieces in units:
        chunks, off = [], 0
        for k_ref, v_ref in pieces:
            for st in range(0, k_ref.shape[0], ATT_KC):
                sz = min(ATT_KC, k_ref.shape[0] - st)
                chunks.append((k_ref, v_ref, st, sz, off))
                off += sz
        chunk_lists.append(chunks)
    nchunk = len(chunk_lists[0])
    run_max = [None] * len(units)
    row_max = [None] * len(units)
    acc = [None] * len(units)

    def scores(u, ci):
        k_ref, _, st, sz, off = chunk_lists[u][ci]
        s = _dot_nt(units[u][0], k_ref[st:st + sz, :].astype(BF16))
        s_ref[u % ATT_SLOTS, :, off:off + sz] = s
        part = functools.reduce(jnp.maximum, [s[:, j:j + LANES] for j in range(0, sz, LANES)])
        run_max[u] = part if run_max[u] is None else jnp.maximum(run_max[u], part)

    def finish_max(u):
        m = jnp.max(run_max[u], axis=1, keepdims=True)
        row_max[u] = jnp.broadcast_to(m, (rows, LANES))

    def weighted(u, ci):
        _, v_ref, st, sz, off = chunk_lists[u][ci]
        e = jnp.concatenate([jnp.exp2(s_ref[u % ATT_SLOTS, :, off + j:off + j + LANES] - row_max[u])
                             for j in range(0, sz, LANES)], axis=1).astype(BF16)
        v1 = jnp.concatenate([v_ref[st:st + sz, :].astype(BF16), jnp.ones((sz, LANES), BF16)], axis=1)
        d = _dot(e, v1)
        acc[u] = d if acc[u] is None else acc[u] + d

    for ci in range(nchunk):
        scores(0, ci)
    finish_max(0)
    for u in range(1, len(units)):
        for ci in range(nchunk):
            weighted(u - 1, ci)
            scores(u, ci)
        finish_max(u)
    for ci in range(nchunk):
        weighted(len(units) - 1, ci)
    return [a[:, :LANES] / a[:, LANES:] for a in acc]


def _gqa_query(qpad):
    return jnp.concatenate([qpad[:, :LANES], qpad[:, LANES:]], axis=0)


def _gqa_output(o, group):
    tq = o.shape[0] // 2
    o0, o1 = o[:tq], o[tq:]
    low = lax.broadcasted_iota(jnp.int32, (tq, LANES), 1) < HEAD_DIM
    first = group == 0
    r0, r1 = pltpu.roll(o0, HEAD_DIM, 1), pltpu.roll(o1, HEAD_DIM, 1)
    return jnp.where(low, jnp.where(first, o0, r0), jnp.where(first, r1, o1))


def _diff_query(q):
    low = lax.broadcasted_iota(jnp.int32, q.shape, 1) < B_QK_DIM
    zero = jnp.zeros_like(q)
    return jnp.concatenate([jnp.where(low, q, zero), jnp.where(low, zero, q)], axis=0)


def _diff_output(o, lam, g_sub, lam_init):
    tq = o.shape[0] // 2
    w = o[:tq] - lam * o[tq:]
    y = w * lax.rsqrt(jnp.mean(w * w, axis=-1, keepdims=True) + RMS_EPS)
    return y * g_sub * (1.0 - lam_init)


def _lambda(lp):
    s1 = jnp.sum(lp[0:1] * lp[1:2], axis=1, keepdims=True)
    s2 = jnp.sum(lp[2:3] * lp[3:4], axis=1, keepdims=True)
    return jnp.exp(s1) - jnp.exp(s2)


def _mod_kernel(c_ref, w_ref, b_ref, o_ref):
    c = c_ref[...]
    s = (c * jax.nn.sigmoid(c)).astype(BF16)
    o_ref[...] = _dot(s, w_ref[...].astype(BF16)) + b_ref[...]


def _modulation(cvec, w_mod, b_mod):
    out = pl.pallas_call(
        _mod_kernel,
        out_shape=jax.ShapeDtypeStruct((DEPTH, N_MOD, MOD_ROWS, D_MODEL), F32),
        grid=(DEPTH, N_MOD),
        in_specs=[
            pl.BlockSpec((MOD_ROWS, D_MODEL), lambda l, n: (0, 0)),
            pl.BlockSpec((None, D_MODEL, D_MODEL), lambda l, n: (l, 0, n)),
            pl.BlockSpec((None, None, 1, D_MODEL), lambda l, n: (l, n, 0, 0)),
        ],
        out_specs=pl.BlockSpec((None, None, MOD_ROWS, D_MODEL), lambda l, n: (l, n, 0, 0)),
        compiler_params=_cparams("parallel", "parallel"),
        name="modulation",
    )(cvec, w_mod, b_mod.reshape(DEPTH, N_MOD, 1, D_MODEL))
    return out.reshape(DEPTH, N_MOD, MOD_ROWS, 1, D_MODEL)


def _mod_spec(layer, chunk, row0, tm):
    return pl.BlockSpec((None, None, None, 1, D_MODEL),
                        lambda i: (layer, chunk, row0 + (i * tm) // SEG_LEN, 0, 0))


def _ffn_kernel(layer, n_ctx, xp_ref, xs_ref, sh_ref, sc_ref, gt_ref, wgu_hbm, wd_hbm, lng_ref, lnb_ref,
                op_ref, os_ref, wg_ref, wu_ref, wd_ref, stage_gu, stage_d, sem, h_ref, acc_ref):
    nf = FFN_DIM // FFN_TF
    step = pl.program_id(0)
    first_step = step == 0
    is_ctx = step < n_ctx

    def chunk_copies(c, slot):
        col = c * FFN_TF if isinstance(c, int) else pl.multiple_of(c * FFN_TF, FFN_TF)
        return (
            pltpu.make_async_copy(wgu_hbm.at[layer, :, pl.ds(col, FFN_TF)], stage_gu.at[slot, 0], sem.at[slot, 0]),
            pltpu.make_async_copy(wgu_hbm.at[layer, :, pl.ds(FFN_DIM + col, FFN_TF)], stage_gu.at[slot, 1],
                                  sem.at[slot, 1]),
            pltpu.make_async_copy(wd_hbm.at[layer, pl.ds(col, FFN_TF), :], stage_d.at[slot], sem.at[slot, 2]),
        )

    def start_chunk(c, slot):
        for n, cp in enumerate(chunk_copies(c, slot)):
            cp.start(priority=n % 2)

    def modulated(x):
        return (x * (1.0 + sc_ref[...]) + sh_ref[...]).astype(BF16)

    def chunk(h, c):
        g = _dot(h, wg_ref[c])
        u = _dot(h, wu_ref[c])
        a = (g * jax.nn.sigmoid(g) * u).astype(BF16)
        return _dot(a, wd_ref[c])

    def finish(x, acc):
        y = ALPHA * x + (0.5 * gt_ref[...]) * acc
        out = _layer_norm(y, lng_ref[...], lnb_ref[...])

        @pl.when(is_ctx)
        def _():
            op_ref[...] = out

        @pl.when(jnp.logical_not(is_ctx))
        def _():
            os_ref[...] = out

    @pl.when(first_step)
    def _():
        for c in range(FFN_STAGES - 1):
            start_chunk(c, c)
        x = xp_ref[...]
        h_ref[...] = modulated(x)
        acc_ref[...] = jnp.zeros_like(acc_ref)

        def body(c, carry):
            ahead = c + (FFN_STAGES - 1)

            @pl.when(ahead < nf)
            def _():
                start_chunk(ahead, lax.rem(ahead, FFN_STAGES))

            slot = lax.rem(c, FFN_STAGES)
            for cp in chunk_copies(c, slot):
                cp.wait()
            wg_ref[c] = stage_gu[slot, 0].astype(BF16)
            wu_ref[c] = stage_gu[slot, 1].astype(BF16)
            wd_ref[c] = stage_d[slot].astype(BF16)
            acc_ref[...] += chunk(h_ref[...], c)
            return carry

        lax.fori_loop(0, nf, body, 0)
        finish(x, acc_ref[...])

    @pl.when(jnp.logical_not(first_step))
    def _():
        x = jnp.where(is_ctx, xp_ref[...], xs_ref[...])
        h = modulated(x)
        acc = chunk(h, 0)
        for c in range(1, nf):
            acc = acc + chunk(h, c)
        finish(x, acc)


def _ffn(xp, xs, mod, layer, sub, w_gu, w_down, ln_g, ln_b):
    n_ctx = xp.shape[0] // FFN_TM
    n_lat = xs.shape[0] // FFN_TM
    nf = FFN_DIM // FFN_TF
    assert n_ctx >= 1 and n_lat >= 1 and nf >= FFN_STAGES

    def ctx_map(i):
        return (jnp.minimum(i, n_ctx - 1), 0)

    def lat_map(i):
        return (jnp.maximum(i - n_ctx, 0), 0)

    def mod_spec(chunk):
        def index(i):
            row = jnp.where(i < n_ctx, 0, 1 + (jnp.maximum(i - n_ctx, 0) * FFN_TM) // SEG_LEN)
            return (layer, chunk, row, 0, 0)
        return pl.BlockSpec((None, None, None, 1, D_MODEL), index)

    return pl.pallas_call(
        functools.partial(_ffn_kernel, layer, n_ctx),
        out_shape=[jax.ShapeDtypeStruct(xp.shape, F32), jax.ShapeDtypeStruct(xs.shape, F32)],
        grid=(n_ctx + n_lat,),
        in_specs=[
            pl.BlockSpec((FFN_TM, D_MODEL), ctx_map),
            pl.BlockSpec((FFN_TM, D_MODEL), lat_map),
            mod_spec(3 * sub), mod_spec(3 * sub + 1), mod_spec(3 * sub + 2),
            pl.BlockSpec(memory_space=pl.ANY),
            pl.BlockSpec(memory_space=pl.ANY),
            pl.BlockSpec((None, None, 1, D_MODEL), lambda i: (layer, sub, 0, 0)),
            pl.BlockSpec((None, None, 1, D_MODEL), lambda i: (layer, sub, 0, 0)),
        ],
        out_specs=[pl.BlockSpec((FFN_TM, D_MODEL), ctx_map), pl.BlockSpec((FFN_TM, D_MODEL), lat_map)],
        scratch_shapes=[
            pltpu.VMEM((nf, D_MODEL, FFN_TF), BF16),
            pltpu.VMEM((nf, D_MODEL, FFN_TF), BF16),
            pltpu.VMEM((nf, FFN_TF, D_MODEL), BF16),
            pltpu.VMEM((FFN_STAGES, 2, D_MODEL, FFN_TF), F32),
            pltpu.VMEM((FFN_STAGES, FFN_TF, D_MODEL), F32),
            pltpu.SemaphoreType.DMA((FFN_STAGES, 3)),
            pltpu.VMEM((FFN_TM, D_MODEL), BF16),
            pltpu.VMEM((FFN_TM, D_MODEL), F32),
        ],
        compiler_params=_cparams("arbitrary"),
        name="ffn",
    )(xp, xs, mod, mod, mod, w_gu, w_down, ln_g, ln_b)


def _rope(x, cos, sin):
    quarter = HEAD_DIM // 4
    lane = lax.broadcasted_iota(jnp.int32, x.shape, 1)
    first = (lane % (2 * quarter)) < quarter
    rot = jnp.where(first, pltpu.roll(x, LANES - quarter, 1), pltpu.roll(x, quarter, 1))
    return x * cos + rot * sin


def _proj_kernel(rope, n_earlier, x_ref, sh_ref, sc_ref, w_ref, gq_ref, gk_ref, pm_ref, *rest):
    if rope:
        cos_ref, sin_ref, p_ref, uc_ref, wb_ref = rest
    else:
        earlier, rest = rest[:n_earlier], rest[n_earlier:]
        p_ref, uc_ref, nak_ref, nav_ref, nbk_ref, nbv_ref, wb_ref = rest

    @pl.when(pl.program_id(0) == 0)
    def _():
        wb_ref[...] = w_ref[...].astype(BF16)

    h = (x_ref[...] * (1.0 + sc_ref[...]) + sh_ref[...]).astype(BF16)
    proj = _dot(h, wb_ref[...])
    tm = proj.shape[0]

    qa = proj[:, 0:256]
    qa = qa * lax.rsqrt(_dot_split(qa * qa, pm_ref[...]) + RMS_EPS) * gq_ref[...]
    ka = proj[:, 256:384]
    ka = ka * lax.rsqrt(_dot_split(ka * ka, pm_ref[0:LANES, 0:LANES]) + RMS_EPS) * gk_ref[...]
    va = proj[:, 384:512]
    qb = proj[:, 512:1024]
    kb = proj[:, 1024:1536]
    vb = proj[:, 1536:2048]
    uc_ref[0] = proj[:, 2048:2048 + LANES]
    uc_ref[1] = proj[:, 2048 + LANES:2304]

    if not rope:
        for n, (ref, val) in enumerate(((nak_ref, ka), (nav_ref, va), (nbk_ref, kb), (nbv_ref, vb))):
            if n_earlier:
                layers = [e[...] for e in earlier[n::len(CACHE_WIDTHS)]] + [val]
                for l, v in enumerate(layers):
                    ref[:, l] = v.reshape(ref.shape[0], ref.shape[2], ref.shape[3])
            else:
                ref[...] = val

    def slabs(a):
        return [a[:, s:s + LANES] for s in range(0, a.shape[1], LANES)]

    if rope:
        cos, sin = cos_ref[...], sin_ref[...]
        qa_s = [_rope(s, cos, sin) for s in slabs(qa)]
        ka_s = [_rope(ka, cos, sin)]
        qb_s = [_rope(s, cos, sin) for s in slabs(qb)]
        kb_s = [_rope(s, cos, sin) for s in slabs(kb)]
    else:
        qa_s, ka_s, qb_s, kb_s = slabs(qa), [ka], slabs(qb), slabs(kb)

    scale = LOG2E * HEAD_DIM ** -0.5
    low = lax.broadcasted_iota(jnp.int32, (tm, LANES), 1) < HEAD_DIM
    zero = jnp.zeros((tm, LANES), F32)
    g0, g1 = qa_s[0] * scale, qa_s[1] * scale
    qa_pad = [jnp.where(low, g0, zero), jnp.where(low, pltpu.roll(g0, HEAD_DIM, 1), zero),
              jnp.where(low, zero, pltpu.roll(g1, HEAD_DIM, 1)), jnp.where(low, zero, g1)]
    pieces = qa_pad + ka_s + [va] + [s * scale for s in qb_s] + kb_s + slabs(vb)
    for n, piece in enumerate(pieces):
        p_ref[:, n * LANES:(n + 1) * LANES] = piece.astype(BF16)


def _project(x, mod, layer, row0, w_in, gq, gk, pmean, rope_tabs, seq=None, earlier=()):
    t = x.shape[0]
    rope = rope_tabs is not None
    nt = t // PROJ_TM
    in_specs = [
        pl.BlockSpec((PROJ_TM, D_MODEL), lambda i: (i, 0)),
        _mod_spec(layer, 3, row0, PROJ_TM),
        _mod_spec(layer, 4, row0, PROJ_TM),
        pl.BlockSpec((None, D_MODEL, IN_WIDTH), lambda i: (layer, 0, 0)),
        pl.BlockSpec((None, 1, 256), lambda i: (layer, 0, 0)),
        pl.BlockSpec((None, 1, LANES), lambda i: (layer, 0, 0)),
        pl.BlockSpec((256, 256), lambda i: (0, 0)),
    ]
    args = [x, mod, mod, w_in, gq, gk, pmean]
    out_shape = [jax.ShapeDtypeStruct((t, PROJ_W), BF16), jax.ShapeDtypeStruct((2, t, LANES), F32)]
    out_specs = [pl.BlockSpec((PROJ_TM, PROJ_W), lambda i: (i, 0)),
                 pl.BlockSpec((2, PROJ_TM, LANES), lambda i: (0, i, 0))]
    if rope:
        per_seg = SEG_LEN // PROJ_TM
        in_specs += [pl.BlockSpec((PROJ_TM, LANES), lambda i: (i % per_seg, 0))] * 2
        args += list(rope_tabs)
    else:
        for e, width in zip(earlier, CACHE_WIDTHS * (len(earlier) // len(CACHE_WIDTHS))):
            in_specs.append(pl.BlockSpec((PROJ_TM, width), lambda i: (i, 0)))
            args.append(e)
        for width in CACHE_WIDTHS:
            if earlier:
                out_shape.append(jax.ShapeDtypeStruct((t // seq, DEPTH, seq, width), F32))
                out_specs.append(pl.BlockSpec((PROJ_TM // seq, DEPTH, seq, width), lambda i: (i, 0, 0, 0)))
            else:
                out_shape.append(jax.ShapeDtypeStruct((t, width), F32))
                out_specs.append(pl.BlockSpec((PROJ_TM, width), lambda i: (i, 0)))
    return pl.pallas_call(
        functools.partial(_proj_kernel, rope, len(earlier)),
        out_shape=out_shape,
        grid=(nt,),
        in_specs=in_specs,
        out_specs=out_specs,
        scratch_shapes=[pltpu.VMEM((D_MODEL, IN_WIDTH), BF16)],
        compiler_params=_cparams("arbitrary"),
        name="project_latent" if rope else "project_context",
    )(*args)


def _ctx_attn_kernel(lam_init, p_ref, lp_ref, gs_ref, oa_ref, ob_ref, s_ref):
    ka = p_ref.at[:, KA_OFF:KA_OFF + LANES]
    va = p_ref.at[:, VA_OFF:VA_OFF + LANES]
    units = []
    for g in range(A_KV_HEADS):
        units.append((_gqa_query(p_ref[:, QA_OFF + 256 * g:QA_OFF + 256 * (g + 1)]), [(ka, va)]))
    for h in range(B_HEADS):
        k = p_ref.at[:, KB_OFF + h * LANES:KB_OFF + (h + 1) * LANES]
        v = p_ref.at[:, VB_OFF + h * LANES:VB_OFF + (h + 1) * LANES]
        units.append((_diff_query(p_ref[:, QB_OFF + h * LANES:QB_OFF + (h + 1) * LANES]), [(k, v)]))
    outs = _softmax_pv_units(units, s_ref)
    lam = _lambda(lp_ref[...]) + lam_init
    for g in range(A_KV_HEADS):
        oa_ref[:, g * LANES:(g + 1) * LANES] = _gqa_output(outs[g], g).astype(BF16)
    for h in range(B_HEADS):
        o = _diff_output(outs[A_KV_HEADS + h], lam, gs_ref[...], lam_init)
        ob_ref[:, h * LANES:(h + 1) * LANES] = o.astype(BF16)


def _context_attention(p, layer, lam_params, g_subln, lam_init, seq):
    t = p.shape[0]
    return pl.pallas_call(
        functools.partial(_ctx_attn_kernel, lam_init),
        out_shape=[jax.ShapeDtypeStruct((t, 256), BF16), jax.ShapeDtypeStruct((t, 512), BF16)],
        grid=(t // seq,),
        in_specs=[
            pl.BlockSpec((seq, PROJ_W), lambda i: (i, 0)),
            pl.BlockSpec((None, 4, B_QK_DIM), lambda i: (layer, 0, 0)),
            pl.BlockSpec((None, 1, B_V_DIM), lambda i: (layer, 0, 0)),
        ],
        out_specs=[pl.BlockSpec((seq, 256), lambda i: (i, 0)), pl.BlockSpec((seq, 512), lambda i: (i, 0))],
        scratch_shapes=[pltpu.VMEM((ATT_SLOTS, 2 * seq, seq), F32)],
        compiler_params=_cparams("parallel"),
        name="context_attention",
    )(p, lam_params, g_subln)


def _lat_gqa_kernel(q_ref, kn_ref, vn_ref, kc_ref, vc_ref, o_ref, s_ref):
    pieces = [(kc_ref, vc_ref), (kn_ref, vn_ref)]
    units = [(_gqa_query(q_ref[u * ATT_TQ:(u + 1) * ATT_TQ, :]), pieces) for u in range(ATT_UNITS)]
    outs = _softmax_pv_units(units, s_ref)
    for u in range(ATT_UNITS):
        o_ref[u * ATT_TQ:(u + 1) * ATT_TQ, :] = _gqa_output(outs[u], pl.program_id(1)).astype(BF16)


def _latent_gqa(p, cache_k, cache_v, layer, batch, length):
    tile = ATT_TQ * ATT_UNITS
    nq = length // tile
    past = cache_k.shape[2]
    return pl.pallas_call(
        _lat_gqa_kernel,
        out_shape=jax.ShapeDtypeStruct((batch * length, 256), BF16),
        grid=(batch, A_KV_HEADS, nq),
        in_specs=[
            pl.BlockSpec((tile, 256), lambda b, g, i: (b * nq + i, QA_OFF // 256 + g)),
            pl.BlockSpec((length, LANES), lambda b, g, i: (b, KA_OFF // LANES)),
            pl.BlockSpec((length, LANES), lambda b, g, i: (b, VA_OFF // LANES)),
            pl.BlockSpec((None, None, past, LANES), lambda b, g, i: (b, layer, 0, 0)),
            pl.BlockSpec((None, None, past, LANES), lambda b, g, i: (b, layer, 0, 0)),
        ],
        out_specs=pl.BlockSpec((tile, LANES), lambda b, g, i: (b * nq + i, g)),
        scratch_shapes=[pltpu.VMEM((ATT_SLOTS, 2 * ATT_TQ, past + length), F32)],
        compiler_params=_cparams("parallel", "parallel", "parallel"),
        name="latent_gqa",
    )(p, p, p, cache_k, cache_v)


def _lat_diff_kernel(lam_init, q_ref, kn_ref, vn_ref, kc_ref, vc_ref, lp_ref, gs_ref, o_ref, s_ref):
    pieces = [(kc_ref, vc_ref), (kn_ref, vn_ref)]
    units = [(_diff_query(q_ref[u * ATT_TQ:(u + 1) * ATT_TQ, :]), pieces) for u in range(ATT_UNITS)]
    outs = _softmax_pv_units(units, s_ref)
    lam = _lambda(lp_ref[...]) + lam_init
    for u in range(ATT_UNITS):
        o = _diff_output(outs[u], lam, gs_ref[...], lam_init)
        o_ref[u * ATT_TQ:(u + 1) * ATT_TQ, :] = o.astype(BF16)


def _latent_diff(p, cache_k, cache_v, layer, lam_params, g_subln, lam_init, batch, length):
    tile = ATT_TQ * ATT_UNITS
    nq = length // tile
    past = cache_k.shape[2]
    return pl.pallas_call(
        functools.partial(_lat_diff_kernel, lam_init),
        out_shape=jax.ShapeDtypeStruct((batch * length, 512), BF16),
        grid=(batch, B_HEADS, nq),
        in_specs=[
            pl.BlockSpec((tile, LANES), lambda b, h, i: (b * nq + i, QB_OFF // LANES + h)),
            pl.BlockSpec((length, LANES), lambda b, h, i: (b, KB_OFF // LANES + h)),
            pl.BlockSpec((length, LANES), lambda b, h, i: (b, VB_OFF // LANES + h)),
            pl.BlockSpec((None, None, past, LANES), lambda b, h, i: (b, layer, 0, h)),
            pl.BlockSpec((None, None, past, LANES), lambda b, h, i: (b, layer, 0, h)),
            pl.BlockSpec((None, 4, B_QK_DIM), lambda b, h, i: (layer, 0, 0)),
            pl.BlockSpec((None, 1, B_V_DIM), lambda b, h, i: (layer, 0, 0)),
        ],
        out_specs=pl.BlockSpec((tile, LANES), lambda b, h, i: (b * nq + i, h)),
        scratch_shapes=[pltpu.VMEM((ATT_SLOTS, 2 * ATT_TQ, past + length), F32)],
        compiler_params=_cparams("parallel", "parallel", "parallel"),
        name="latent_diff",
    )(p, p, p, cache_k, cache_v, lam_params, g_subln)


def _ctx_fourier_kernel(norm, u_ref, bcs_ref, cs_ref, o_ref):
    u = jnp.concatenate([u_ref[0], u_ref[1]], axis=1).astype(BF16)
    bcs = bcs_ref[...]
    rhs = jnp.concatenate([_dot(u, bcs[:C_WIDTH]), _dot(u, bcs[C_WIDTH:])], axis=0).astype(BF16)
    f = _dot(cs_ref[...], rhs) * norm
    o_ref[0] = f[:, :LANES]
    o_ref[1] = f[:, LANES:]


def _context_fourier(uc, bcs, cs, seq):
    t = uc.shape[1]
    norm = 1.0 / math.sqrt(seq * C_GROUP_DIM)
    return pl.pallas_call(
        functools.partial(_ctx_fourier_kernel, norm),
        out_shape=jax.ShapeDtypeStruct((2, t, LANES), F32),
        grid=(t // seq,),
        in_specs=[
            pl.BlockSpec((2, seq, LANES), lambda i: (0, i, 0)),
            pl.BlockSpec((2 * C_WIDTH, C_WIDTH), lambda i: (0, 0)),
            pl.BlockSpec((seq, 2 * seq), lambda i: (0, 0)),
        ],
        out_specs=pl.BlockSpec((2, seq, LANES), lambda i: (0, i, 0)),
        compiler_params=_cparams("parallel"),
        name="context_fourier",
    )(uc, bcs, cs)


def _lat_fourier_kernel(norm, u_ref, m1_ref, m2_ref, bcs_ref, tc_ref, ts_ref, o_ref, z_ref):
    n1 = GRID_W

    def strided(ref, plane, start):
        rows = pl.ds(start, n1, stride=n1)
        return jnp.concatenate([ref[(*plane, 0, rows, slice(None))], ref[(*plane, 1, rows, slice(None))]], axis=1)

    m1 = m1_ref[...]
    for b in range(n1):
        ub = strided(u_ref, (), b).astype(BF16)
        y = _dot(m1, ub)
        yr, yi = y[:n1], y[n1:]
        tc = jnp.concatenate([tc_ref[b * n1:(b + 1) * n1, :]] * 2, axis=1)
        ts = jnp.concatenate([ts_ref[b * n1:(b + 1) * n1, :]] * 2, axis=1)
        zr = yr * tc + yi * ts
        zi = yi * tc - yr * ts
        rows = pl.ds(b, n1, stride=n1)
        for half in range(2):
            z_ref[0, half, rows, :] = zr[:, half * LANES:(half + 1) * LANES]
            z_ref[1, half, rows, :] = zi[:, half * LANES:(half + 1) * LANES]
    m2 = m2_ref[...]
    bcs = bcs_ref[...]
    for q in range(n1):
        rows = slice(q * n1, (q + 1) * n1)
        z = jnp.concatenate([z_ref[0, 0, rows, :], z_ref[0, 1, rows, :]], axis=1)
        z = jnp.concatenate([z, jnp.concatenate([z_ref[1, 0, rows, :], z_ref[1, 1, rows, :]], axis=1)],
                            axis=0).astype(BF16)
        g = _dot(m2, z)
        gcat = jnp.concatenate([g[:n1], g[n1:]], axis=1).astype(BF16)
        f = _dot(gcat, bcs) * norm
        o_ref[0, pl.ds(q, n1, stride=n1), :] = f[:, :LANES]
        o_ref[1, pl.ds(q, n1, stride=n1), :] = f[:, LANES:]


def _latent_fourier(uc, m1, m2, bcs, tc, ts, length):
    t = uc.shape[1]
    norm = 1.0 / math.sqrt(length * C_GROUP_DIM)
    return pl.pallas_call(
        functools.partial(_lat_fourier_kernel, norm),
        out_shape=jax.ShapeDtypeStruct((2, t, LANES), F32),
        grid=(t // length,),
        in_specs=[
            pl.BlockSpec((2, length, LANES), lambda i: (0, i, 0)),
            pl.BlockSpec((2 * GRID_W, GRID_W), lambda i: (0, 0)),
            pl.BlockSpec((2 * GRID_W, 2 * GRID_W), lambda i: (0, 0)),
            pl.BlockSpec((2 * C_WIDTH, C_WIDTH), lambda i: (0, 0)),
            pl.BlockSpec((length, LANES), lambda i: (0, 0)),
            pl.BlockSpec((length, LANES), lambda i: (0, 0)),
        ],
        out_specs=pl.BlockSpec((2, length, LANES), lambda i: (0, i, 0)),
        scratch_shapes=[pltpu.VMEM((2, 2, length, LANES), F32)],
        compiler_params=_cparams("parallel"),
        name="latent_fourier",
    )(uc, m1, m2, bcs, tc, ts)


def _out_kernel(x_ref, gt_ref, oa_ref, ob_ref, f_ref, wf_ref, wo_ref, lng_ref, lnb_ref, o_ref, wob_ref):
    @pl.when(pl.program_id(0) == 0)
    def _():
        wob_ref[...] = wo_ref[...].astype(BF16)

    wf = wf_ref[...].astype(BF16)
    half = OUT_TM // 2
    for r in range(0, OUT_TM, half):
        rows = slice(r, r + half)
        f = jnp.concatenate([f_ref[0, rows, :], f_ref[1, rows, :]], axis=1).astype(BF16)
        oc = _dot(f, wf).astype(BF16)
        cat = jnp.concatenate([oa_ref[rows, :], ob_ref[rows, :], oc], axis=1)
        mo = _dot(cat, wob_ref[...])
        y = ALPHA * x_ref[rows, :] + gt_ref[...] * mo
        o_ref[rows, :] = _layer_norm(y, lng_ref[...], lnb_ref[...])


def _out_project(x, mod, layer, row0, oa, ob, f, w_fourier, w_out, ln_g, ln_b):
    t = x.shape[0]
    return pl.pallas_call(
        _out_kernel,
        out_shape=jax.ShapeDtypeStruct((t, D_MODEL), F32),
        grid=(t // OUT_TM,),
        in_specs=[
            pl.BlockSpec((OUT_TM, D_MODEL), lambda i: (i, 0)),
            _mod_spec(layer, 5, row0, OUT_TM),
            pl.BlockSpec((OUT_TM, 256), lambda i: (i, 0)),
            pl.BlockSpec((OUT_TM, 512), lambda i: (i, 0)),
            pl.BlockSpec((2, OUT_TM, LANES), lambda i: (0, i, 0)),
            pl.BlockSpec((None, C_WIDTH, C_WIDTH), lambda i: (layer, 0, 0)),
            pl.BlockSpec((None, D_MODEL, D_MODEL), lambda i: (layer, 0, 0)),
            pl.BlockSpec((None, None, 1, D_MODEL), lambda i: (layer, 1, 0, 0)),
            pl.BlockSpec((None, None, 1, D_MODEL), lambda i: (layer, 1, 0, 0)),
        ],
        out_specs=pl.BlockSpec((OUT_TM, D_MODEL), lambda i: (i, 0)),
        scratch_shapes=[pltpu.VMEM((D_MODEL, D_MODEL), BF16)],
        compiler_params=_cparams("arbitrary"),
        name="out_project",
    )(x, mod, oa, ob, f, w_fourier, w_out, ln_g, ln_b)


def kernel(x_prompt, x_sample, cache_a_k, cache_a_v, cache_b_k, cache_b_v, c, c_ctx, w_mod, b_mod, w_in, g_qa, g_ka, lam_q1, lam_k1, lam_q2, lam_k2, g_subln, w_fourier, w_out, w_ffn1_gu, w_ffn1_down, w_ffn2_gu, w_ffn2_down, ln_g, ln_b):
    batch, seq, d = x_prompt.shape
    dec_batch, dec_seq, _ = x_sample.shape
    past = cache_a_k.shape[2]
    assert d == D_MODEL and dec_seq == SEG_LEN and batch * seq == SEG_LEN and dec_batch == 2

    cvec = jnp.concatenate([c_ctx[None, :], c, jnp.zeros((MOD_ROWS - 1 - dec_batch, d), F32)], axis=0)
    mod = _modulation(cvec, w_mod, b_mod)

    gq = jnp.tile(g_qa, (1, A_HEADS)).reshape(DEPTH, 1, A_HEADS * HEAD_DIM)
    gk = jnp.tile(g_ka, (1, A_KV_HEADS)).reshape(DEPTH, 1, A_KV_HEADS * HEAD_DIM)
    gs = g_subln.reshape(DEPTH, 1, B_V_DIM)
    lam_params = jnp.stack([lam_q1, lam_k1, lam_q2, lam_k2], axis=1)
    ln_g4 = ln_g.reshape(DEPTH, 3, 1, d)
    ln_b4 = ln_b.reshape(DEPTH, 3, 1, d)
    pmean = _group_mean_matrix(256, HEAD_DIM).astype(BF16)
    rope_tabs = _rope_tables(dec_seq)
    bcs = _channel_dft().astype(BF16)
    cs_ctx = _position_dft_small(seq).astype(BF16)
    m1, m2, tw_cos, tw_sin = _two_stage_dft(GRID_W)
    m1, m2 = m1.astype(BF16), m2.astype(BF16)
    ck_a = cache_a_k.reshape(dec_batch, DEPTH, past, A_KV_HEADS * HEAD_DIM)
    cv_a = cache_a_v.reshape(dec_batch, DEPTH, past, A_KV_HEADS * HEAD_DIM)
    ck_b = cache_b_k.reshape(dec_batch, DEPTH, past, B_HEADS * 2 * B_QK_DIM)
    cv_b = cache_b_v.reshape(dec_batch, DEPTH, past, B_HEADS * B_V_DIM)

    xp = x_prompt.reshape(batch * seq, d)
    xs = x_sample.reshape(dec_batch * dec_seq, d)
    new = []
    for layer in range(DEPTH):
        lam_init = 0.8 - 0.6 * math.exp(-0.3 * layer)
        xp, xs = _ffn(xp, xs, mod, layer, 0, w_ffn1_gu, w_ffn1_down, ln_g4, ln_b4)
        earlier = tuple(new) if layer == DEPTH - 1 else ()
        pp, ucp, *caches = _project(xp, mod, layer, 0, w_in, gq, gk, pmean, None, seq, earlier)
        new = caches if layer == DEPTH - 1 else new + caches
        oa, ob = _context_attention(pp, layer, lam_params, gs, lam_init, seq)
        fo = _context_fourier(ucp, bcs, cs_ctx, seq)
        xp = _out_project(xp, mod, layer, 0, oa, ob, fo, w_fourier, w_out, ln_g4, ln_b4)
        ps, ucs = _project(xs, mod, layer, 1, w_in, gq, gk, pmean, rope_tabs)
        oa = _latent_gqa(ps, ck_a, cv_a, layer, dec_batch, dec_seq)
        ob = _latent_diff(ps, ck_b, cv_b, layer, lam_params, gs, lam_init, dec_batch, dec_seq)
        fo = _latent_fourier(ucs, m1, m2, bcs, tw_cos, tw_sin, dec_seq)
        xs = _out_project(xs, mod, layer, 1, oa, ob, fo, w_fourier, w_out, ln_g4, ln_b4)
        xp, xs = _ffn(xp, xs, mod, layer, 2, w_ffn2_gu, w_ffn2_down, ln_g4, ln_b4)

    new_a_k, new_a_v, new_b_k, new_b_v = new
    return (xp.reshape(batch, seq, d), xs.reshape(dec_batch, dec_seq, d),
            new_a_k.reshape(batch, DEPTH, seq, A_KV_HEADS, HEAD_DIM),
            new_a_v.reshape(batch, DEPTH, seq, A_KV_HEADS, HEAD_DIM),
            new_b_k.reshape(batch, DEPTH, seq, B_HEADS, 2 * B_QK_DIM),
            new_b_v.reshape(batch, DEPTH, seq, B_HEADS, B_V_DIM))
```

```python
import functools
import math

import numpy as np
import jax
import jax.numpy as jnp
from jax import lax
from jax.experimental import pallas as pl
from jax.experimental.pallas import tpu as pltpu

F32 = jnp.float32
BF16 = jnp.bfloat16

D_MODEL = 1024
DEPTH = 2
GRID_W = 64
HEAD_DIM = 64
ROPE_THETA = 10000.0
A_HEADS = 4
A_KV_HEADS = 2
B_HEADS = 4
B_QK_DIM = 64
B_V_DIM = 128
C_GROUPS = 4
C_GROUP_DIM = 64
C_WIDTH = 256
FFN_DIM = 2816
N_MOD = 9
IN_WIDTH = 2304
ALPHA = (2.0 * DEPTH) ** 0.25
LN_EPS = 1e-5
RMS_EPS = 1e-6

LANES = 128
MOD_ROWS = 8
SEG_LEN = 4096
VMEM_LIMIT = 52 << 20

QA_OFF, KA_OFF, VA_OFF, QB_OFF, KB_OFF, VB_OFF = 0, 512, 640, 768, 1280, 1792
PROJ_W = 2304
CACHE_WIDTHS = (128, 128, 512, 512)

FFN_TM = 512
FFN_TF = 256
FFN_STAGES = 3
PROJ_TM = 512
OUT_TM = 1024
ATT_TQ = 256
ATT_UNITS = 4
ATT_SLOTS = 2
ATT_KC = 512
LOG2E = math.log2(math.e)


def _cparams(*sem):
    return pltpu.CompilerParams(dimension_semantics=sem, vmem_limit_bytes=VMEM_LIMIT)


def _rope_tables(length):
    rows = length // GRID_W
    row = np.repeat(np.arange(rows), GRID_W).astype(np.float64)
    col = np.tile(np.arange(GRID_W), rows).astype(np.float64)
    half = HEAD_DIM // 2
    inv = 1.0 / (ROPE_THETA ** (np.arange(0, half, 2, dtype=np.float64) / half))
    ar = row[:, None] * inv
    ac = col[:, None] * inv
    cos = np.concatenate([np.cos(ar), np.cos(ar), np.cos(ac), np.cos(ac)], -1)
    sin = np.concatenate([-np.sin(ar), np.sin(ar), -np.sin(ac), np.sin(ac)], -1)
    return (jnp.asarray(np.tile(cos, (1, 2)), F32), jnp.asarray(np.tile(sin, (1, 2)), F32))


def _group_mean_matrix(width, group):
    m = np.kron(np.eye(width // group), np.full((group, group), 1.0 / group))
    return jnp.asarray(m, F32)


def _channel_dft():
    c = np.arange(C_GROUP_DIM)
    ang = 2.0 * np.pi * np.outer(c, c) / C_GROUP_DIM
    eye = np.eye(C_GROUPS)
    return jnp.asarray(np.concatenate([np.kron(eye, np.cos(ang)), np.kron(eye, np.sin(ang))], 0), F32)


def _position_dft_small(length):
    n = np.arange(length)
    ang = 2.0 * np.pi * ((np.outer(n, n)) % length) / length
    return jnp.asarray(np.concatenate([np.cos(ang), -np.sin(ang)], 1), F32)


def _two_stage_dft(n1):
    i = np.arange(n1)
    ang = 2.0 * np.pi * np.outer(i, i) / n1
    c, s = np.cos(ang), np.sin(ang)
    m1 = np.concatenate([c, -s], 0)
    m2 = np.block([[c, s], [-s, c]])
    tw = 2.0 * np.pi * np.outer(i, i).reshape(-1) / (n1 * n1)
    tc = np.repeat(np.cos(tw)[:, None], LANES, 1)
    ts = np.repeat(np.sin(tw)[:, None], LANES, 1)
    return tuple(jnp.asarray(a, F32) for a in (m1, m2, tc, ts))


def _layer_norm(y, g, b):
    mu = jnp.mean(y, axis=-1, keepdims=True)
    yc = y - mu
    var = jnp.mean(yc * yc, axis=-1, keepdims=True)
    return yc * lax.rsqrt(var + LN_EPS) * g + b


def _dot(a, b):
    return jnp.dot(a, b, preferred_element_type=F32)


def _dot_nt(a, b):
    return lax.dot_general(a, b, (((1,), (1,)), ((), ())), preferred_element_type=F32)


def _dot_split(a, b):
    hi = a.astype(BF16)
    lo = (a - hi.astype(F32)).astype(BF16)
    return _dot(hi, b) + _dot(lo, b)


def _softmax_pv_units(units, s_ref):
    rows = units[0][0].shape[0]
    chunk_lists = []
    for _, pieces in units:
        chunks, off = [], 0
        for k_ref, v_ref in pieces:
            for st in range(0, k_ref.shape[0], ATT_KC):
                sz = min(ATT_KC, k_ref.shape[0] - st)
                chunks.append((k_ref, v_ref, st, sz, off))
                off += sz
        chunk_lists.append(chunks)
    nchunk = len(chunk_lists[0])
    run_max = [None] * len(units)
    row_max = [None] * len(units)
    acc = [None] * len(units)

    def scores(u, ci):
        k_ref, _, st, sz, off = chunk_lists[u][ci]
        s = _dot_nt(units[u][0], k_ref[st:st + sz, :].astype(BF16))
        s_ref[u % ATT_SLOTS, :, off:off + sz] = s
        part = functools.reduce(jnp.maximum, [s[:, j:j + LANES] for j in range(0, sz, LANES)])
        run_max[u] = part if run_max[u] is None else jnp.maximum(run_max[u], part)

    def finish_max(u):
        m = jnp.max(run_max[u], axis=1, keepdims=True)
        row_max[u] = jnp.broadcast_to(m, (rows, LANES))

    def weighted(u, ci):
        _, v_ref, st, sz, off = chunk_lists[u][ci]
        e = jnp.concatenate([jnp.exp2(s_ref[u % ATT_SLOTS, :, off + j:off + j + LANES] - row_max[u])
                             for j in range(0, sz, LANES)], axis=1).astype(BF16)
        v1 = jnp.concatenate([v_ref[st:st + sz, :].astype(BF16), jnp.ones((sz, LANES), BF16)], axis=1)
        d = _dot(e, v1)
        acc[u] = d if acc[u] is None else acc[u] + d

    for ci in range(nchunk):
        scores(0, ci)
    finish_max(0)
    for u in range(1, len(units)):
        for ci in range(nchunk):
            weighted(u - 1, ci)
            scores(u, ci)
        finish_max(u)
    for ci in range(nchunk):
        weighted(len(units) - 1, ci)
    return [a[:, :LANES] / a[:, LANES:] for a in acc]


def _gqa_query(qpad):
    return jnp.concatenate([qpad[:, :LANES], qpad[:, LANES:]], axis=0)


def _gqa_output(o, group):
    tq = o.shape[0] // 2
    o0, o1 = o[:tq], o[tq:]
    low = lax.broadcasted_iota(jnp.int32, (tq, LANES), 1) < HEAD_DIM
    first = group == 0
    r0, r1 = pltpu.roll(o0, HEAD_DIM, 1), pltpu.roll(o1, HEAD_DIM, 1)
    return jnp.where(low, jnp.where(first, o0, r0), jnp.where(first, r1, o1))


def _diff_query(q):
    low = lax.broadcasted_iota(jnp.int32, q.shape, 1) < B_QK_DIM
    zero = jnp.zeros_like(q)
    return jnp.concatenate([jnp.where(low, q, zero), jnp.where(low, zero, q)], axis=0)


def _diff_output(o, lam, g_sub, lam_init):
    tq = o.shape[0] // 2
    w = o[:tq] - lam * o[tq:]
    y = w * lax.rsqrt(jnp.mean(w * w, axis=-1, keepdims=True) + RMS_EPS)
    return y * g_sub * (1.0 - lam_init)


def _lambda(lp):
    s1 = jnp.sum(lp[0:1] * lp[1:2], axis=1, keepdims=True)
    s2 = jnp.sum(lp[2:3] * lp[3:4], axis=1, keepdims=True)
    return jnp.exp(s1) - jnp.exp(s2)


def _mod_kernel(c_ref, w_ref, b_ref, o_ref):
    c = c_ref[...]
    s = (c * jax.nn.sigmoid(c)).astype(BF16)
    o_ref[...] = _dot(s, w_ref[...].astype(BF16)) + b_ref[...]


def _modulation(cvec, w_mod, b_mod):
    out = pl.pallas_call(
        _mod_kernel,
        out_shape=jax.ShapeDtypeStruct((DEPTH, N_MOD, MOD_ROWS, D_MODEL), F32),
        grid=(DEPTH, N_MOD),
        in_specs=[
            pl.BlockSpec((MOD_ROWS, D_MODEL), lambda l, n: (0, 0)),
            pl.BlockSpec((None, D_MODEL, D_MODEL), lambda l, n: (l, 0, n)),
            pl.BlockSpec((None, None, 1, D_MODEL), lambda l, n: (l, n, 0, 0)),
        ],
        out_specs=pl.BlockSpec((None, None, MOD_ROWS, D_MODEL), lambda l, n: (l, n, 0, 0)),
        compiler_params=_cparams("parallel", "parallel"),
        name="modulation",
    )(cvec, w_mod, b_mod.reshape(DEPTH, N_MOD, 1, D_MODEL))
    return out.reshape(DEPTH, N_MOD, MOD_ROWS, 1, D_MODEL)


def _mod_spec(layer, chunk, row0, tm):
    return pl.BlockSpec((None, None, None, 1, D_MODEL),
                        lambda i: (layer, chunk, row0 + (i * tm) // SEG_LEN, 0, 0))


def _ffn_kernel(layer, n_ctx, xp_ref, xs_ref, sh_ref, sc_ref, gt_ref, wgu_hbm, wd_hbm, lng_ref, lnb_ref,
                op_ref, os_ref, wg_ref, wu_ref, wd_ref, stage_gu, stage_d, sem, h_ref, acc_ref):
    nf = FFN_DIM // FFN_TF
    step = pl.program_id(0)
    first_step = step == 0
    is_ctx = step < n_ctx

    def chunk_copies(c, slot):
        col = c * FFN_TF if isinstance(c, int) else pl.multiple_of(c * FFN_TF, FFN_TF)
        return (
            pltpu.make_async_copy(wgu_hbm.at[layer, :, pl.ds(col, FFN_TF)], stage_gu.at[slot, 0], sem.at[slot, 0]),
            pltpu.make_async_copy(wgu_hbm.at[layer, :, pl.ds(FFN_DIM + col, FFN_TF)], stage_gu.at[slot, 1],
                                  sem.at[slot, 1]),
            pltpu.make_async_copy(wd_hbm.at[layer, pl.ds(col, FFN_TF), :], stage_d.at[slot], sem.at[slot, 2]),
        )

    def start_chunk(c, slot):
        for cp in chunk_copies(c, slot):
            cp.start()

    def modulated(x):
        return (x * (1.0 + sc_ref[...]) + sh_ref[...]).astype(BF16)

    def chunk(h, c):
        g = _dot(h, wg_ref[c])
        u = _dot(h, wu_ref[c])
        a = (g * jax.nn.sigmoid(g) * u).astype(BF16)
        return _dot(a, wd_ref[c])

    def finish(x, acc, o_ref):
        y = ALPHA * x + (0.5 * gt_ref[...]) * acc
        o_ref[...] = _layer_norm(y, lng_ref[...], lnb_ref[...])

    def later_step(x_ref, o_ref):
        x = x_ref[...]
        h = modulated(x)
        acc = chunk(h, 0)
        for c in range(1, nf):
            acc = acc + chunk(h, c)
        finish(x, acc, o_ref)

    @pl.when(first_step)
    def _():
        for c in range(FFN_STAGES - 1):
            start_chunk(c, c)
        x = xp_ref[...]
        h_ref[...] = modulated(x)
        acc_ref[...] = jnp.zeros_like(acc_ref)

        def body(c, carry):
            ahead = c + (FFN_STAGES - 1)

            @pl.when(ahead < nf)
            def _():
                start_chunk(ahead, lax.rem(ahead, FFN_STAGES))

            slot = lax.rem(c, FFN_STAGES)
            for cp in chunk_copies(c, slot):
                cp.wait()
            wg_ref[c] = stage_gu[slot, 0].astype(BF16)
            wu_ref[c] = stage_gu[slot, 1].astype(BF16)
            wd_ref[c] = stage_d[slot].astype(BF16)
            acc_ref[...] += chunk(h_ref[...], c)
            return carry

        lax.fori_loop(0, nf, body, 0)
        finish(x, acc_ref[...], op_ref)

    @pl.when(jnp.logical_and(jnp.logical_not(first_step), is_ctx))
    def _():
        later_step(xp_ref, op_ref)

    @pl.when(jnp.logical_not(is_ctx))
    def _():
        later_step(xs_ref, os_ref)


def _ffn(xp, xs, mod, layer, sub, w_gu, w_down, ln_g, ln_b):
    n_ctx = xp.shape[0] // FFN_TM
    n_lat = xs.shape[0] // FFN_TM
    nf = FFN_DIM // FFN_TF
    assert n_ctx >= 1 and n_lat >= 1 and nf >= FFN_STAGES

    def ctx_map(i):
        return (jnp.minimum(i, n_ctx - 1), 0)

    def lat_map(i):
        return (jnp.maximum(i - n_ctx, 0), 0)

    def mod_spec(chunk):
        def index(i):
            row = jnp.where(i < n_ctx, 0, 1 + (jnp.maximum(i - n_ctx, 0) * FFN_TM) // SEG_LEN)
            return (layer, chunk, row, 0, 0)
        return pl.BlockSpec((None, None, None, 1, D_MODEL), index)

    return pl.pallas_call(
        functools.partial(_ffn_kernel, layer, n_ctx),
        out_shape=[jax.ShapeDtypeStruct(xp.shape, F32), jax.ShapeDtypeStruct(xs.shape, F32)],
        grid=(n_ctx + n_lat,),
        in_specs=[
            pl.BlockSpec((FFN_TM, D_MODEL), ctx_map),
            pl.BlockSpec((FFN_TM, D_MODEL), lat_map),
            mod_spec(3 * sub), mod_spec(3 * sub + 1), mod_spec(3 * sub + 2),
            pl.BlockSpec(memory_space=pl.ANY),
            pl.BlockSpec(memory_space=pl.ANY),
            pl.BlockSpec((None, None, 1, D_MODEL), lambda i: (layer, sub, 0, 0)),
            pl.BlockSpec((None, None, 1, D_MODEL), lambda i: (layer, sub, 0, 0)),
        ],
        out_specs=[pl.BlockSpec((FFN_TM, D_MODEL), ctx_map), pl.BlockSpec((FFN_TM, D_MODEL), lat_map)],
        scratch_shapes=[
            pltpu.VMEM((nf, D_MODEL, FFN_TF), BF16),
            pltpu.VMEM((nf, D_MODEL, FFN_TF), BF16),
            pltpu.VMEM((nf, FFN_TF, D_MODEL), BF16),
            pltpu.VMEM((FFN_STAGES, 2, D_MODEL, FFN_TF), F32),
            pltpu.VMEM((FFN_STAGES, FFN_TF, D_MODEL), F32),
            pltpu.SemaphoreType.DMA((FFN_STAGES, 3)),
            pltpu.VMEM((FFN_TM, D_MODEL), BF16),
            pltpu.VMEM((FFN_TM, D_MODEL), F32),
        ],
        compiler_params=_cparams("arbitrary"),
        name="ffn",
    )(xp, xs, mod, mod, mod, w_gu, w_down, ln_g, ln_b)


def _rope(x, cos, sin):
    quarter = HEAD_DIM // 4
    lane = lax.broadcasted_iota(jnp.int32, x.shape, 1)
    first = (lane % (2 * quarter)) < quarter
    rot = jnp.where(first, pltpu.roll(x, LANES - quarter, 1), pltpu.roll(x, quarter, 1))
    return x * cos + rot * sin


def _proj_kernel(rope, n_earlier, x_ref, sh_ref, sc_ref, w_ref, gq_ref, gk_ref, pm_ref, *rest):
    if rope:
        cos_ref, sin_ref, p_ref, uc_ref, wb_ref = rest
    else:
        earlier, rest = rest[:n_earlier], rest[n_earlier:]
        p_ref, uc_ref, nak_ref, nav_ref, nbk_ref, nbv_ref, wb_ref = rest

    @pl.when(pl.program_id(0) == 0)
    def _():
        wb_ref[...] = w_ref[...].astype(BF16)

    h = (x_ref[...] * (1.0 + sc_ref[...]) + sh_ref[...]).astype(BF16)
    proj = _dot(h, wb_ref[...])
    tm = proj.shape[0]

    qa = proj[:, 0:256]
    qa = qa * lax.rsqrt(_dot_split(qa * qa, pm_ref[...]) + RMS_EPS) * gq_ref[...]
    ka = proj[:, 256:384]
    ka = ka * lax.rsqrt(_dot_split(ka * ka, pm_ref[0:LANES, 0:LANES]) + RMS_EPS) * gk_ref[...]
    va = proj[:, 384:512]
    qb = proj[:, 512:1024]
    kb = proj[:, 1024:1536]
    vb = proj[:, 1536:2048]
    uc_ref[0] = proj[:, 2048:2048 + LANES]
    uc_ref[1] = proj[:, 2048 + LANES:2304]

    if not rope:
        for n, (ref, val) in enumerate(((nak_ref, ka), (nav_ref, va), (nbk_ref, kb), (nbv_ref, vb))):
            if n_earlier:
                layers = [e[...] for e in earlier[n::len(CACHE_WIDTHS)]] + [val]
                for l, v in enumerate(layers):
                    ref[:, l] = v.reshape(ref.shape[0], ref.shape[2], ref.shape[3])
            else:
                ref[...] = val

    def slabs(a):
        return [a[:, s:s + LANES] for s in range(0, a.shape[1], LANES)]

    if rope:
        cos, sin = cos_ref[...], sin_ref[...]
        qa_s = [_rope(s, cos, sin) for s in slabs(qa)]
        ka_s = [_rope(ka, cos, sin)]
        qb_s = [_rope(s, cos, sin) for s in slabs(qb)]
        kb_s = [_rope(s, cos, sin) for s in slabs(kb)]
    else:
        qa_s, ka_s, qb_s, kb_s = slabs(qa), [ka], slabs(qb), slabs(kb)

    scale = LOG2E * HEAD_DIM ** -0.5
    low = lax.broadcasted_iota(jnp.int32, (tm, LANES), 1) < HEAD_DIM
    zero = jnp.zeros((tm, LANES), F32)
    g0, g1 = qa_s[0] * scale, qa_s[1] * scale
    qa_pad = [jnp.where(low, g0, zero), jnp.where(low, pltpu.roll(g0, HEAD_DIM, 1), zero),
              jnp.where(low, zero, pltpu.roll(g1, HEAD_DIM, 1)), jnp.where(low, zero, g1)]
    pieces = qa_pad + ka_s + [va] + [s * scale for s in qb_s] + kb_s + slabs(vb)
    for n, piece in enumerate(pieces):
        p_ref[:, n * LANES:(n + 1) * LANES] = piece.astype(BF16)


def _project(x, mod, layer, row0, w_in, gq, gk, pmean, rope_tabs, seq=None, earlier=()):
    t = x.shape[0]
    rope = rope_tabs is not None
    nt = t // PROJ_TM
    in_specs = [
        pl.BlockSpec((PROJ_TM, D_MODEL), lambda i: (i, 0)),
        _mod_spec(layer, 3, row0, PROJ_TM),
        _mod_spec(layer, 4, row0, PROJ_TM),
        pl.BlockSpec((None, D_MODEL, IN_WIDTH), lambda i: (layer, 0, 0)),
        pl.BlockSpec((None, 1, 256), lambda i: (layer, 0, 0)),
        pl.BlockSpec((None, 1, LANES), lambda i: (layer, 0, 0)),
        pl.BlockSpec((256, 256), lambda i: (0, 0)),
    ]
    args = [x, mod, mod, w_in, gq, gk, pmean]
    out_shape = [jax.ShapeDtypeStruct((t, PROJ_W), BF16), jax.ShapeDtypeStruct((2, t, LANES), F32)]
    out_specs = [pl.BlockSpec((PROJ_TM, PROJ_W), lambda i: (i, 0)),
                 pl.BlockSpec((2, PROJ_TM, LANES), lambda i: (0, i, 0))]
    if rope:
        per_seg = SEG_LEN // PROJ_TM
        in_specs += [pl.BlockSpec((PROJ_TM, LANES), lambda i: (i % per_seg, 0))] * 2
        args += list(rope_tabs)
    else:
        for e, width in zip(earlier, CACHE_WIDTHS * (len(earlier) // len(CACHE_WIDTHS))):
            in_specs.append(pl.BlockSpec((PROJ_TM, width), lambda i: (i, 0)))
            args.append(e)
        for width in CACHE_WIDTHS:
            if earlier:
                out_shape.append(jax.ShapeDtypeStruct((t // seq, DEPTH, seq, width), F32))
                out_specs.append(pl.BlockSpec((PROJ_TM // seq, DEPTH, seq, width), lambda i: (i, 0, 0, 0)))
            else:
                out_shape.append(jax.ShapeDtypeStruct((t, width), F32))
                out_specs.append(pl.BlockSpec((PROJ_TM, width), lambda i: (i, 0)))
    return pl.pallas_call(
        functools.partial(_proj_kernel, rope, len(earlier)),
        out_shape=out_shape,
        grid=(nt,),
        in_specs=in_specs,
        out_specs=out_specs,
        scratch_shapes=[pltpu.VMEM((D_MODEL, IN_WIDTH), BF16)],
        compiler_params=_cparams("arbitrary"),
        name="project_latent" if rope else "project_context",
    )(*args)


def _ctx_attn_kernel(lam_init, p_ref, lp_ref, gs_ref, oa_ref, ob_ref, s_ref):
    ka = p_ref.at[:, KA_OFF:KA_OFF + LANES]
    va = p_ref.at[:, VA_OFF:VA_OFF + LANES]
    units = []
    for g in range(A_KV_HEADS):
        units.append((_gqa_query(p_ref[:, QA_OFF + 256 * g:QA_OFF + 256 * (g + 1)]), [(ka, va)]))
    for h in range(B_HEADS):
        k = p_ref.at[:, KB_OFF + h * LANES:KB_OFF + (h + 1) * LANES]
        v = p_ref.at[:, VB_OFF + h * LANES:VB_OFF + (h + 1) * LANES]
        units.append((_diff_query(p_ref[:, QB_OFF + h * LANES:QB_OFF + (h + 1) * LANES]), [(k, v)]))
    outs = _softmax_pv_units(units, s_ref)
    lam = _lambda(lp_ref[...]) + lam_init
    for g in range(A_KV_HEADS):
        oa_ref[:, g * LANES:(g + 1) * LANES] = _gqa_output(outs[g], g).astype(BF16)
    for h in range(B_HEADS):
        o = _diff_output(outs[A_KV_HEADS + h], lam, gs_ref[...], lam_init)
        ob_ref[:, h * LANES:(h + 1) * LANES] = o.astype(BF16)


def _context_attention(p, layer, lam_params, g_subln, lam_init, seq):
    t = p.shape[0]
    return pl.pallas_call(
        functools.partial(_ctx_attn_kernel, lam_init),
        out_shape=[jax.ShapeDtypeStruct((t, 256), BF16), jax.ShapeDtypeStruct((t, 512), BF16)],
        grid=(t // seq,),
        in_specs=[
            pl.BlockSpec((seq, PROJ_W), lambda i: (i, 0)),
            pl.BlockSpec((None, 4, B_QK_DIM), lambda i: (layer, 0, 0)),
            pl.BlockSpec((None, 1, B_V_DIM), lambda i: (layer, 0, 0)),
        ],
        out_specs=[pl.BlockSpec((seq, 256), lambda i: (i, 0)), pl.BlockSpec((seq, 512), lambda i: (i, 0))],
        scratch_shapes=[pltpu.VMEM((ATT_SLOTS, 2 * seq, seq), F32)],
        compiler_params=_cparams("parallel"),
        name="context_attention",
    )(p, lam_params, g_subln)


def _lat_gqa_kernel(q_ref, kn_ref, vn_ref, kc_ref, vc_ref, o_ref, s_ref):
    pieces = [(kc_ref, vc_ref), (kn_ref, vn_ref)]
    units = [(_gqa_query(q_ref[u * ATT_TQ:(u + 1) * ATT_TQ, :]), pieces) for u in range(ATT_UNITS)]
    outs = _softmax_pv_units(units, s_ref)
    for u in range(ATT_UNITS):
        o_ref[u * ATT_TQ:(u + 1) * ATT_TQ, :] = _gqa_output(outs[u], pl.program_id(1)).astype(BF16)


def _latent_gqa(p, cache_k, cache_v, layer, batch, length):
    tile = ATT_TQ * ATT_UNITS
    nq = length // tile
    past = cache_k.shape[2]
    return pl.pallas_call(
        _lat_gqa_kernel,
        out_shape=jax.ShapeDtypeStruct((batch * length, 256), BF16),
        grid=(batch, A_KV_HEADS, nq),
        in_specs=[
            pl.BlockSpec((tile, 256), lambda b, g, i: (b * nq + i, QA_OFF // 256 + g)),
            pl.BlockSpec((length, LANES), lambda b, g, i: (b, KA_OFF // LANES)),
            pl.BlockSpec((length, LANES), lambda b, g, i: (b, VA_OFF // LANES)),
            pl.BlockSpec((None, None, past, LANES), lambda b, g, i: (b, layer, 0, 0)),
            pl.BlockSpec((None, None, past, LANES), lambda b, g, i: (b, layer, 0, 0)),
        ],
        out_specs=pl.BlockSpec((tile, LANES), lambda b, g, i: (b * nq + i, g)),
        scratch_shapes=[pltpu.VMEM((ATT_SLOTS, 2 * ATT_TQ, past + length), F32)],
        compiler_params=_cparams("parallel", "parallel", "parallel"),
        name="latent_gqa",
    )(p, p, p, cache_k, cache_v)


def _lat_diff_kernel(lam_init, q_ref, kn_ref, vn_ref, kc_ref, vc_ref, lp_ref, gs_ref, o_ref, s_ref):
    pieces = [(kc_ref, vc_ref), (kn_ref, vn_ref)]
    units = [(_diff_query(q_ref[u * ATT_TQ:(u + 1) * ATT_TQ, :]), pieces) for u in range(ATT_UNITS)]
    outs = _softmax_pv_units(units, s_ref)
    lam = _lambda(lp_ref[...]) + lam_init
    for u in range(ATT_UNITS):
        o = _diff_output(outs[u], lam, gs_ref[...], lam_init)
        o_ref[u * ATT_TQ:(u + 1) * ATT_TQ, :] = o.astype(BF16)


def _latent_diff(p, cache_k, cache_v, layer, lam_params, g_subln, lam_init, batch, length):
    tile = ATT_TQ * ATT_UNITS
    nq = length // tile
    past = cache_k.shape[2]
    return pl.pallas_call(
        functools.partial(_lat_diff_kernel, lam_init),
        out_shape=jax.ShapeDtypeStruct((batch * length, 512), BF16),
        grid=(batch, B_HEADS, nq),
        in_specs=[
            pl.BlockSpec((tile, LANES), lambda b, h, i: (b * nq + i, QB_OFF // LANES + h)),
            pl.BlockSpec((length, LANES), lambda b, h, i: (b, KB_OFF // LANES + h)),
            pl.BlockSpec((length, LANES), lambda b, h, i: (b, VB_OFF // LANES + h)),
            pl.BlockSpec((None, None, past, LANES), lambda b, h, i: (b, layer, 0, h)),
            pl.BlockSpec((None, None, past, LANES), lambda b, h, i: (b, layer, 0, h)),
            pl.BlockSpec((None, 4, B_QK_DIM), lambda b, h, i: (layer, 0, 0)),
            pl.BlockSpec((None, 1, B_V_DIM), lambda b, h, i: (layer, 0, 0)),
        ],
        out_specs=pl.BlockSpec((tile, LANES), lambda b, h, i: (b * nq + i, h)),
        scratch_shapes=[pltpu.VMEM((ATT_SLOTS, 2 * ATT_TQ, past + length), F32)],
        compiler_params=_cparams("parallel", "parallel", "parallel"),
        name="latent_diff",
    )(p, p, p, cache_k, cache_v, lam_params, g_subln)


def _ctx_fourier_kernel(norm, u_ref, bcs_ref, cs_ref, o_ref):
    u = jnp.concatenate([u_ref[0], u_ref[1]], axis=1).astype(BF16)
    bcs = bcs_ref[...]
    rhs = jnp.concatenate([_dot(u, bcs[:C_WIDTH]), _dot(u, bcs[C_WIDTH:])], axis=0).astype(BF16)
    f = _dot(cs_ref[...], rhs) * norm
    o_ref[0] = f[:, :LANES]
    o_ref[1] = f[:, LANES:]


def _context_fourier(uc, bcs, cs, seq):
    t = uc.shape[1]
    norm = 1.0 / math.sqrt(seq * C_GROUP_DIM)
    return pl.pallas_call(
        functools.partial(_ctx_fourier_kernel, norm),
        out_shape=jax.ShapeDtypeStruct((2, t, LANES), F32),
        grid=(t // seq,),
        in_specs=[
            pl.BlockSpec((2, seq, LANES), lambda i: (0, i, 0)),
            pl.BlockSpec((2 * C_WIDTH, C_WIDTH), lambda i: (0, 0)),
            pl.BlockSpec((seq, 2 * seq), lambda i: (0, 0)),
        ],
        out_specs=pl.BlockSpec((2, seq, LANES), lambda i: (0, i, 0)),
        compiler_params=_cparams("parallel"),
        name="context_fourier",
    )(uc, bcs, cs)


def _lat_fourier_kernel(norm, u_ref, m1_ref, m2_ref, bcs_ref, tc_ref, ts_ref, o_ref, z_ref):
    n1 = GRID_W

    def strided(ref, plane, start):
        rows = pl.ds(start, n1, stride=n1)
        return jnp.concatenate([ref[(*plane, 0, rows, slice(None))], ref[(*plane, 1, rows, slice(None))]], axis=1)

    m1 = m1_ref[...]
    for b in range(n1):
        ub = strided(u_ref, (), b).astype(BF16)
        y = _dot(m1, ub)
        yr, yi = y[:n1], y[n1:]
        tc = jnp.concatenate([tc_ref[b * n1:(b + 1) * n1, :]] * 2, axis=1)
        ts = jnp.concatenate([ts_ref[b * n1:(b + 1) * n1, :]] * 2, axis=1)
        zr = yr * tc + yi * ts
        zi = yi * tc - yr * ts
        for half in range(2):
            z_ref[0, half, b * n1:(b + 1) * n1, :] = zr[:, half * LANES:(half + 1) * LANES]
            z_ref[1, half, b * n1:(b + 1) * n1, :] = zi[:, half * LANES:(half + 1) * LANES]
    m2 = m2_ref[...]
    bcs = bcs_ref[...]
    for q in range(n1):
        z = jnp.concatenate([strided(z_ref, (0,), q), strided(z_ref, (1,), q)], axis=0).astype(BF16)
        g = _dot(m2, z)
        gcat = jnp.concatenate([g[:n1], g[n1:]], axis=1).astype(BF16)
        f = _dot(gcat, bcs) * norm
        o_ref[0, pl.ds(q, n1, stride=n1), :] = f[:, :LANES]
        o_ref[1, pl.ds(q, n1, stride=n1), :] = f[:, LANES:]


def _latent_fourier(uc, m1, m2, bcs, tc, ts, length):
    t = uc.shape[1]
    norm = 1.0 / math.sqrt(length * C_GROUP_DIM)
    return pl.pallas_call(
        functools.partial(_lat_fourier_kernel, norm),
        out_shape=jax.ShapeDtypeStruct((2, t, LANES), F32),
        grid=(t // length,),
        in_specs=[
            pl.BlockSpec((2, length, LANES), lambda i: (0, i, 0)),
            pl.BlockSpec((2 * GRID_W, GRID_W), lambda i: (0, 0)),
            pl.BlockSpec((2 * GRID_W, 2 * GRID_W), lambda i: (0, 0)),
            pl.BlockSpec((2 * C_WIDTH, C_WIDTH), lambda i: (0, 0)),
            pl.BlockSpec((length, LANES), lambda i: (0, 0)),
            pl.BlockSpec((length, LANES), lambda i: (0, 0)),
        ],
        out_specs=pl.BlockSpec((2, length, LANES), lambda i: (0, i, 0)),
        scratch_shapes=[pltpu.VMEM((2, 2, length, LANES), F32)],
        compiler_params=_cparams("parallel"),
        name="latent_fourier",
    )(uc, m1, m2, bcs, tc, ts)


def _out_kernel(x_ref, gt_ref, oa_ref, ob_ref, f_ref, wf_ref, wo_ref, lng_ref, lnb_ref, o_ref, wob_ref):
    @pl.when(pl.program_id(0) == 0)
    def _():
        wob_ref[...] = wo_ref[...].astype(BF16)

    wf = wf_ref[...].astype(BF16)
    half = OUT_TM // 2
    for r in range(0, OUT_TM, half):
        rows = slice(r, r + half)
        f = jnp.concatenate([f_ref[0, rows, :], f_ref[1, rows, :]], axis=1).astype(BF16)
        oc = _dot(f, wf).astype(BF16)
        cat = jnp.concatenate([oa_ref[rows, :], ob_ref[rows, :], oc], axis=1)
        mo = _dot(cat, wob_ref[...])
        y = ALPHA * x_ref[rows, :] + gt_ref[...] * mo
        o_ref[rows, :] = _layer_norm(y, lng_ref[...], lnb_ref[...])


def _out_project(x, mod, layer, row0, oa, ob, f, w_fourier, w_out, ln_g, ln_b):
    t = x.shape[0]
    return pl.pallas_call(
        _out_kernel,
        out_shape=jax.ShapeDtypeStruct((t, D_MODEL), F32),
        grid=(t // OUT_TM,),
        in_specs=[
            pl.BlockSpec((OUT_TM, D_MODEL), lambda i: (i, 0)),
            _mod_spec(layer, 5, row0, OUT_TM),
            pl.BlockSpec((OUT_TM, 256), lambda i: (i, 0)),
            pl.BlockSpec((OUT_TM, 512), lambda i: (i, 0)),
            pl.BlockSpec((2, OUT_TM, LANES), lambda i: (0, i, 0)),
            pl.BlockSpec((None, C_WIDTH, C_WIDTH), lambda i: (layer, 0, 0)),
            pl.BlockSpec((None, D_MODEL, D_MODEL), lambda i: (layer, 0, 0)),
            pl.BlockSpec((None, None, 1, D_MODEL), lambda i: (layer, 1, 0, 0)),
            pl.BlockSpec((None, None, 1, D_MODEL), lambda i: (layer, 1, 0, 0)),
        ],
        out_specs=pl.BlockSpec((OUT_TM, D_MODEL), lambda i: (i, 0)),
        scratch_shapes=[pltpu.VMEM((D_MODEL, D_MODEL), BF16)],
        compiler_params=_cparams("arbitrary"),
        name="out_project",
    )(x, mod, oa, ob, f, w_fourier, w_out, ln_g, ln_b)


def kernel(x_prompt, x_sample, cache_a_k, cache_a_v, cache_b_k, cache_b_v, c, c_ctx, w_mod, b_mod, w_in, g_qa, g_ka, lam_q1, lam_k1, lam_q2, lam_k2, g_subln, w_fourier, w_out, w_ffn1_gu, w_ffn1_down, w_ffn2_gu, w_ffn2_down, ln_g, ln_b):
    batch, seq, d = x_prompt.shape
    dec_batch, dec_seq, _ = x_sample.shape
    past = cache_a_k.shape[2]
    assert d == D_MODEL and dec_seq == SEG_LEN and batch * seq == SEG_LEN and dec_batch == 2

    cvec = jnp.concatenate([c_ctx[None, :], c, jnp.zeros((MOD_ROWS - 1 - dec_batch, d), F32)], axis=0)
    mod = _modulation(cvec, w_mod, b_mod)

    gq = jnp.tile(g_qa, (1, A_HEADS)).reshape(DEPTH, 1, A_HEADS * HEAD_DIM)
    gk = jnp.tile(g_ka, (1, A_KV_HEADS)).reshape(DEPTH, 1, A_KV_HEADS * HEAD_DIM)
    gs = g_subln.reshape(DEPTH, 1, B_V_DIM)
    lam_params = jnp.stack([lam_q1, lam_k1, lam_q2, lam_k2], axis=1)
    ln_g4 = ln_g.reshape(DEPTH, 3, 1, d)
    ln_b4 = ln_b.reshape(DEPTH, 3, 1, d)
    pmean = _group_mean_matrix(256, HEAD_DIM).astype(BF16)
    rope_tabs = _rope_tables(dec_seq)
    bcs = _channel_dft().astype(BF16)
    cs_ctx = _position_dft_small(seq).astype(BF16)
    m1, m2, tw_cos, tw_sin = _two_stage_dft(GRID_W)
    m1, m2 = m1.astype(BF16), m2.astype(BF16)
    ck_a = cache_a_k.reshape(dec_batch, DEPTH, past, A_KV_HEADS * HEAD_DIM)
    cv_a = cache_a_v.reshape(dec_batch, DEPTH, past, A_KV_HEADS * HEAD_DIM)
    ck_b = cache_b_k.reshape(dec_batch, DEPTH, past, B_HEADS * 2 * B_QK_DIM)
    cv_b = cache_b_v.reshape(dec_batch, DEPTH, past, B_HEADS * B_V_DIM)

    xp = x_prompt.reshape(batch * seq, d)
    xs = x_sample.reshape(dec_batch * dec_seq, d)
    new = []
    for layer in range(DEPTH):
        lam_init = 0.8 - 0.6 * math.exp(-0.3 * layer)
        xp, xs = _ffn(xp, xs, mod, layer, 0, w_ffn1_gu, w_ffn1_down, ln_g4, ln_b4)
        earlier = tuple(new) if layer == DEPTH - 1 else ()
        pp, ucp, *caches = _project(xp, mod, layer, 0, w_in, gq, gk, pmean, None, seq, earlier)
        new = caches if layer == DEPTH - 1 else new + caches
        oa, ob = _context_attention(pp, layer, lam_params, gs, lam_init, seq)
        fo = _context_fourier(ucp, bcs, cs_ctx, seq)
        xp = _out_project(xp, mod, layer, 0, oa, ob, fo, w_fourier, w_out, ln_g4, ln_b4)
        ps, ucs = _project(xs, mod, layer, 1, w_in, gq, gk, pmean, rope_tabs)
        oa = _latent_gqa(ps, ck_a, cv_a, layer, dec_batch, dec_seq)
        ob = _latent_diff(ps, ck_b, cv_b, layer, lam_params, gs, lam_init, dec_batch, dec_seq)
        fo = _latent_fourier(ucs, m1, m2, bcs, tw_cos, tw_sin, dec_seq)
        xs = _out_project(xs, mod, layer, 1, oa, ob, fo, w_fourier, w_out, ln_g4, ln_b4)
        xp, xs = _ffn(xp, xs, mod, layer, 2, w_ffn2_gu, w_ffn2_down, ln_g4, ln_b4)

    new_a_k, new_a_v, new_b_k, new_b_v = new
    return (xp.reshape(batch, seq, d), xs.reshape(dec_batch, dec_seq, d),
            new_a_k.reshape(batch, DEPTH, seq, A_KV_HEADS, HEAD_DIM),
            new_a_v.reshape(batch, DEPTH, seq, A_KV_HEADS, HEAD_DIM),
            new_b_k.reshape(batch, DEPTH, seq, B_HEADS, 2 * B_QK_DIM),
            new_b_v.reshape(batch, DEPTH, seq, B_HEADS, B_V_DIM))
```

```python
import functools
import math

import numpy as np
import jax
import jax.numpy as jnp
from jax import lax
from jax.experimental import pallas as pl
from jax.experimental.pallas import tpu as pltpu

F32 = jnp.float32
BF16 = jnp.bfloat16

D_MODEL = 1024
DEPTH = 2
GRID_W = 64
HEAD_DIM = 64
ROPE_THETA = 10000.0
A_HEADS = 4
A_KV_HEADS = 2
B_HEADS = 4
B_QK_DIM = 64
B_V_DIM = 128
C_GROUPS = 4
C_GROUP_DIM = 64
C_WIDTH = 256
FFN_DIM = 2816
N_MOD = 9
IN_WIDTH = 2304
ALPHA = (2.0 * DEPTH) ** 0.25
LN_EPS = 1e-5
RMS_EPS = 1e-6

LANES = 128
MOD_ROWS = 8
SEG_LEN = 4096
VMEM_CAPACITY = 64 << 20
VMEM_COMPILER_SHARE = 6 << 20

QA_OFF, KA_OFF, VA_OFF, QB_OFF, KB_OFF, VB_OFF = 0, 512, 640, 768, 1280, 1792
PROJ_W = 2304
CACHE_WIDTHS = (128, 128, 512, 512)

FFN_TM = 512
FFN_TF = 256
FFN_STAGES = 3
PROJ_TM = 512
OUT_TM = 1024
ATT_TQ = 256
ATT_UNITS = 4
ATT_SLOTS = 2
ATT_KC = 512
LOG2E = math.log2(math.e)


def _pallas_call(kernel, *, name, grid, sem, in_specs, out_specs, out_shape, scratch_shapes=(), temps=0):
    def run(*args):
        outs = out_shape if isinstance(out_shape, (list, tuple)) else [out_shape]
        specs = list(in_specs) + (list(out_specs) if isinstance(out_specs, (list, tuple)) else [out_specs])
        windows = sum(math.prod(d for d in s.block_shape if d is not None) * jnp.dtype(a.dtype).itemsize
                      for s, a in zip(specs, list(args) + list(outs)) if s.block_shape is not None)
        scratch = sum(math.prod(s.shape) * jnp.dtype(s.dtype).itemsize
                      for s in scratch_shapes if s.memory_space == pltpu.VMEM)
        need = 2 * windows + scratch + temps + VMEM_COMPILER_SHARE
        assert need <= VMEM_CAPACITY, (name, need)
        params = pltpu.CompilerParams(dimension_semantics=sem, vmem_limit_bytes=need)
        return pl.pallas_call(kernel, out_shape=out_shape, grid=grid, in_specs=in_specs, out_specs=out_specs,
                              scratch_shapes=list(scratch_shapes), compiler_params=params, name=name)(*args)
    return run


def _rope_tables(length):
    rows = length // GRID_W
    row = np.repeat(np.arange(rows), GRID_W).astype(np.float64)
    col = np.tile(np.arange(GRID_W), rows).astype(np.float64)
    half = HEAD_DIM // 2
    inv = 1.0 / (ROPE_THETA ** (np.arange(0, half, 2, dtype=np.float64) / half))
    ar = row[:, None] * inv
    ac = col[:, None] * inv
    cos = np.concatenate([np.cos(ar), np.cos(ar), np.cos(ac), np.cos(ac)], -1)
    sin = np.concatenate([-np.sin(ar), np.sin(ar), -np.sin(ac), np.sin(ac)], -1)
    return (jnp.asarray(np.tile(cos, (1, 2)), F32), jnp.asarray(np.tile(sin, (1, 2)), F32))


def _group_mean_matrix(width, group):
    m = np.kron(np.eye(width // group), np.full((group, group), 1.0 / group))
    return jnp.asarray(m, F32)


def _channel_dft():
    c = np.arange(C_GROUP_DIM)
    ang = 2.0 * np.pi * np.outer(c, c) / C_GROUP_DIM
    eye = np.eye(C_GROUPS)
    return jnp.asarray(np.concatenate([np.kron(eye, np.cos(ang)), np.kron(eye, np.sin(ang))], 0), F32)


def _position_dft_small(length):
    n = np.arange(length)
    ang = 2.0 * np.pi * ((np.outer(n, n)) % length) / length
    return jnp.asarray(np.concatenate([np.cos(ang), -np.sin(ang)], 1), F32)


def _two_stage_dft(n1):
    i = np.arange(n1)
    ang = 2.0 * np.pi * np.outer(i, i) / n1
    c, s = np.cos(ang), np.sin(ang)
    m1 = np.concatenate([c, -s], 0)
    m2 = np.block([[c, s], [-s, c]])
    tw = 2.0 * np.pi * np.outer(i, i).reshape(-1) / (n1 * n1)
    tc = np.repeat(np.cos(tw)[:, None], LANES, 1)
    ts = np.repeat(np.sin(tw)[:, None], LANES, 1)
    return tuple(jnp.asarray(a, F32) for a in (m1, m2, tc, ts))


def _layer_norm(y, g, b):
    mu = jnp.mean(y, axis=-1, keepdims=True)
    yc = y - mu
    var = jnp.mean(yc * yc, axis=-1, keepdims=True)
    return yc * lax.rsqrt(var + LN_EPS) * g + b


def _dot(a, b):
    return jnp.dot(a, b, preferred_element_type=F32)


def _dot_nt(a, b):
    return lax.dot_general(a, b, (((1,), (1,)), ((), ())), preferred_element_type=F32)


def _dot_split(a, b):
    hi = a.astype(BF16)
    lo = (a - hi.astype(F32)).astype(BF16)
    return _dot(hi, b) + _dot(lo, b)


def _softmax_pv_units(units, s_ref):
    rows = units[0][0].shape[0]
    chunk_lists = []
    for _, pieces in units:
        chunks, off = [], 0
        for k_ref, v_ref in pieces:
            for st in range(0, k_ref.shape[0], ATT_KC):
                sz = min(ATT_KC, k_ref.shape[0] - st)
                chunks.append((k_ref, v_ref, st, sz, off))
                off += sz
        chunk_lists.append(chunks)
    nchunk = len(chunk_lists[0])
    run_max = [None] * len(units)
    row_max = [None] * len(units)
    acc = [None] * len(units)

    def scores(u, ci):
        k_ref, _, st, sz, off = chunk_lists[u][ci]
        s = _dot_nt(units[u][0], k_ref[st:st + sz, :].astype(BF16))
        s_ref[u % ATT_SLOTS, :, off:off + sz] = s
        part = functools.reduce(jnp.maximum, [s[:, j:j + LANES] for j in range(0, sz, LANES)])
        run_max[u] = part if run_max[u] is None else jnp.maximum(run_max[u], part)

    def finish_max(u):
        m = jnp.max(run_max[u], axis=1, keepdims=True)
        row_max[u] = jnp.broadcast_to(m, (rows, LANES))

    def weighted(u, ci):
        _, v_ref, st, sz, off = chunk_lists[u][ci]
        e = jnp.concatenate([jnp.exp2(s_ref[u % ATT_SLOTS, :, off + j:off + j + LANES] - row_max[u])
                             for j in range(0, sz, LANES)], axis=1).astype(BF16)
        v1 = jnp.concatenate([v_ref[st:st + sz, :].astype(BF16), jnp.ones((sz, LANES), BF16)], axis=1)
        d = _dot(e, v1)
        acc[u] = d if acc[u] is None else acc[u] + d

    for ci in range(nchunk):
        scores(0, ci)
    finish_max(0)
    for u in range(1, len(units)):
        for ci in range(nchunk):
            weighted(u - 1, ci)
            scores(u, ci)
        finish_max(u)
    for ci in range(nchunk):
        weighted(len(units) - 1, ci)
    return [a[:, :LANES] / a[:, LANES:] for a in acc]


def _gqa_query(qpad):
    return jnp.concatenate([qpad[:, :LANES], qpad[:, LANES:]], axis=0)


def _gqa_output(o, group):
    tq = o.shape[0] // 2
    o0, o1 = o[:tq], o[tq:]
    low = lax.broadcasted_iota(jnp.int32, (tq, LANES), 1) < HEAD_DIM
    first = group == 0
    r0, r1 = pltpu.roll(o0, HEAD_DIM, 1), pltpu.roll(o1, HEAD_DIM, 1)
    return jnp.where(low, jnp.where(first, o0, r0), jnp.where(first, r1, o1))


def _diff_query(q):
    low = lax.broadcasted_iota(jnp.int32, q.shape, 1) < B_QK_DIM
    zero = jnp.zeros_like(q)
    return jnp.concatenate([jnp.where(low, q, zero), jnp.where(low, zero, q)], axis=0)


def _diff_output(o, lam, g_sub, lam_init):
    tq = o.shape[0] // 2
    w = o[:tq] - lam * o[tq:]
    y = w * lax.rsqrt(jnp.mean(w * w, axis=-1, keepdims=True) + RMS_EPS)
    return y * g_sub * (1.0 - lam_init)


def _lambda(lp):
    s1 = jnp.sum(lp[0:1] * lp[1:2], axis=1, keepdims=True)
    s2 = jnp.sum(lp[2:3] * lp[3:4], axis=1, keepdims=True)
    return jnp.exp(s1) - jnp.exp(s2)


def _mod_kernel(c_ref, w_ref, b_ref, o_ref):
    c = c_ref[...]
    s = (c * jax.nn.sigmoid(c)).astype(BF16)
    o_ref[...] = _dot(s, w_ref[...].astype(BF16)) + b_ref[...]


def _modulation(cvec, w_mod, b_mod):
    out = _pallas_call(
        _mod_kernel,
        out_shape=jax.ShapeDtypeStruct((DEPTH, N_MOD, MOD_ROWS, D_MODEL), F32),
        grid=(DEPTH, N_MOD),
        in_specs=[
            pl.BlockSpec((MOD_ROWS, D_MODEL), lambda l, n: (0, 0)),
            pl.BlockSpec((None, D_MODEL, D_MODEL), lambda l, n: (l, 0, n)),
            pl.BlockSpec((None, None, 1, D_MODEL), lambda l, n: (l, n, 0, 0)),
        ],
        out_specs=pl.BlockSpec((None, None, MOD_ROWS, D_MODEL), lambda l, n: (l, n, 0, 0)),
        sem=("parallel", "parallel"), temps=2 << 20,
        name="modulation",
    )(cvec, w_mod, b_mod.reshape(DEPTH, N_MOD, 1, D_MODEL))
    return out.reshape(DEPTH, N_MOD, MOD_ROWS, 1, D_MODEL)


def _mod_spec(layer, chunk, row0, tm):
    return pl.BlockSpec((None, None, None, 1, D_MODEL),
                        lambda i: (layer, chunk, row0 + (i * tm) // SEG_LEN, 0, 0))


def _ffn_kernel(layer, n_ctx, xp_ref, xs_ref, sh_ref, sc_ref, gt_ref, wgu_hbm, wd_hbm, lng_ref, lnb_ref,
                op_ref, os_ref, wg_ref, wu_ref, wd_ref, stage_gu, stage_d, sem, h_ref, acc_ref):
    nf = FFN_DIM // FFN_TF
    step = pl.program_id(0)
    first_step = step == 0
    is_ctx = step < n_ctx

    def chunk_copies(c, slot):
        col = c * FFN_TF if isinstance(c, int) else pl.multiple_of(c * FFN_TF, FFN_TF)
        return (
            pltpu.make_async_copy(wgu_hbm.at[layer, :, pl.ds(col, FFN_TF)], stage_gu.at[slot, 0], sem.at[slot, 0]),
            pltpu.make_async_copy(wgu_hbm.at[layer, :, pl.ds(FFN_DIM + col, FFN_TF)], stage_gu.at[slot, 1],
                                  sem.at[slot, 1]),
            pltpu.make_async_copy(wd_hbm.at[layer, pl.ds(col, FFN_TF), :], stage_d.at[slot], sem.at[slot, 2]),
        )

    def start_chunk(c, slot):
        for cp in chunk_copies(c, slot):
            cp.start()

    def modulated(x):
        return (x * (1.0 + sc_ref[...]) + sh_ref[...]).astype(BF16)

    def chunk(h, c):
        g = _dot(h, wg_ref[c])
        u = _dot(h, wu_ref[c])
        a = (g * jax.nn.sigmoid(g) * u).astype(BF16)
        return _dot(a, wd_ref[c])

    def finish(x, acc, o_ref):
        y = ALPHA * x + (0.5 * gt_ref[...]) * acc
        o_ref[...] = _layer_norm(y, lng_ref[...], lnb_ref[...])

    def later_step(x_ref, o_ref):
        x = x_ref[...]
        h = modulated(x)
        acc = chunk(h, 0)
        for c in range(1, nf):
            acc = acc + chunk(h, c)
        finish(x, acc, o_ref)

    @pl.when(first_step)
    def _():
        for c in range(FFN_STAGES - 1):
            start_chunk(c, c)
        x = xp_ref[...]
        h_ref[...] = modulated(x)
        acc_ref[...] = jnp.zeros_like(acc_ref)

        def body(c, carry):
            ahead = c + (FFN_STAGES - 1)

            @pl.when(ahead < nf)
            def _():
                start_chunk(ahead, lax.rem(ahead, FFN_STAGES))

            slot = lax.rem(c, FFN_STAGES)
            for cp in chunk_copies(c, slot):
                cp.wait()
            wg_ref[c] = stage_gu[slot, 0].astype(BF16)
            wu_ref[c] = stage_gu[slot, 1].astype(BF16)
            wd_ref[c] = stage_d[slot].astype(BF16)
            acc_ref[...] += chunk(h_ref[...], c)
            return carry

        lax.fori_loop(0, nf, body, 0)
        finish(x, acc_ref[...], op_ref)

    @pl.when(jnp.logical_and(jnp.logical_not(first_step), is_ctx))
    def _():
        later_step(xp_ref, op_ref)

    @pl.when(jnp.logical_not(is_ctx))
    def _():
        later_step(xs_ref, os_ref)


def _ffn(xp, xs, mod, layer, sub, w_gu, w_down, ln_g, ln_b):
    n_ctx = xp.shape[0] // FFN_TM
    n_lat = xs.shape[0] // FFN_TM
    nf = FFN_DIM // FFN_TF
    assert n_ctx >= 1 and n_lat >= 1 and nf >= FFN_STAGES

    def ctx_map(i):
        return (jnp.minimum(i, n_ctx - 1), 0)

    def lat_map(i):
        return (jnp.maximum(i - n_ctx, 0), 0)

    def mod_spec(chunk):
        def index(i):
            row = jnp.where(i < n_ctx, 0, 1 + (jnp.maximum(i - n_ctx, 0) * FFN_TM) // SEG_LEN)
            return (layer, chunk, row, 0, 0)
        return pl.BlockSpec((None, None, None, 1, D_MODEL), index)

    return _pallas_call(
        functools.partial(_ffn_kernel, layer, n_ctx),
        out_shape=[jax.ShapeDtypeStruct(xp.shape, F32), jax.ShapeDtypeStruct(xs.shape, F32)],
        grid=(n_ctx + n_lat,),
        in_specs=[
            pl.BlockSpec((FFN_TM, D_MODEL), ctx_map),
            pl.BlockSpec((FFN_TM, D_MODEL), lat_map),
            mod_spec(3 * sub), mod_spec(3 * sub + 1), mod_spec(3 * sub + 2),
            pl.BlockSpec(memory_space=pl.ANY),
            pl.BlockSpec(memory_space=pl.ANY),
            pl.BlockSpec((None, None, 1, D_MODEL), lambda i: (layer, sub, 0, 0)),
            pl.BlockSpec((None, None, 1, D_MODEL), lambda i: (layer, sub, 0, 0)),
        ],
        out_specs=[pl.BlockSpec((FFN_TM, D_MODEL), ctx_map), pl.BlockSpec((FFN_TM, D_MODEL), lat_map)],
        scratch_shapes=[
            pltpu.VMEM((nf, D_MODEL, FFN_TF), BF16),
            pltpu.VMEM((nf, D_MODEL, FFN_TF), BF16),
            pltpu.VMEM((nf, FFN_TF, D_MODEL), BF16),
            pltpu.VMEM((FFN_STAGES, 2, D_MODEL, FFN_TF), F32),
            pltpu.VMEM((FFN_STAGES, FFN_TF, D_MODEL), F32),
            pltpu.SemaphoreType.DMA((FFN_STAGES, 3)),
            pltpu.VMEM((FFN_TM, D_MODEL), BF16),
            pltpu.VMEM((FFN_TM, D_MODEL), F32),
        ],
        sem=("arbitrary",), temps=6 << 20,
        name="ffn",
    )(xp, xs, mod, mod, mod, w_gu, w_down, ln_g, ln_b)


def _rope(x, cos, sin):
    quarter = HEAD_DIM // 4
    lane = lax.broadcasted_iota(jnp.int32, x.shape, 1)
    first = (lane % (2 * quarter)) < quarter
    rot = jnp.where(first, pltpu.roll(x, LANES - quarter, 1), pltpu.roll(x, quarter, 1))
    return x * cos + rot * sin


def _proj_kernel(rope, n_earlier, x_ref, sh_ref, sc_ref, w_ref, gq_ref, gk_ref, pm_ref, *rest):
    if rope:
        cos_ref, sin_ref, p_ref, uc_ref, wb_ref = rest
    else:
        earlier, rest = rest[:n_earlier], rest[n_earlier:]
        p_ref, uc_ref, nak_ref, nav_ref, nbk_ref, nbv_ref, wb_ref = rest

    @pl.when(pl.program_id(0) == 0)
    def _():
        wb_ref[...] = w_ref[...].astype(BF16)

    h = (x_ref[...] * (1.0 + sc_ref[...]) + sh_ref[...]).astype(BF16)
    proj = _dot(h, wb_ref[...])
    tm = proj.shape[0]

    qa = proj[:, 0:256]
    qa = qa * lax.rsqrt(_dot_split(qa * qa, pm_ref[...]) + RMS_EPS) * gq_ref[...]
    ka = proj[:, 256:384]
    ka = ka * lax.rsqrt(_dot_split(ka * ka, pm_ref[0:LANES, 0:LANES]) + RMS_EPS) * gk_ref[...]
    va = proj[:, 384:512]
    qb = proj[:, 512:1024]
    kb = proj[:, 1024:1536]
    vb = proj[:, 1536:2048]
    uc_ref[0] = proj[:, 2048:2048 + LANES]
    uc_ref[1] = proj[:, 2048 + LANES:2304]

    if not rope:
        for n, (ref, val) in enumerate(((nak_ref, ka), (nav_ref, va), (nbk_ref, kb), (nbv_ref, vb))):
            if n_earlier:
                layers = [e[...] for e in earlier[n::len(CACHE_WIDTHS)]] + [val]
                for l, v in enumerate(layers):
                    ref[:, l] = v.reshape(ref.shape[0], ref.shape[2], ref.shape[3])
            else:
                ref[...] = val

    def slabs(a):
        return [a[:, s:s + LANES] for s in range(0, a.shape[1], LANES)]

    if rope:
        cos, sin = cos_ref[...], sin_ref[...]
        qa_s = [_rope(s, cos, sin) for s in slabs(qa)]
        ka_s = [_rope(ka, cos, sin)]
        qb_s = [_rope(s, cos, sin) for s in slabs(qb)]
        kb_s = [_rope(s, cos, sin) for s in slabs(kb)]
    else:
        qa_s, ka_s, qb_s, kb_s = slabs(qa), [ka], slabs(qb), slabs(kb)

    scale = LOG2E * HEAD_DIM ** -0.5
    low = lax.broadcasted_iota(jnp.int32, (tm, LANES), 1) < HEAD_DIM
    zero = jnp.zeros((tm, LANES), F32)
    g0, g1 = qa_s[0] * scale, qa_s[1] * scale
    qa_pad = [jnp.where(low, g0, zero), jnp.where(low, pltpu.roll(g0, HEAD_DIM, 1), zero),
              jnp.where(low, zero, pltpu.roll(g1, HEAD_DIM, 1)), jnp.where(low, zero, g1)]
    pieces = qa_pad + ka_s + [va] + [s * scale for s in qb_s] + kb_s + slabs(vb)
    for n, piece in enumerate(pieces):
        p_ref[:, n * LANES:(n + 1) * LANES] = piece.astype(BF16)


def _project(x, mod, layer, row0, w_in, gq, gk, pmean, rope_tabs, seq=None, earlier=()):
    t = x.shape[0]
    rope = rope_tabs is not None
    nt = t // PROJ_TM
    in_specs = [
        pl.BlockSpec((PROJ_TM, D_MODEL), lambda i: (i, 0)),
        _mod_spec(layer, 3, row0, PROJ_TM),
        _mod_spec(layer, 4, row0, PROJ_TM),
        pl.BlockSpec((None, D_MODEL, IN_WIDTH), lambda i: (layer, 0, 0)),
        pl.BlockSpec((None, 1, 256), lambda i: (layer, 0, 0)),
        pl.BlockSpec((None, 1, LANES), lambda i: (layer, 0, 0)),
        pl.BlockSpec((256, 256), lambda i: (0, 0)),
    ]
    args = [x, mod, mod, w_in, gq, gk, pmean]
    out_shape = [jax.ShapeDtypeStruct((t, PROJ_W), BF16), jax.ShapeDtypeStruct((2, t, LANES), F32)]
    out_specs = [pl.BlockSpec((PROJ_TM, PROJ_W), lambda i: (i, 0)),
                 pl.BlockSpec((2, PROJ_TM, LANES), lambda i: (0, i, 0))]
    if rope:
        per_seg = SEG_LEN // PROJ_TM
        in_specs += [pl.BlockSpec((PROJ_TM, LANES), lambda i: (i % per_seg, 0))] * 2
        args += list(rope_tabs)
    else:
        for e, width in zip(earlier, CACHE_WIDTHS * (len(earlier) // len(CACHE_WIDTHS))):
            in_specs.append(pl.BlockSpec((PROJ_TM, width), lambda i: (i, 0)))
            args.append(e)
        for width in CACHE_WIDTHS:
            if earlier:
                out_shape.append(jax.ShapeDtypeStruct((t // seq, DEPTH, seq, width), F32))
                out_specs.append(pl.BlockSpec((PROJ_TM // seq, DEPTH, seq, width), lambda i: (i, 0, 0, 0)))
            else:
                out_shape.append(jax.ShapeDtypeStruct((t, width), F32))
                out_specs.append(pl.BlockSpec((PROJ_TM, width), lambda i: (i, 0)))
    return _pallas_call(
        functools.partial(_proj_kernel, rope, len(earlier)),
        out_shape=out_shape,
        grid=(nt,),
        in_specs=in_specs,
        out_specs=out_specs,
        scratch_shapes=[pltpu.VMEM((D_MODEL, IN_WIDTH), BF16)],
        sem=("arbitrary",), temps=6 << 20,
        name="project_latent" if rope else "project_context",
    )(*args)


def _ctx_attn_kernel(lam_init, p_ref, lp_ref, gs_ref, oa_ref, ob_ref, s_ref):
    ka = p_ref.at[:, KA_OFF:KA_OFF + LANES]
    va = p_ref.at[:, VA_OFF:VA_OFF + LANES]
    units = []
    for g in range(A_KV_HEADS):
        units.append((_gqa_query(p_ref[:, QA_OFF + 256 * g:QA_OFF + 256 * (g + 1)]), [(ka, va)]))
    for h in range(B_HEADS):
        k = p_ref.at[:, KB_OFF + h * LANES:KB_OFF + (h + 1) * LANES]
        v = p_ref.at[:, VB_OFF + h * LANES:VB_OFF + (h + 1) * LANES]
        units.append((_diff_query(p_ref[:, QB_OFF + h * LANES:QB_OFF + (h + 1) * LANES]), [(k, v)]))
    outs = _softmax_pv_units(units, s_ref)
    lam = _lambda(lp_ref[...]) + lam_init
    for g in range(A_KV_HEADS):
        oa_ref[:, g * LANES:(g + 1) * LANES] = _gqa_output(outs[g], g).astype(BF16)
    for h in range(B_HEADS):
        o = _diff_output(outs[A_KV_HEADS + h], lam, gs_ref[...], lam_init)
        ob_ref[:, h * LANES:(h + 1) * LANES] = o.astype(BF16)


def _context_attention(p, layer, lam_params, g_subln, lam_init, seq):
    t = p.shape[0]
    return _pallas_call(
        functools.partial(_ctx_attn_kernel, lam_init),
        out_shape=[jax.ShapeDtypeStruct((t, 256), BF16), jax.ShapeDtypeStruct((t, 512), BF16)],
        grid=(t // seq,),
        in_specs=[
            pl.BlockSpec((seq, PROJ_W), lambda i: (i, 0)),
            pl.BlockSpec((None, 4, B_QK_DIM), lambda i: (layer, 0, 0)),
            pl.BlockSpec((None, 1, B_V_DIM), lambda i: (layer, 0, 0)),
        ],
        out_specs=[pl.BlockSpec((seq, 256), lambda i: (i, 0)), pl.BlockSpec((seq, 512), lambda i: (i, 0))],
        scratch_shapes=[pltpu.VMEM((ATT_SLOTS, 2 * seq, seq), F32)],
        sem=("parallel",), temps=2 << 20,
        name="context_attention",
    )(p, lam_params, g_subln)


def _lat_gqa_kernel(q_ref, kn_ref, vn_ref, kc_ref, vc_ref, o_ref, s_ref):
    pieces = [(kc_ref, vc_ref), (kn_ref, vn_ref)]
    units = [(_gqa_query(q_ref[u * ATT_TQ:(u + 1) * ATT_TQ, :]), pieces) for u in range(ATT_UNITS)]
    outs = _softmax_pv_units(units, s_ref)
    for u in range(ATT_UNITS):
        o_ref[u * ATT_TQ:(u + 1) * ATT_TQ, :] = _gqa_output(outs[u], pl.program_id(1)).astype(BF16)


def _latent_gqa(p, cache_k, cache_v, layer, batch, length):
    tile = ATT_TQ * ATT_UNITS
    nq = length // tile
    past = cache_k.shape[2]
    return _pallas_call(
        _lat_gqa_kernel,
        out_shape=jax.ShapeDtypeStruct((batch * length, 256), BF16),
        grid=(batch, A_KV_HEADS, nq),
        in_specs=[
            pl.BlockSpec((tile, 256), lambda b, g, i: (b * nq + i, QA_OFF // 256 + g)),
            pl.BlockSpec((length, LANES), lambda b, g, i: (b, KA_OFF // LANES)),
            pl.BlockSpec((length, LANES), lambda b, g, i: (b, VA_OFF // LANES)),
            pl.BlockSpec((None, None, past, LANES), lambda b, g, i: (b, layer, 0, 0)),
            pl.BlockSpec((None, None, past, LANES), lambda b, g, i: (b, layer, 0, 0)),
        ],
        out_specs=pl.BlockSpec((tile, LANES), lambda b, g, i: (b * nq + i, g)),
        scratch_shapes=[pltpu.VMEM((ATT_SLOTS, 2 * ATT_TQ, past + length), F32)],
        sem=("parallel", "parallel", "parallel"), temps=4 << 20,
        name="latent_gqa",
    )(p, p, p, cache_k, cache_v)


def _lat_diff_kernel(lam_init, q_ref, kn_ref, vn_ref, kc_ref, vc_ref, lp_ref, gs_ref, o_ref, s_ref):
    pieces = [(kc_ref, vc_ref), (kn_ref, vn_ref)]
    units = [(_diff_query(q_ref[u * ATT_TQ:(u + 1) * ATT_TQ, :]), pieces) for u in range(ATT_UNITS)]
    outs = _softmax_pv_units(units, s_ref)
    lam = _lambda(lp_ref[...]) + lam_init
    for u in range(ATT_UNITS):
        o = _diff_output(outs[u], lam, gs_ref[...], lam_init)
        o_ref[u * ATT_TQ:(u + 1) * ATT_TQ, :] = o.astype(BF16)


def _latent_diff(p, cache_k, cache_v, layer, lam_params, g_subln, lam_init, batch, length):
    tile = ATT_TQ * ATT_UNITS
    nq = length // tile
    past = cache_k.shape[2]
    return _pallas_call(
        functools.partial(_lat_diff_kernel, lam_init),
        out_shape=jax.ShapeDtypeStruct((batch * length, 512), BF16),
        grid=(batch, B_HEADS, nq),
        in_specs=[
            pl.BlockSpec((tile, LANES), lambda b, h, i: (b * nq + i, QB_OFF // LANES + h)),
            pl.BlockSpec((length, LANES), lambda b, h, i: (b, KB_OFF // LANES + h)),
            pl.BlockSpec((length, LANES), lambda b, h, i: (b, VB_OFF // LANES + h)),
            pl.BlockSpec((None, None, past, LANES), lambda b, h, i: (b, layer, 0, h)),
            pl.BlockSpec((None, None, past, LANES), lambda b, h, i: (b, layer, 0, h)),
            pl.BlockSpec((None, 4, B_QK_DIM), lambda b, h, i: (layer, 0, 0)),
            pl.BlockSpec((None, 1, B_V_DIM), lambda b, h, i: (layer, 0, 0)),
        ],
        out_specs=pl.BlockSpec((tile, LANES), lambda b, h, i: (b * nq + i, h)),
        scratch_shapes=[pltpu.VMEM((ATT_SLOTS, 2 * ATT_TQ, past + length), F32)],
        sem=("parallel", "parallel", "parallel"), temps=4 << 20,
        name="latent_diff",
    )(p, p, p, cache_k, cache_v, lam_params, g_subln)


def _ctx_fourier_kernel(norm, u_ref, bcs_ref, cs_ref, o_ref):
    u = jnp.concatenate([u_ref[0], u_ref[1]], axis=1).astype(BF16)
    bcs = bcs_ref[...]
    rhs = jnp.concatenate([_dot(u, bcs[:C_WIDTH]), _dot(u, bcs[C_WIDTH:])], axis=0).astype(BF16)
    f = _dot(cs_ref[...], rhs) * norm
    o_ref[0] = f[:, :LANES]
    o_ref[1] = f[:, LANES:]


def _context_fourier(uc, bcs, cs, seq):
    t = uc.shape[1]
    norm = 1.0 / math.sqrt(seq * C_GROUP_DIM)
    return _pallas_call(
        functools.partial(_ctx_fourier_kernel, norm),
        out_shape=jax.ShapeDtypeStruct((2, t, LANES), F32),
        grid=(t // seq,),
        in_specs=[
            pl.BlockSpec((2, seq, LANES), lambda i: (0, i, 0)),
            pl.BlockSpec((2 * C_WIDTH, C_WIDTH), lambda i: (0, 0)),
            pl.BlockSpec((seq, 2 * seq), lambda i: (0, 0)),
        ],
        out_specs=pl.BlockSpec((2, seq, LANES), lambda i: (0, i, 0)),
        sem=("parallel",), temps=1 << 20,
        name="context_fourier",
    )(uc, bcs, cs)


def _lat_fourier_kernel(norm, u_ref, m1_ref, m2_ref, bcs_ref, tc_ref, ts_ref, o_ref, z_ref):
    n1 = GRID_W

    def strided(ref, plane, start):
        rows = pl.ds(start, n1, stride=n1)
        return jnp.concatenate([ref[(*plane, 0, rows, slice(None))], ref[(*plane, 1, rows, slice(None))]], axis=1)

    m1 = m1_ref[...]
    for b in range(n1):
        ub = strided(u_ref, (), b).astype(BF16)
        y = _dot(m1, ub)
        yr, yi = y[:n1], y[n1:]
        tc = jnp.concatenate([tc_ref[b * n1:(b + 1) * n1, :]] * 2, axis=1)
        ts = jnp.concatenate([ts_ref[b * n1:(b + 1) * n1, :]] * 2, axis=1)
        zr = yr * tc + yi * ts
        zi = yi * tc - yr * ts
        for half in range(2):
            z_ref[0, half, b * n1:(b + 1) * n1, :] = zr[:, half * LANES:(half + 1) * LANES]
            z_ref[1, half, b * n1:(b + 1) * n1, :] = zi[:, half * LANES:(half + 1) * LANES]
    m2 = m2_ref[...]
    bcs = bcs_ref[...]
    for q in range(n1):
        z = jnp.concatenate([strided(z_ref, (0,), q), strided(z_ref, (1,), q)], axis=0).astype(BF16)
        g = _dot(m2, z)
        gcat = jnp.concatenate([g[:n1], g[n1:]], axis=1).astype(BF16)
        f = _dot(gcat, bcs) * norm
        o_ref[0, pl.ds(q, n1, stride=n1), :] = f[:, :LANES]
        o_ref[1, pl.ds(q, n1, stride=n1), :] = f[:, LANES:]


def _latent_fourier(uc, m1, m2, bcs, tc, ts, length):
    t = uc.shape[1]
    norm = 1.0 / math.sqrt(length * C_GROUP_DIM)
    return _pallas_call(
        functools.partial(_lat_fourier_kernel, norm),
        out_shape=jax.ShapeDtypeStruct((2, t, LANES), F32),
        grid=(t // length,),
        in_specs=[
            pl.BlockSpec((2, length, LANES), lambda i: (0, i, 0)),
            pl.BlockSpec((2 * GRID_W, GRID_W), lambda i: (0, 0)),
            pl.BlockSpec((2 * GRID_W, 2 * GRID_W), lambda i: (0, 0)),
            pl.BlockSpec((2 * C_WIDTH, C_WIDTH), lambda i: (0, 0)),
            pl.BlockSpec((length, LANES), lambda i: (0, 0)),
            pl.BlockSpec((length, LANES), lambda i: (0, 0)),
        ],
        out_specs=pl.BlockSpec((2, length, LANES), lambda i: (0, i, 0)),
        scratch_shapes=[pltpu.VMEM((2, 2, length, LANES), F32)],
        sem=("parallel",), temps=2 << 20,
        name="latent_fourier",
    )(uc, m1, m2, bcs, tc, ts)


def _out_kernel(x_ref, gt_ref, oa_ref, ob_ref, f_ref, wf_ref, wo_ref, lng_ref, lnb_ref, o_ref, wob_ref):
    @pl.when(pl.program_id(0) == 0)
    def _():
        wob_ref[...] = wo_ref[...].astype(BF16)

    wf = wf_ref[...].astype(BF16)
    half = OUT_TM // 2
    for r in range(0, OUT_TM, half):
        rows = slice(r, r + half)
        f = jnp.concatenate([f_ref[0, rows, :], f_ref[1, rows, :]], axis=1).astype(BF16)
        oc = _dot(f, wf).astype(BF16)
        cat = jnp.concatenate([oa_ref[rows, :], ob_ref[rows, :], oc], axis=1)
        mo = _dot(cat, wob_ref[...])
        y = ALPHA * x_ref[rows, :] + gt_ref[...] * mo
        o_ref[rows, :] = _layer_norm(y, lng_ref[...], lnb_ref[...])


def _out_project(x, mod, layer, row0, oa, ob, f, w_fourier, w_out, ln_g, ln_b):
    t = x.shape[0]
    return _pallas_call(
        _out_kernel,
        out_shape=jax.ShapeDtypeStruct((t, D_MODEL), F32),
        grid=(t // OUT_TM,),
        in_specs=[
            pl.BlockSpec((OUT_TM, D_MODEL), lambda i: (i, 0)),
            _mod_spec(layer, 5, row0, OUT_TM),
            pl.BlockSpec((OUT_TM, 256), lambda i: (i, 0)),
            pl.BlockSpec((OUT_TM, 512), lambda i: (i, 0)),
            pl.BlockSpec((2, OUT_TM, LANES), lambda i: (0, i, 0)),
            pl.BlockSpec((None, C_WIDTH, C_WIDTH), lambda i: (layer, 0, 0)),
            pl.BlockSpec((None, D_MODEL, D_MODEL), lambda i: (layer, 0, 0)),
            pl.BlockSpec((None, None, 1, D_MODEL), lambda i: (layer, 1, 0, 0)),
            pl.BlockSpec((None, None, 1, D_MODEL), lambda i: (layer, 1, 0, 0)),
        ],
        out_specs=pl.BlockSpec((OUT_TM, D_MODEL), lambda i: (i, 0)),
        scratch_shapes=[pltpu.VMEM((D_MODEL, D_MODEL), BF16)],
        sem=("arbitrary",), temps=8 << 20,
        name="out_project",
    )(x, mod, oa, ob, f, w_fourier, w_out, ln_g, ln_b)


def kernel(x_prompt, x_sample, cache_a_k, cache_a_v, cache_b_k, cache_b_v, c, c_ctx, w_mod, b_mod, w_in, g_qa, g_ka, lam_q1, lam_k1, lam_q2, lam_k2, g_subln, w_fourier, w_out, w_ffn1_gu, w_ffn1_down, w_ffn2_gu, w_ffn2_down, ln_g, ln_b):
    batch, seq, d = x_prompt.shape
    dec_batch, dec_seq, _ = x_sample.shape
    past = cache_a_k.shape[2]
    assert d == D_MODEL and dec_seq == SEG_LEN and batch * seq == SEG_LEN and dec_batch == 2

    cvec = jnp.concatenate([c_ctx[None, :], c, jnp.zeros((MOD_ROWS - 1 - dec_batch, d), F32)], axis=0)
    mod = _modulation(cvec, w_mod, b_mod)

    gq = jnp.tile(g_qa, (1, A_HEADS)).reshape(DEPTH, 1, A_HEADS * HEAD_DIM)
    gk = jnp.tile(g_ka, (1, A_KV_HEADS)).reshape(DEPTH, 1, A_KV_HEADS * HEAD_DIM)
    gs = g_subln.reshape(DEPTH, 1, B_V_DIM)
    lam_params = jnp.stack([lam_q1, lam_k1, lam_q2, lam_k2], axis=1)
    ln_g4 = ln_g.reshape(DEPTH, 3, 1, d)
    ln_b4 = ln_b.reshape(DEPTH, 3, 1, d)
    pmean = _group_mean_matrix(256, HEAD_DIM).astype(BF16)
    rope_tabs = _rope_tables(dec_seq)
    bcs = _channel_dft().astype(BF16)
    cs_ctx = _position_dft_small(seq).astype(BF16)
    m1, m2, tw_cos, tw_sin = _two_stage_dft(GRID_W)
    m1, m2 = m1.astype(BF16), m2.astype(BF16)
    ck_a = cache_a_k.reshape(dec_batch, DEPTH, past, A_KV_HEADS * HEAD_DIM)
    cv_a = cache_a_v.reshape(dec_batch, DEPTH, past, A_KV_HEADS * HEAD_DIM)
    ck_b = cache_b_k.reshape(dec_batch, DEPTH, past, B_HEADS * 2 * B_QK_DIM)
    cv_b = cache_b_v.reshape(dec_batch, DEPTH, past, B_HEADS * B_V_DIM)

    xp = x_prompt.reshape(batch * seq, d)
    xs = x_sample.reshape(dec_batch * dec_seq, d)
    new = []
    for layer in range(DEPTH):
        lam_init = 0.8 - 0.6 * math.exp(-0.3 * layer)
        xp, xs = _ffn(xp, xs, mod, layer, 0, w_ffn1_gu, w_ffn1_down, ln_g4, ln_b4)
        earlier = tuple(new) if layer == DEPTH - 1 else ()
        pp, ucp, *caches = _project(xp, mod, layer, 0, w_in, gq, gk, pmean, None, seq, earlier)
        new = caches if layer == DEPTH - 1 else new + caches
        oa, ob = _context_attention(pp, layer, lam_params, gs, lam_init, seq)
        fo = _context_fourier(ucp, bcs, cs_ctx, seq)
        xp = _out_project(xp, mod, layer, 0, oa, ob, fo, w_fourier, w_out, ln_g4, ln_b4)
        ps, ucs = _project(xs, mod, layer, 1, w_in, gq, gk, pmean, rope_tabs)
        oa = _latent_gqa(ps, ck_a, cv_a, layer, dec_batch, dec_seq)
        ob = _latent_diff(ps, ck_b, cv_b, layer, lam_params, gs, lam_init, dec_batch, dec_seq)
        fo = _latent_fourier(ucs, m1, m2, bcs, tw_cos, tw_sin, dec_seq)
        xs = _out_project(xs, mod, layer, 1, oa, ob, fo, w_fourier, w_out, ln_g4, ln_b4)
        xp, xs = _ffn(xp, xs, mod, layer, 2, w_ffn2_gu, w_ffn2_down, ln_g4, ln_b4)

    new_a_k, new_a_v, new_b_k, new_b_v = new
    return (xp.reshape(batch, seq, d), xs.reshape(dec_batch, dec_seq, d),
            new_a_k.reshape(batch, DEPTH, seq, A_KV_HEADS, HEAD_DIM),
            new_a_v.reshape(batch, DEPTH, seq, A_KV_HEADS, HEAD_DIM),
            new_b_k.reshape(batch, DEPTH, seq, B_HEADS, 2 * B_QK_DIM),
            new_b_v.reshape(batch, DEPTH, seq, B_HEADS, B_V_DIM))
```

```python
import functools
import math

import numpy as np
import jax
import jax.numpy as jnp
from jax import lax
from jax.experimental import pallas as pl
from jax.experimental.pallas import tpu as pltpu

F32 = jnp.float32
BF16 = jnp.bfloat16

D_MODEL = 1024
DEPTH = 2
GRID_W = 64
HEAD_DIM = 64
ROPE_THETA = 10000.0
A_HEADS = 4
A_KV_HEADS = 2
B_HEADS = 4
B_QK_DIM = 64
B_V_DIM = 128
C_GROUPS = 4
C_GROUP_DIM = 64
C_WIDTH = 256
FFN_DIM = 2816
N_MOD = 9
IN_WIDTH = 2304
ALPHA = (2.0 * DEPTH) ** 0.25
LN_EPS = 1e-5
RMS_EPS = 1e-6

LANES = 128
MOD_ROWS = 8
SEG_LEN = 4096
VMEM_LIMIT = 52 << 20

QA_OFF, KA_OFF, VA_OFF, QB_OFF, KB_OFF, VB_OFF = 0, 512, 640, 768, 1280, 1792
PROJ_W = 2304
CACHE_WIDTHS = (128, 128, 512, 512)

FFN_TM = 512
FFN_TF = 256
FFN_STAGES = 3
PROJ_TM = 512
OUT_TM = 1024
ATT_TQ = 256
ATT_UNITS = 8
ATT_SLOTS = 2
ATT_KC = 512
LOG2E = math.log2(math.e)


def _cparams(*sem):
    return pltpu.CompilerParams(dimension_semantics=sem, vmem_limit_bytes=VMEM_LIMIT)


def _rope_tables(length):
    rows = length // GRID_W
    row = np.repeat(np.arange(rows), GRID_W).astype(np.float64)
    col = np.tile(np.arange(GRID_W), rows).astype(np.float64)
    half = HEAD_DIM // 2
    inv = 1.0 / (ROPE_THETA ** (np.arange(0, half, 2, dtype=np.float64) / half))
    ar = row[:, None] * inv
    ac = col[:, None] * inv
    cos = np.concatenate([np.cos(ar), np.cos(ar), np.cos(ac), np.cos(ac)], -1)
    sin = np.concatenate([-np.sin(ar), np.sin(ar), -np.sin(ac), np.sin(ac)], -1)
    return (jnp.asarray(np.tile(cos, (1, 2)), F32), jnp.asarray(np.tile(sin, (1, 2)), F32))


def _group_mean_matrix(width, group):
    m = np.kron(np.eye(width // group), np.full((group, group), 1.0 / group))
    return jnp.asarray(m, F32)


def _channel_dft():
    c = np.arange(C_GROUP_DIM)
    ang = 2.0 * np.pi * np.outer(c, c) / C_GROUP_DIM
    eye = np.eye(C_GROUPS)
    return jnp.asarray(np.concatenate([np.kron(eye, np.cos(ang)), np.kron(eye, np.sin(ang))], 0), F32)


def _position_dft_small(length):
    n = np.arange(length)
    ang = 2.0 * np.pi * ((np.outer(n, n)) % length) / length
    return jnp.asarray(np.concatenate([np.cos(ang), -np.sin(ang)], 1), F32)


def _two_stage_dft(n1):
    i = np.arange(n1)
    ang = 2.0 * np.pi * np.outer(i, i) / n1
    c, s = np.cos(ang), np.sin(ang)
    m1 = np.concatenate([c, -s], 0)
    m2 = np.block([[c, s], [-s, c]])
    tw = 2.0 * np.pi * np.outer(i, i).reshape(-1) / (n1 * n1)
    tc = np.repeat(np.cos(tw)[:, None], LANES, 1)
    ts = np.repeat(np.sin(tw)[:, None], LANES, 1)
    return tuple(jnp.asarray(a, F32) for a in (m1, m2, tc, ts))


def _layer_norm(y, g, b):
    mu = jnp.mean(y, axis=-1, keepdims=True)
    yc = y - mu
    var = jnp.mean(yc * yc, axis=-1, keepdims=True)
    return yc * lax.rsqrt(var + LN_EPS) * g + b


def _dot(a, b):
    return jnp.dot(a, b, preferred_element_type=F32)


def _dot_nt(a, b):
    return lax.dot_general(a, b, (((1,), (1,)), ((), ())), preferred_element_type=F32)


def _dot_split(a, b):
    hi = a.astype(BF16)
    lo = (a - hi.astype(F32)).astype(BF16)
    return _dot(hi, b) + _dot(lo, b)


def _softmax_pv_units(units, s_ref):
    rows = units[0][0].shape[0]
    chunk_lists = []
    for _, pieces in units:
        chunks, off = [], 0
        for k_ref, v_ref in pieces:
            for st in range(0, k_ref.shape[0], ATT_KC):
                sz = min(ATT_KC, k_ref.shape[0] - st)
                chunks.append((k_ref, v_ref, st, sz, off))
                off += sz
        chunk_lists.append(chunks)
    nchunk = len(chunk_lists[0])
    run_max = [None] * len(units)
    row_max = [None] * len(units)
    acc = [None] * len(units)

    def scores(u, ci):
        k_ref, _, st, sz, off = chunk_lists[u][ci]
        s = _dot_nt(units[u][0], k_ref[st:st + sz, :].astype(BF16))
        s_ref[u % ATT_SLOTS, :, off:off + sz] = s
        part = functools.reduce(jnp.maximum, [s[:, j:j + LANES] for j in range(0, sz, LANES)])
        run_max[u] = part if run_max[u] is None else jnp.maximum(run_max[u], part)

    def finish_max(u):
        m = jnp.max(run_max[u], axis=1, keepdims=True)
        row_max[u] = jnp.broadcast_to(m, (rows, LANES))

    def weighted(u, ci):
        _, v_ref, st, sz, off = chunk_lists[u][ci]
        e = jnp.concatenate([jnp.exp2(s_ref[u % ATT_SLOTS, :, off + j:off + j + LANES] - row_max[u])
                             for j in range(0, sz, LANES)], axis=1).astype(BF16)
        v1 = jnp.concatenate([v_ref[st:st + sz, :].astype(BF16), jnp.ones((sz, LANES), BF16)], axis=1)
        d = _dot(e, v1)
        acc[u] = d if acc[u] is None else acc[u] + d

    for ci in range(nchunk):
        scores(0, ci)
    finish_max(0)
    for u in range(1, len(units)):
        for ci in range(nchunk):
            weighted(u - 1, ci)
            scores(u, ci)
        finish_max(u)
    for ci in range(nchunk):
        weighted(len(units) - 1, ci)
    return [a[:, :LANES] / a[:, LANES:] for a in acc]


def _gqa_query(qpad):
    return jnp.concatenate([qpad[:, :LANES], qpad[:, LANES:]], axis=0)


def _gqa_output(o, group):
    tq = o.shape[0] // 2
    o0, o1 = o[:tq], o[tq:]
    low = lax.broadcasted_iota(jnp.int32, (tq, LANES), 1) < HEAD_DIM
    first = group == 0
    r0, r1 = pltpu.roll(o0, HEAD_DIM, 1), pltpu.roll(o1, HEAD_DIM, 1)
    return jnp.where(low, jnp.where(first, o0, r0), jnp.where(first, r1, o1))


def _diff_query(q):
    low = lax.broadcasted_iota(jnp.int32, q.shape, 1) < B_QK_DIM
    zero = jnp.zeros_like(q)
    return jnp.concatenate([jnp.where(low, q, zero), jnp.where(low, zero, q)], axis=0)


def _diff_output(o, lam, g_sub, lam_init):
    tq = o.shape[0] // 2
    w = o[:tq] - lam * o[tq:]
    y = w * lax.rsqrt(jnp.mean(w * w, axis=-1, keepdims=True) + RMS_EPS)
    return y * g_sub * (1.0 - lam_init)


def _lambda(lp):
    s1 = jnp.sum(lp[0:1] * lp[1:2], axis=1, keepdims=True)
    s2 = jnp.sum(lp[2:3] * lp[3:4], axis=1, keepdims=True)
    return jnp.exp(s1) - jnp.exp(s2)


def _mod_kernel(c_ref, w_ref, b_ref, o_ref):
    c = c_ref[...]
    s = (c * jax.nn.sigmoid(c)).astype(BF16)
    o_ref[...] = _dot(s, w_ref[...].astype(BF16)) + b_ref[...]


def _modulation(cvec, w_mod, b_mod):
    out = pl.pallas_call(
        _mod_kernel,
        out_shape=jax.ShapeDtypeStruct((DEPTH, N_MOD, MOD_ROWS, D_MODEL), F32),
        grid=(DEPTH, N_MOD),
        in_specs=[
            pl.BlockSpec((MOD_ROWS, D_MODEL), lambda l, n: (0, 0)),
            pl.BlockSpec((None, D_MODEL, D_MODEL), lambda l, n: (l, 0, n)),
            pl.BlockSpec((None, None, 1, D_MODEL), lambda l, n: (l, n, 0, 0)),
        ],
        out_specs=pl.BlockSpec((None, None, MOD_ROWS, D_MODEL), lambda l, n: (l, n, 0, 0)),
        compiler_params=_cparams("parallel", "parallel"),
        name="modulation",
    )(cvec, w_mod, b_mod.reshape(DEPTH, N_MOD, 1, D_MODEL))
    return out.reshape(DEPTH, N_MOD, MOD_ROWS, 1, D_MODEL)


def _mod_spec(layer, chunk, row0, tm):
    return pl.BlockSpec((None, None, None, 1, D_MODEL),
                        lambda i: (layer, chunk, row0 + (i * tm) // SEG_LEN, 0, 0))


def _ffn_kernel(layer, n_ctx, xp_ref, xs_ref, sh_ref, sc_ref, gt_ref, wgu_hbm, wd_hbm, lng_ref, lnb_ref,
                op_ref, os_ref, wg_ref, wu_ref, wd_ref, stage_gu, stage_d, sem, h_ref, acc_ref):
    nf = FFN_DIM // FFN_TF
    step = pl.program_id(0)
    first_step = step == 0
    is_ctx = step < n_ctx

    def chunk_copies(c, slot):
        col = c * FFN_TF if isinstance(c, int) else pl.multiple_of(c * FFN_TF, FFN_TF)
        return (
            pltpu.make_async_copy(wgu_hbm.at[layer, :, pl.ds(col, FFN_TF)], stage_gu.at[slot, 0], sem.at[slot, 0]),
            pltpu.make_async_copy(wgu_hbm.at[layer, :, pl.ds(FFN_DIM + col, FFN_TF)], stage_gu.at[slot, 1],
                                  sem.at[slot, 1]),
            pltpu.make_async_copy(wd_hbm.at[layer, pl.ds(col, FFN_TF), :], stage_d.at[slot], sem.at[slot, 2]),
        )

    def start_chunk(c, slot):
        for cp in chunk_copies(c, slot):
            cp.start()

    def modulated(x):
        return (x * (1.0 + sc_ref[...]) + sh_ref[...]).astype(BF16)

    def chunk(h, c):
        g = _dot(h, wg_ref[c])
        u = _dot(h, wu_ref[c])
        a = (g * jax.nn.sigmoid(g) * u).astype(BF16)
        return _dot(a, wd_ref[c])

    def finish(x, acc, o_ref):
        y = ALPHA * x + (0.5 * gt_ref[...]) * acc
        o_ref[...] = _layer_norm(y, lng_ref[...], lnb_ref[...])

    def later_step(x_ref, o_ref):
        x = x_ref[...]
        h = modulated(x)
        acc = chunk(h, 0)
        for c in range(1, nf):
            acc = acc + chunk(h, c)
        finish(x, acc, o_ref)

    @pl.when(first_step)
    def _():
        for c in range(FFN_STAGES - 1):
            start_chunk(c, c)
        x = xp_ref[...]
        h_ref[...] = modulated(x)
        acc_ref[...] = jnp.zeros_like(acc_ref)

        def body(c, carry):
            ahead = c + (FFN_STAGES - 1)

            @pl.when(ahead < nf)
            def _():
                start_chunk(ahead, lax.rem(ahead, FFN_STAGES))

            slot = lax.rem(c, FFN_STAGES)
            for cp in chunk_copies(c, slot):
                cp.wait()
            wg_ref[c] = stage_gu[slot, 0].astype(BF16)
            wu_ref[c] = stage_gu[slot, 1].astype(BF16)
            wd_ref[c] = stage_d[slot].astype(BF16)
            acc_ref[...] += chunk(h_ref[...], c)
            return carry

        lax.fori_loop(0, nf, body, 0)
        finish(x, acc_ref[...], op_ref)

    @pl.when(jnp.logical_and(jnp.logical_not(first_step), is_ctx))
    def _():
        later_step(xp_ref, op_ref)

    @pl.when(jnp.logical_not(is_ctx))
    def _():
        later_step(xs_ref, os_ref)


def _ffn(xp, xs, mod, layer, sub, w_gu, w_down, ln_g, ln_b):
    n_ctx = xp.shape[0] // FFN_TM
    n_lat = xs.shape[0] // FFN_TM
    nf = FFN_DIM // FFN_TF
    assert n_ctx >= 1 and n_lat >= 1 and nf >= FFN_STAGES

    def ctx_map(i):
        return (jnp.minimum(i, n_ctx - 1), 0)

    def lat_map(i):
        return (jnp.maximum(i - n_ctx, 0), 0)

    def mod_spec(chunk):
        def index(i):
            row = jnp.where(i < n_ctx, 0, 1 + (jnp.maximum(i - n_ctx, 0) * FFN_TM) // SEG_LEN)
            return (layer, chunk, row, 0, 0)
        return pl.BlockSpec((None, None, None, 1, D_MODEL), index)

    return pl.pallas_call(
        functools.partial(_ffn_kernel, layer, n_ctx),
        out_shape=[jax.ShapeDtypeStruct(xp.shape, F32), jax.ShapeDtypeStruct(xs.shape, F32)],
        grid=(n_ctx + n_lat,),
        in_specs=[
            pl.BlockSpec((FFN_TM, D_MODEL), ctx_map),
            pl.BlockSpec((FFN_TM, D_MODEL), lat_map),
            mod_spec(3 * sub), mod_spec(3 * sub + 1), mod_spec(3 * sub + 2),
            pl.BlockSpec(memory_space=pl.ANY),
            pl.BlockSpec(memory_space=pl.ANY),
            pl.BlockSpec((None, None, 1, D_MODEL), lambda i: (layer, sub, 0, 0)),
            pl.BlockSpec((None, None, 1, D_MODEL), lambda i: (layer, sub, 0, 0)),
        ],
        out_specs=[pl.BlockSpec((FFN_TM, D_MODEL), ctx_map), pl.BlockSpec((FFN_TM, D_MODEL), lat_map)],
        scratch_shapes=[
            pltpu.VMEM((nf, D_MODEL, FFN_TF), BF16),
            pltpu.VMEM((nf, D_MODEL, FFN_TF), BF16),
            pltpu.VMEM((nf, FFN_TF, D_MODEL), BF16),
            pltpu.VMEM((FFN_STAGES, 2, D_MODEL, FFN_TF), F32),
            pltpu.VMEM((FFN_STAGES, FFN_TF, D_MODEL), F32),
            pltpu.SemaphoreType.DMA((FFN_STAGES, 3)),
            pltpu.VMEM((FFN_TM, D_MODEL), BF16),
            pltpu.VMEM((FFN_TM, D_MODEL), F32),
        ],
        compiler_params=_cparams("arbitrary"),
        name="ffn",
    )(xp, xs, mod, mod, mod, w_gu, w_down, ln_g, ln_b)


def _rope(x, cos, sin):
    quarter = HEAD_DIM // 4
    lane = lax.broadcasted_iota(jnp.int32, x.shape, 1)
    first = (lane % (2 * quarter)) < quarter
    rot = jnp.where(first, pltpu.roll(x, LANES - quarter, 1), pltpu.roll(x, quarter, 1))
    return x * cos + rot * sin


def _proj_kernel(rope, n_earlier, x_ref, sh_ref, sc_ref, w_ref, gq_ref, gk_ref, pm_ref, *rest):
    if rope:
        cos_ref, sin_ref, p_ref, uc_ref, wb_ref = rest
    else:
        earlier, rest = rest[:n_earlier], rest[n_earlier:]
        p_ref, uc_ref, nak_ref, nav_ref, nbk_ref, nbv_ref, wb_ref = rest

    @pl.when(pl.program_id(0) == 0)
    def _():
        wb_ref[...] = w_ref[...].astype(BF16)

    h = (x_ref[...] * (1.0 + sc_ref[...]) + sh_ref[...]).astype(BF16)
    proj = _dot(h, wb_ref[...])
    tm = proj.shape[0]

    qa = proj[:, 0:256]
    qa = qa * lax.rsqrt(_dot_split(qa * qa, pm_ref[...]) + RMS_EPS) * gq_ref[...]
    ka = proj[:, 256:384]
    ka = ka * lax.rsqrt(_dot_split(ka * ka, pm_ref[0:LANES, 0:LANES]) + RMS_EPS) * gk_ref[...]
    va = proj[:, 384:512]
    qb = proj[:, 512:1024]
    kb = proj[:, 1024:1536]
    vb = proj[:, 1536:2048]
    uc_ref[0] = proj[:, 2048:2048 + LANES]
    uc_ref[1] = proj[:, 2048 + LANES:2304]

    if not rope:
        for n, (ref, val) in enumerate(((nak_ref, ka), (nav_ref, va), (nbk_ref, kb), (nbv_ref, vb))):
            if n_earlier:
                layers = [e[...] for e in earlier[n::len(CACHE_WIDTHS)]] + [val]
                for l, v in enumerate(layers):
                    ref[:, l] = v.reshape(ref.shape[0], ref.shape[2], ref.shape[3])
            else:
                ref[...] = val

    def slabs(a):
        return [a[:, s:s + LANES] for s in range(0, a.shape[1], LANES)]

    if rope:
        cos, sin = cos_ref[...], sin_ref[...]
        qa_s = [_rope(s, cos, sin) for s in slabs(qa)]
        ka_s = [_rope(ka, cos, sin)]
        qb_s = [_rope(s, cos, sin) for s in slabs(qb)]
        kb_s = [_rope(s, cos, sin) for s in slabs(kb)]
    else:
        qa_s, ka_s, qb_s, kb_s = slabs(qa), [ka], slabs(qb), slabs(kb)

    scale = LOG2E * HEAD_DIM ** -0.5
    low = lax.broadcasted_iota(jnp.int32, (tm, LANES), 1) < HEAD_DIM
    zero = jnp.zeros((tm, LANES), F32)
    g0, g1 = qa_s[0] * scale, qa_s[1] * scale
    qa_pad = [jnp.where(low, g0, zero), jnp.where(low, pltpu.roll(g0, HEAD_DIM, 1), zero),
              jnp.where(low, zero, pltpu.roll(g1, HEAD_DIM, 1)), jnp.where(low, zero, g1)]
    pieces = qa_pad + ka_s + [va] + [s * scale for s in qb_s] + kb_s + slabs(vb)
    for n, piece in enumerate(pieces):
        p_ref[:, n * LANES:(n + 1) * LANES] = piece.astype(BF16)


def _project(x, mod, layer, row0, w_in, gq, gk, pmean, rope_tabs, seq=None, earlier=()):
    t = x.shape[0]
    rope = rope_tabs is not None
    nt = t // PROJ_TM
    in_specs = [
        pl.BlockSpec((PROJ_TM, D_MODEL), lambda i: (i, 0)),
        _mod_spec(layer, 3, row0, PROJ_TM),
        _mod_spec(layer, 4, row0, PROJ_TM),
        pl.BlockSpec((None, D_MODEL, IN_WIDTH), lambda i: (layer, 0, 0)),
        pl.BlockSpec((None, 1, 256), lambda i: (layer, 0, 0)),
        pl.BlockSpec((None, 1, LANES), lambda i: (layer, 0, 0)),
        pl.BlockSpec((256, 256), lambda i: (0, 0)),
    ]
    args = [x, mod, mod, w_in, gq, gk, pmean]
    out_shape = [jax.ShapeDtypeStruct((t, PROJ_W), BF16), jax.ShapeDtypeStruct((2, t, LANES), F32)]
    out_specs = [pl.BlockSpec((PROJ_TM, PROJ_W), lambda i: (i, 0)),
                 pl.BlockSpec((2, PROJ_TM, LANES), lambda i: (0, i, 0))]
    if rope:
        per_seg = SEG_LEN // PROJ_TM
        in_specs += [pl.BlockSpec((PROJ_TM, LANES), lambda i: (i % per_seg, 0))] * 2
        args += list(rope_tabs)
    else:
        for e, width in zip(earlier, CACHE_WIDTHS * (len(earlier) // len(CACHE_WIDTHS))):
            in_specs.append(pl.BlockSpec((PROJ_TM, width), lambda i: (i, 0)))
            args.append(e)
        for width in CACHE_WIDTHS:
            if earlier:
                out_shape.append(jax.ShapeDtypeStruct((t // seq, DEPTH, seq, width), F32))
                out_specs.append(pl.BlockSpec((PROJ_TM // seq, DEPTH, seq, width), lambda i: (i, 0, 0, 0)))
            else:
                out_shape.append(jax.ShapeDtypeStruct((t, width), F32))
                out_specs.append(pl.BlockSpec((PROJ_TM, width), lambda i: (i, 0)))
    return pl.pallas_call(
        functools.partial(_proj_kernel, rope, len(earlier)),
        out_shape=out_shape,
        grid=(nt,),
        in_specs=in_specs,
        out_specs=out_specs,
        scratch_shapes=[pltpu.VMEM((D_MODEL, IN_WIDTH), BF16)],
        compiler_params=_cparams("arbitrary"),
        name="project_latent" if rope else "project_context",
    )(*args)


def _ctx_attn_kernel(lam_init, p_ref, lp_ref, gs_ref, oa_ref, ob_ref, s_ref):
    ka = p_ref.at[:, KA_OFF:KA_OFF + LANES]
    va = p_ref.at[:, VA_OFF:VA_OFF + LANES]
    units = []
    for g in range(A_KV_HEADS):
        units.append((_gqa_query(p_ref[:, QA_OFF + 256 * g:QA_OFF + 256 * (g + 1)]), [(ka, va)]))
    for h in range(B_HEADS):
        k = p_ref.at[:, KB_OFF + h * LANES:KB_OFF + (h + 1) * LANES]
        v = p_ref.at[:, VB_OFF + h * LANES:VB_OFF + (h + 1) * LANES]
        units.append((_diff_query(p_ref[:, QB_OFF + h * LANES:QB_OFF + (h + 1) * LANES]), [(k, v)]))
    outs = _softmax_pv_units(units, s_ref)
    lam = _lambda(lp_ref[...]) + lam_init
    for g in range(A_KV_HEADS):
        oa_ref[:, g * LANES:(g + 1) * LANES] = _gqa_output(outs[g], g).astype(BF16)
    for h in range(B_HEADS):
        o = _diff_output(outs[A_KV_HEADS + h], lam, gs_ref[...], lam_init)
        ob_ref[:, h * LANES:(h + 1) * LANES] = o.astype(BF16)


def _context_attention(p, layer, lam_params, g_subln, lam_init, seq):
    t = p.shape[0]
    return pl.pallas_call(
        functools.partial(_ctx_attn_kernel, lam_init),
        out_shape=[jax.ShapeDtypeStruct((t, 256), BF16), jax.ShapeDtypeStruct((t, 512), BF16)],
        grid=(t // seq,),
        in_specs=[
            pl.BlockSpec((seq, PROJ_W), lambda i: (i, 0)),
            pl.BlockSpec((None, 4, B_QK_DIM), lambda i: (layer, 0, 0)),
            pl.BlockSpec((None, 1, B_V_DIM), lambda i: (layer, 0, 0)),
        ],
        out_specs=[pl.BlockSpec((seq, 256), lambda i: (i, 0)), pl.BlockSpec((seq, 512), lambda i: (i, 0))],
        scratch_shapes=[pltpu.VMEM((ATT_SLOTS, 2 * seq, seq), F32)],
        compiler_params=_cparams("parallel"),
        name="context_attention",
    )(p, lam_params, g_subln)


def _lat_gqa_kernel(q_ref, kn_ref, vn_ref, kc_ref, vc_ref, o_ref, s_ref):
    pieces = [(kc_ref, vc_ref), (kn_ref, vn_ref)]
    units = [(_gqa_query(q_ref[u * ATT_TQ:(u + 1) * ATT_TQ, :]), pieces) for u in range(ATT_UNITS)]
    outs = _softmax_pv_units(units, s_ref)
    for u in range(ATT_UNITS):
        o_ref[u * ATT_TQ:(u + 1) * ATT_TQ, :] = _gqa_output(outs[u], pl.program_id(1)).astype(BF16)


def _latent_gqa(p, cache_k, cache_v, layer, batch, length):
    tile = ATT_TQ * ATT_UNITS
    nq = length // tile
    past = cache_k.shape[2]
    return pl.pallas_call(
        _lat_gqa_kernel,
        out_shape=jax.ShapeDtypeStruct((batch * length, 256), BF16),
        grid=(batch, A_KV_HEADS, nq),
        in_specs=[
            pl.BlockSpec((tile, 256), lambda b, g, i: (b * nq + i, QA_OFF // 256 + g)),
            pl.BlockSpec((length, LANES), lambda b, g, i: (b, KA_OFF // LANES)),
            pl.BlockSpec((length, LANES), lambda b, g, i: (b, VA_OFF // LANES)),
            pl.BlockSpec((None, None, past, LANES), lambda b, g, i: (b, layer, 0, 0)),
            pl.BlockSpec((None, None, past, LANES), lambda b, g, i: (b, layer, 0, 0)),
        ],
        out_specs=pl.BlockSpec((tile, LANES), lambda b, g, i: (b * nq + i, g)),
        scratch_shapes=[pltpu.VMEM((ATT_SLOTS, 2 * ATT_TQ, past + length), F32)],
        compiler_params=_cparams("parallel", "parallel", "parallel"),
        name="latent_gqa",
    )(p, p, p, cache_k, cache_v)


def _lat_diff_kernel(lam_init, q_ref, kn_ref, vn_ref, kc_ref, vc_ref, lp_ref, gs_ref, o_ref, s_ref):
    pieces = [(kc_ref, vc_ref), (kn_ref, vn_ref)]
    units = [(_diff_query(q_ref[u * ATT_TQ:(u + 1) * ATT_TQ, :]), pieces) for u in range(ATT_UNITS)]
    outs = _softmax_pv_units(units, s_ref)
    lam = _lambda(lp_ref[...]) + lam_init
    for u in range(ATT_UNITS):
        o = _diff_output(outs[u], lam, gs_ref[...], lam_init)
        o_ref[u * ATT_TQ:(u + 1) * ATT_TQ, :] = o.astype(BF16)


def _latent_diff(p, cache_k, cache_v, layer, lam_params, g_subln, lam_init, batch, length):
    tile = ATT_TQ * ATT_UNITS
    nq = length // tile
    past = cache_k.shape[2]
    return pl.pallas_call(
        functools.partial(_lat_diff_kernel, lam_init),
        out_shape=jax.ShapeDtypeStruct((batch * length, 512), BF16),
        grid=(batch, B_HEADS, nq),
        in_specs=[
            pl.BlockSpec((tile, LANES), lambda b, h, i: (b * nq + i, QB_OFF // LANES + h)),
            pl.BlockSpec((length, LANES), lambda b, h, i: (b, KB_OFF // LANES + h)),
            pl.BlockSpec((length, LANES), lambda b, h, i: (b, VB_OFF // LANES + h)),
            pl.BlockSpec((None, None, past, LANES), lambda b, h, i: (b, layer, 0, h)),
            pl.BlockSpec((None, None, past, LANES), lambda b, h, i: (b, layer, 0, h)),
            pl.BlockSpec((None, 4, B_QK_DIM), lambda b, h, i: (layer, 0, 0)),
            pl.BlockSpec((None, 1, B_V_DIM), lambda b, h, i: (layer, 0, 0)),
        ],
        out_specs=pl.BlockSpec((tile, LANES), lambda b, h, i: (b * nq + i, h)),
        scratch_shapes=[pltpu.VMEM((ATT_SLOTS, 2 * ATT_TQ, past + length), F32)],
        compiler_params=_cparams("parallel", "parallel", "parallel"),
        name="latent_diff",
    )(p, p, p, cache_k, cache_v, lam_params, g_subln)


def _ctx_fourier_kernel(norm, u_ref, bcs_ref, cs_ref, o_ref):
    u = jnp.concatenate([u_ref[0], u_ref[1]], axis=1).astype(BF16)
    bcs = bcs_ref[...]
    rhs = jnp.concatenate([_dot(u, bcs[:C_WIDTH]), _dot(u, bcs[C_WIDTH:])], axis=0).astype(BF16)
    f = _dot(cs_ref[...], rhs) * norm
    o_ref[0] = f[:, :LANES]
    o_ref[1] = f[:, LANES:]


def _context_fourier(uc, bcs, cs, seq):
    t = uc.shape[1]
    norm = 1.0 / math.sqrt(seq * C_GROUP_DIM)
    return pl.pallas_call(
        functools.partial(_ctx_fourier_kernel, norm),
        out_shape=jax.ShapeDtypeStruct((2, t, LANES), F32),
        grid=(t // seq,),
        in_specs=[
            pl.BlockSpec((2, seq, LANES), lambda i: (0, i, 0)),
            pl.BlockSpec((2 * C_WIDTH, C_WIDTH), lambda i: (0, 0)),
            pl.BlockSpec((seq, 2 * seq), lambda i: (0, 0)),
        ],
        out_specs=pl.BlockSpec((2, seq, LANES), lambda i: (0, i, 0)),
        compiler_params=_cparams("parallel"),
        name="context_fourier",
    )(uc, bcs, cs)


def _lat_fourier_kernel(norm, u_ref, m1_ref, m2_ref, bcs_ref, tc_ref, ts_ref, o_ref, z_ref):
    n1 = GRID_W

    def strided(ref, plane, start):
        rows = pl.ds(start, n1, stride=n1)
        return jnp.concatenate([ref[(*plane, 0, rows, slice(None))], ref[(*plane, 1, rows, slice(None))]], axis=1)

    m1 = m1_ref[...]
    for b in range(n1):
        ub = strided(u_ref, (), b).astype(BF16)
        y = _dot(m1, ub)
        yr, yi = y[:n1], y[n1:]
        tc = jnp.concatenate([tc_ref[b * n1:(b + 1) * n1, :]] * 2, axis=1)
        ts = jnp.concatenate([ts_ref[b * n1:(b + 1) * n1, :]] * 2, axis=1)
        zr = yr * tc + yi * ts
        zi = yi * tc - yr * ts
        for half in range(2):
            z_ref[0, half, b * n1:(b + 1) * n1, :] = zr[:, half * LANES:(half + 1) * LANES]
            z_ref[1, half, b * n1:(b + 1) * n1, :] = zi[:, half * LANES:(half + 1) * LANES]
    m2 = m2_ref[...]
    bcs = bcs_ref[...]
    for q in range(n1):
        z = jnp.concatenate([strided(z_ref, (0,), q), strided(z_ref, (1,), q)], axis=0).astype(BF16)
        g = _dot(m2, z)
        gcat = jnp.concatenate([g[:n1], g[n1:]], axis=1).astype(BF16)
        f = _dot(gcat, bcs) * norm
        o_ref[0, pl.ds(q, n1, stride=n1), :] = f[:, :LANES]
        o_ref[1, pl.ds(q, n1, stride=n1), :] = f[:, LANES:]


def _latent_fourier(uc, m1, m2, bcs, tc, ts, length):
    t = uc.shape[1]
    norm = 1.0 / math.sqrt(length * C_GROUP_DIM)
    return pl.pallas_call(
        functools.partial(_lat_fourier_kernel, norm),
        out_shape=jax.ShapeDtypeStruct((2, t, LANES), F32),
        grid=(t // length,),
        in_specs=[
            pl.BlockSpec((2, length, LANES), lambda i: (0, i, 0)),
            pl.BlockSpec((2 * GRID_W, GRID_W), lambda i: (0, 0)),
            pl.BlockSpec((2 * GRID_W, 2 * GRID_W), lambda i: (0, 0)),
            pl.BlockSpec((2 * C_WIDTH, C_WIDTH), lambda i: (0, 0)),
            pl.BlockSpec((length, LANES), lambda i: (0, 0)),
            pl.BlockSpec((length, LANES), lambda i: (0, 0)),
        ],
        out_specs=pl.BlockSpec((2, length, LANES), lambda i: (0, i, 0)),
        scratch_shapes=[pltpu.VMEM((2, 2, length, LANES), F32)],
        compiler_params=_cparams("parallel"),
        name="latent_fourier",
    )(uc, m1, m2, bcs, tc, ts)


def _out_kernel(x_ref, gt_ref, oa_ref, ob_ref, f_ref, wf_ref, wo_ref, lng_ref, lnb_ref, o_ref, wob_ref):
    @pl.when(pl.program_id(0) == 0)
    def _():
        wob_ref[...] = wo_ref[...].astype(BF16)

    wf = wf_ref[...].astype(BF16)
    half = OUT_TM // 2
    for r in range(0, OUT_TM, half):
        rows = slice(r, r + half)
        f = jnp.concatenate([f_ref[0, rows, :], f_ref[1, rows, :]], axis=1).astype(BF16)
        oc = _dot(f, wf).astype(BF16)
        cat = jnp.concatenate([oa_ref[rows, :], ob_ref[rows, :], oc], axis=1)
        mo = _dot(cat, wob_ref[...])
        y = ALPHA * x_ref[rows, :] + gt_ref[...] * mo
        o_ref[rows, :] = _layer_norm(y, lng_ref[...], lnb_ref[...])


def _out_project(x, mod, layer, row0, oa, ob, f, w_fourier, w_out, ln_g, ln_b):
    t = x.shape[0]
    return pl.pallas_call(
        _out_kernel,
        out_shape=jax.ShapeDtypeStruct((t, D_MODEL), F32),
        grid=(t // OUT_TM,),
        in_specs=[
            pl.BlockSpec((OUT_TM, D_MODEL), lambda i: (i, 0)),
            _mod_spec(layer, 5, row0, OUT_TM),
            pl.BlockSpec((OUT_TM, 256), lambda i: (i, 0)),
            pl.BlockSpec((OUT_TM, 512), lambda i: (i, 0)),
            pl.BlockSpec((2, OUT_TM, LANES), lambda i: (0, i, 0)),
            pl.BlockSpec((None, C_WIDTH, C_WIDTH), lambda i: (layer, 0, 0)),
            pl.BlockSpec((None, D_MODEL, D_MODEL), lambda i: (layer, 0, 0)),
            pl.BlockSpec((None, None, 1, D_MODEL), lambda i: (layer, 1, 0, 0)),
            pl.BlockSpec((None, None, 1, D_MODEL), lambda i: (layer, 1, 0, 0)),
        ],
        out_specs=pl.BlockSpec((OUT_TM, D_MODEL), lambda i: (i, 0)),
        scratch_shapes=[pltpu.VMEM((D_MODEL, D_MODEL), BF16)],
        compiler_params=_cparams("arbitrary"),
        name="out_project",
    )(x, mod, oa, ob, f, w_fourier, w_out, ln_g, ln_b)


def kernel(x_prompt, x_sample, cache_a_k, cache_a_v, cache_b_k, cache_b_v, c, c_ctx, w_mod, b_mod, w_in, g_qa, g_ka, lam_q1, lam_k1, lam_q2, lam_k2, g_subln, w_fourier, w_out, w_ffn1_gu, w_ffn1_down, w_ffn2_gu, w_ffn2_down, ln_g, ln_b):
    batch, seq, d = x_prompt.shape
    dec_batch, dec_seq, _ = x_sample.shape
    past = cache_a_k.shape[2]
    assert d == D_MODEL and dec_seq == SEG_LEN and batch * seq == SEG_LEN and dec_batch == 2

    cvec = jnp.concatenate([c_ctx[None, :], c, jnp.zeros((MOD_ROWS - 1 - dec_batch, d), F32)], axis=0)
    mod = _modulation(cvec, w_mod, b_mod)

    gq = jnp.tile(g_qa, (1, A_HEADS)).reshape(DEPTH, 1, A_HEADS * HEAD_DIM)
    gk = jnp.tile(g_ka, (1, A_KV_HEADS)).reshape(DEPTH, 1, A_KV_HEADS * HEAD_DIM)
    gs = g_subln.reshape(DEPTH, 1, B_V_DIM)
    lam_params = jnp.stack([lam_q1, lam_k1, lam_q2, lam_k2], axis=1)
    ln_g4 = ln_g.reshape(DEPTH, 3, 1, d)
    ln_b4 = ln_b.reshape(DEPTH, 3, 1, d)
    pmean = _group_mean_matrix(256, HEAD_DIM).astype(BF16)
    rope_tabs = _rope_tables(dec_seq)
    bcs = _channel_dft().astype(BF16)
    cs_ctx = _position_dft_small(seq).astype(BF16)
    m1, m2, tw_cos, tw_sin = _two_stage_dft(GRID_W)
    m1, m2 = m1.astype(BF16), m2.astype(BF16)
    ck_a = cache_a_k.reshape(dec_batch, DEPTH, past, A_KV_HEADS * HEAD_DIM)
    cv_a = cache_a_v.reshape(dec_batch, DEPTH, past, A_KV_HEADS * HEAD_DIM)
    ck_b = cache_b_k.reshape(dec_batch, DEPTH, past, B_HEADS * 2 * B_QK_DIM)
    cv_b = cache_b_v.reshape(dec_batch, DEPTH, past, B_HEADS * B_V_DIM)

    xp = x_prompt.reshape(batch * seq, d)
    xs = x_sample.reshape(dec_batch * dec_seq, d)
    new = []
    for layer in range(DEPTH):
        lam_init = 0.8 - 0.6 * math.exp(-0.3 * layer)
        xp, xs = _ffn(xp, xs, mod, layer, 0, w_ffn1_gu, w_ffn1_down, ln_g4, ln_b4)
        earlier = tuple(new) if layer == DEPTH - 1 else ()
        pp, ucp, *caches = _project(xp, mod, layer, 0, w_in, gq, gk, pmean, None, seq, earlier)
        new = caches if layer == DEPTH - 1 else new + caches
        oa, ob = _context_attention(pp, layer, lam_params, gs, lam_init, seq)
        fo = _context_fourier(ucp, bcs, cs_ctx, seq)
        xp = _out_project(xp, mod, layer, 0, oa, ob, fo, w_fourier, w_out, ln_g4, ln_b4)
        ps, ucs = _project(xs, mod, layer, 1, w_in, gq, gk, pmean, rope_tabs)
        oa = _latent_gqa(ps, ck_a, cv_a, layer, dec_batch, dec_seq)
        ob = _latent_diff(ps, ck_b, cv_b, layer, lam_params, gs, lam_init, dec_batch, dec_seq)
        fo = _latent_fourier(ucs, m1, m2, bcs, tw_cos, tw_sin, dec_seq)
        xs = _out_project(xs, mod, layer, 1, oa, ob, fo, w_fourier, w_out, ln_g4, ln_b4)
        xp, xs = _ffn(xp, xs, mod, layer, 2, w_ffn2_gu, w_ffn2_down, ln_g4, ln_b4)

    new_a_k, new_a_v, new_b_k, new_b_v = new
    return (xp.reshape(batch, seq, d), xs.reshape(dec_batch, dec_seq, d),
            new_a_k.reshape(batch, DEPTH, seq, A_KV_HEADS, HEAD_DIM),
            new_a_v.reshape(batch, DEPTH, seq, A_KV_HEADS, HEAD_DIM),
            new_b_k.reshape(batch, DEPTH, seq, B_HEADS, 2 * B_QK_DIM),
            new_b_v.reshape(batch, DEPTH, seq, B_HEADS, B_V_DIM))
```

```python
import functools
import math

import numpy as np
import jax
import jax.numpy as jnp
from jax import lax
from jax.experimental import pallas as pl
from jax.experimental.pallas import tpu as pltpu

F32 = jnp.float32
BF16 = jnp.bfloat16

D_MODEL = 1024
DEPTH = 2
GRID_W = 64
HEAD_DIM = 64
ROPE_THETA = 10000.0
A_HEADS = 4
A_KV_HEADS = 2
B_HEADS = 4
B_QK_DIM = 64
B_V_DIM = 128
C_GROUPS = 4
C_GROUP_DIM = 64
C_WIDTH = 256
FFN_DIM = 2816
N_MOD = 9
IN_WIDTH = 2304
ALPHA = (2.0 * DEPTH) ** 0.25
LN_EPS = 1e-5
RMS_EPS = 1e-6

LANES = 128
MOD_ROWS = 8
SEG_LEN = 4096
VMEM_LIMIT = 52 << 20

QA_OFF, KA_OFF, VA_OFF, QB_OFF, KB_OFF, VB_OFF = 0, 512, 640, 768, 1280, 1792
PROJ_W = 2304
CACHE_WIDTHS = (128, 128, 512, 512)

FFN_TM = 512
FFN_TF = 256
FFN_STAGES = 3
PROJ_TM = 512
OUT_TM = 1024
ATT_TQ = 256
ATT_UNITS = 4
ATT_SLOTS = 2
ATT_KC = 512
CTX_ATT_SEQS = 2
CTX_FOUR_SEQS = 4
LOG2E = math.log2(math.e)


def _cparams(*sem):
    return pltpu.CompilerParams(dimension_semantics=sem, vmem_limit_bytes=VMEM_LIMIT)


def _rope_tables(length):
    rows = length // GRID_W
    row = np.repeat(np.arange(rows), GRID_W).astype(np.float64)
    col = np.tile(np.arange(GRID_W), rows).astype(np.float64)
    half = HEAD_DIM // 2
    inv = 1.0 / (ROPE_THETA ** (np.arange(0, half, 2, dtype=np.float64) / half))
    ar = row[:, None] * inv
    ac = col[:, None] * inv
    cos = np.concatenate([np.cos(ar), np.cos(ar), np.cos(ac), np.cos(ac)], -1)
    sin = np.concatenate([-np.sin(ar), np.sin(ar), -np.sin(ac), np.sin(ac)], -1)
    return (jnp.asarray(np.tile(cos, (1, 2)), F32), jnp.asarray(np.tile(sin, (1, 2)), F32))


def _group_mean_matrix(width, group):
    m = np.kron(np.eye(width // group), np.full((group, group), 1.0 / group))
    return jnp.asarray(m, F32)


def _channel_dft():
    c = np.arange(C_GROUP_DIM)
    ang = 2.0 * np.pi * np.outer(c, c) / C_GROUP_DIM
    eye = np.eye(C_GROUPS)
    return jnp.asarray(np.concatenate([np.kron(eye, np.cos(ang)), np.kron(eye, np.sin(ang))], 0), F32)


def _position_dft_small(length):
    n = np.arange(length)
    ang = 2.0 * np.pi * ((np.outer(n, n)) % length) / length
    return jnp.asarray(np.concatenate([np.cos(ang), -np.sin(ang)], 1), F32)


def _two_stage_dft(n1):
    i = np.arange(n1)
    ang = 2.0 * np.pi * np.outer(i, i) / n1
    c, s = np.cos(ang), np.sin(ang)
    m1 = np.concatenate([c, -s], 0)
    m2 = np.block([[c, s], [-s, c]])
    tw = 2.0 * np.pi * np.outer(i, i).reshape(-1) / (n1 * n1)
    tc = np.repeat(np.cos(tw)[:, None], LANES, 1)
    ts = np.repeat(np.sin(tw)[:, None], LANES, 1)
    return tuple(jnp.asarray(a, F32) for a in (m1, m2, tc, ts))


def _layer_norm(y, g, b):
    mu = jnp.mean(y, axis=-1, keepdims=True)
    yc = y - mu
    var = jnp.mean(yc * yc, axis=-1, keepdims=True)
    return yc * lax.rsqrt(var + LN_EPS) * g + b


def _dot(a, b):
    return jnp.dot(a, b, preferred_element_type=F32)


def _dot_nt(a, b):
    return lax.dot_general(a, b, (((1,), (1,)), ((), ())), preferred_element_type=F32)


def _dot_split(a, b):
    hi = a.astype(BF16)
    lo = (a - hi.astype(F32)).astype(BF16)
    return _dot(hi, b) + _dot(lo, b)


def _softmax_pv_units(units, s_ref):
    rows = units[0][0].shape[0]
    chunk_lists = []
    for _, pieces in units:
        chunks, off = [], 0
        for k_ref, v_ref in pieces:
            for st in range(0, k_ref.shape[0], ATT_KC):
                sz = min(ATT_KC, k_ref.shape[0] - st)
                chunks.append((k_ref, v_ref, st, sz, off))
                off += sz
        chunk_lists.append(chunks)
    nchunk = len(chunk_lists[0])
    run_max = [None] * len(units)
    row_max = [None] * len(units)
    acc = [None] * len(units)

    def scores(u, ci):
        k_ref, _, st, sz, off = chunk_lists[u][ci]
        s = _dot_nt(units[u][0], k_ref[st:st + sz, :].astype(BF16))
        s_ref[u % ATT_SLOTS, :, off:off + sz] = s
        part = functools.reduce(jnp.maximum, [s[:, j:j + LANES] for j in range(0, sz, LANES)])
        run_max[u] = part if run_max[u] is None else jnp.maximum(run_max[u], part)

    def finish_max(u):
        m = jnp.max(run_max[u], axis=1, keepdims=True)
        row_max[u] = jnp.broadcast_to(m, (rows, LANES))

    def weighted(u, ci):
        _, v_ref, st, sz, off = chunk_lists[u][ci]
        e = jnp.concatenate([jnp.exp2(s_ref[u % ATT_SLOTS, :, off + j:off + j + LANES] - row_max[u])
                             for j in range(0, sz, LANES)], axis=1).astype(BF16)
        v1 = jnp.concatenate([v_ref[st:st + sz, :].astype(BF16), jnp.ones((sz, LANES), BF16)], axis=1)
        d = _dot(e, v1)
        acc[u] = d if acc[u] is None else acc[u] + d

    for ci in range(nchunk):
        scores(0, ci)
    finish_max(0)
    for u in range(1, len(units)):
        for ci in range(nchunk):
            weighted(u - 1, ci)
            scores(u, ci)
        finish_max(u)
    for ci in range(nchunk):
        weighted(len(units) - 1, ci)
    return [a[:, :LANES] / a[:, LANES:] for a in acc]


def _gqa_query(qpad):
    return jnp.concatenate([qpad[:, :LANES], qpad[:, LANES:]], axis=0)


def _gqa_output(o, group):
    tq = o.shape[0] // 2
    o0, o1 = o[:tq], o[tq:]
    low = lax.broadcasted_iota(jnp.int32, (tq, LANES), 1) < HEAD_DIM
    first = group == 0
    r0, r1 = pltpu.roll(o0, HEAD_DIM, 1), pltpu.roll(o1, HEAD_DIM, 1)
    return jnp.where(low, jnp.where(first, o0, r0), jnp.where(first, r1, o1))


def _diff_query(q):
    low = lax.broadcasted_iota(jnp.int32, q.shape, 1) < B_QK_DIM
    zero = jnp.zeros_like(q)
    return jnp.concatenate([jnp.where(low, q, zero), jnp.where(low, zero, q)], axis=0)


def _diff_output(o, lam, g_sub, lam_init):
    tq = o.shape[0] // 2
    w = o[:tq] - lam * o[tq:]
    y = w * lax.rsqrt(jnp.mean(w * w, axis=-1, keepdims=True) + RMS_EPS)
    return y * g_sub * (1.0 - lam_init)


def _lambda(lp):
    s1 = jnp.sum(lp[0:1] * lp[1:2], axis=1, keepdims=True)
    s2 = jnp.sum(lp[2:3] * lp[3:4], axis=1, keepdims=True)
    return jnp.exp(s1) - jnp.exp(s2)


def _mod_kernel(c_ref, w_ref, b_ref, o_ref):
    c = c_ref[...]
    s = (c * jax.nn.sigmoid(c)).astype(BF16)
    o_ref[...] = _dot(s, w_ref[...].astype(BF16)) + b_ref[...]


def _modulation(cvec, w_mod, b_mod):
    out = pl.pallas_call(
        _mod_kernel,
        out_shape=jax.ShapeDtypeStruct((DEPTH, N_MOD, MOD_ROWS, D_MODEL), F32),
        grid=(DEPTH, N_MOD),
        in_specs=[
            pl.BlockSpec((MOD_ROWS, D_MODEL), lambda l, n: (0, 0)),
            pl.BlockSpec((None, D_MODEL, D_MODEL), lambda l, n: (l, 0, n)),
            pl.BlockSpec((None, None, 1, D_MODEL), lambda l, n: (l, n, 0, 0)),
        ],
        out_specs=pl.BlockSpec((None, None, MOD_ROWS, D_MODEL), lambda l, n: (l, n, 0, 0)),
        compiler_params=_cparams("parallel", "parallel"),
        name="modulation",
    )(cvec, w_mod, b_mod.reshape(DEPTH, N_MOD, 1, D_MODEL))
    return out.reshape(DEPTH, N_MOD, MOD_ROWS, 1, D_MODEL)


def _mod_spec(layer, chunk, row0, tm):
    return pl.BlockSpec((None, None, None, 1, D_MODEL),
                        lambda i: (layer, chunk, row0 + (i * tm) // SEG_LEN, 0, 0))


def _ffn_kernel(layer, n_ctx, xp_ref, xs_ref, sh_ref, sc_ref, gt_ref, wgu_hbm, wd_hbm, lng_ref, lnb_ref,
                op_ref, os_ref, wg_ref, wu_ref, wd_ref, stage_gu, stage_d, sem, h_ref, acc_ref):
    nf = FFN_DIM // FFN_TF
    step = pl.program_id(0)
    first_step = step == 0
    is_ctx = step < n_ctx

    def chunk_copies(c, slot):
        col = c * FFN_TF if isinstance(c, int) else pl.multiple_of(c * FFN_TF, FFN_TF)
        return (
            pltpu.make_async_copy(wgu_hbm.at[layer, :, pl.ds(col, FFN_TF)], stage_gu.at[slot, 0], sem.at[slot, 0]),
            pltpu.make_async_copy(wgu_hbm.at[layer, :, pl.ds(FFN_DIM + col, FFN_TF)], stage_gu.at[slot, 1],
                                  sem.at[slot, 1]),
            pltpu.make_async_copy(wd_hbm.at[layer, pl.ds(col, FFN_TF), :], stage_d.at[slot], sem.at[slot, 2]),
        )

    def start_chunk(c, slot):
        for cp in chunk_copies(c, slot):
            cp.start()

    def modulated(x):
        return (x * (1.0 + sc_ref[...]) + sh_ref[...]).astype(BF16)

    def chunk(h, c):
        g = _dot(h, wg_ref[c])
        u = _dot(h, wu_ref[c])
        a = (g * jax.nn.sigmoid(g) * u).astype(BF16)
        return _dot(a, wd_ref[c])

    def finish(x, acc, o_ref):
        y = ALPHA * x + (0.5 * gt_ref[...]) * acc
        o_ref[...] = _layer_norm(y, lng_ref[...], lnb_ref[...])

    def later_step(x_ref, o_ref):
        x = x_ref[...]
        h = modulated(x)
        acc = chunk(h, 0)
        for c in range(1, nf):
            acc = acc + chunk(h, c)
        finish(x, acc, o_ref)

    @pl.when(first_step)
    def _():
        for c in range(FFN_STAGES - 1):
            start_chunk(c, c)
        x = xp_ref[...]
        h_ref[...] = modulated(x)
        acc_ref[...] = jnp.zeros_like(acc_ref)

        def body(c, carry):
            ahead = c + (FFN_STAGES - 1)

            @pl.when(ahead < nf)
            def _():
                start_chunk(ahead, lax.rem(ahead, FFN_STAGES))

            slot = lax.rem(c, FFN_STAGES)
            for cp in chunk_copies(c, slot):
                cp.wait()
            wg_ref[c] = stage_gu[slot, 0].astype(BF16)
            wu_ref[c] = stage_gu[slot, 1].astype(BF16)
            wd_ref[c] = stage_d[slot].astype(BF16)
            acc_ref[...] += chunk(h_ref[...], c)
            return carry

        lax.fori_loop(0, nf, body, 0)
        finish(x, acc_ref[...], op_ref)

    @pl.when(jnp.logical_and(jnp.logical_not(first_step), is_ctx))
    def _():
        later_step(xp_ref, op_ref)

    @pl.when(jnp.logical_not(is_ctx))
    def _():
        later_step(xs_ref, os_ref)


def _ffn(xp, xs, mod, layer, sub, w_gu, w_down, ln_g, ln_b):
    n_ctx = xp.shape[0] // FFN_TM
    n_lat = xs.shape[0] // FFN_TM
    nf = FFN_DIM // FFN_TF
    assert n_ctx >= 1 and n_lat >= 1 and nf >= FFN_STAGES

    def ctx_map(i):
        return (jnp.minimum(i, n_ctx - 1), 0)

    def lat_map(i):
        return (jnp.maximum(i - n_ctx, 0), 0)

    def mod_spec(chunk):
        def index(i):
            row = jnp.where(i < n_ctx, 0, 1 + (jnp.maximum(i - n_ctx, 0) * FFN_TM) // SEG_LEN)
            return (layer, chunk, row, 0, 0)
        return pl.BlockSpec((None, None, None, 1, D_MODEL), index)

    return pl.pallas_call(
        functools.partial(_ffn_kernel, layer, n_ctx),
        out_shape=[jax.ShapeDtypeStruct(xp.shape, F32), jax.ShapeDtypeStruct(xs.shape, F32)],
        grid=(n_ctx + n_lat,),
        in_specs=[
            pl.BlockSpec((FFN_TM, D_MODEL), ctx_map),
            pl.BlockSpec((FFN_TM, D_MODEL), lat_map),
            mod_spec(3 * sub), mod_spec(3 * sub + 1), mod_spec(3 * sub + 2),
            pl.BlockSpec(memory_space=pl.ANY),
            pl.BlockSpec(memory_space=pl.ANY),
            pl.BlockSpec((None, None, 1, D_MODEL), lambda i: (layer, sub, 0, 0)),
            pl.BlockSpec((None, None, 1, D_MODEL), lambda i: (layer, sub, 0, 0)),
        ],
        out_specs=[pl.BlockSpec((FFN_TM, D_MODEL), ctx_map), pl.BlockSpec((FFN_TM, D_MODEL), lat_map)],
        scratch_shapes=[
            pltpu.VMEM((nf, D_MODEL, FFN_TF), BF16),
            pltpu.VMEM((nf, D_MODEL, FFN_TF), BF16),
            pltpu.VMEM((nf, FFN_TF, D_MODEL), BF16),
            pltpu.VMEM((FFN_STAGES, 2, D_MODEL, FFN_TF), F32),
            pltpu.VMEM((FFN_STAGES, FFN_TF, D_MODEL), F32),
            pltpu.SemaphoreType.DMA((FFN_STAGES, 3)),
            pltpu.VMEM((FFN_TM, D_MODEL), BF16),
            pltpu.VMEM((FFN_TM, D_MODEL), F32),
        ],
        compiler_params=_cparams("arbitrary"),
        name="ffn",
    )(xp, xs, mod, mod, mod, w_gu, w_down, ln_g, ln_b)


def _rope(x, cos, sin):
    quarter = HEAD_DIM // 4
    lane = lax.broadcasted_iota(jnp.int32, x.shape, 1)
    first = (lane % (2 * quarter)) < quarter
    rot = jnp.where(first, pltpu.roll(x, LANES - quarter, 1), pltpu.roll(x, quarter, 1))
    return x * cos + rot * sin


def _proj_kernel(rope, n_earlier, x_ref, sh_ref, sc_ref, w_ref, gq_ref, gk_ref, pm_ref, *rest):
    if rope:
        cos_ref, sin_ref, p_ref, uc_ref, wb_ref = rest
    else:
        earlier, rest = rest[:n_earlier], rest[n_earlier:]
        p_ref, uc_ref, nak_ref, nav_ref, nbk_ref, nbv_ref, wb_ref = rest

    @pl.when(pl.program_id(0) == 0)
    def _():
        wb_ref[...] = w_ref[...].astype(BF16)

    h = (x_ref[...] * (1.0 + sc_ref[...]) + sh_ref[...]).astype(BF16)
    proj = _dot(h, wb_ref[...])
    tm = proj.shape[0]

    qa = proj[:, 0:256]
    qa = qa * lax.rsqrt(_dot_split(qa * qa, pm_ref[...]) + RMS_EPS) * gq_ref[...]
    ka = proj[:, 256:384]
    ka = ka * lax.rsqrt(_dot_split(ka * ka, pm_ref[0:LANES, 0:LANES]) + RMS_EPS) * gk_ref[...]
    va = proj[:, 384:512]
    qb = proj[:, 512:1024]
    kb = proj[:, 1024:1536]
    vb = proj[:, 1536:2048]
    uc_ref[0] = proj[:, 2048:2048 + LANES]
    uc_ref[1] = proj[:, 2048 + LANES:2304]

    if not rope:
        for n, (ref, val) in enumerate(((nak_ref, ka), (nav_ref, va), (nbk_ref, kb), (nbv_ref, vb))):
            if n_earlier:
                layers = [e[...] for e in earlier[n::len(CACHE_WIDTHS)]] + [val]
                for l, v in enumerate(layers):
                    ref[:, l] = v.reshape(ref.shape[0], ref.shape[2], ref.shape[3])
            else:
                ref[...] = val

    def slabs(a):
        return [a[:, s:s + LANES] for s in range(0, a.shape[1], LANES)]

    if rope:
        cos, sin = cos_ref[...], sin_ref[...]
        qa_s = [_rope(s, cos, sin) for s in slabs(qa)]
        ka_s = [_rope(ka, cos, sin)]
        qb_s = [_rope(s, cos, sin) for s in slabs(qb)]
        kb_s = [_rope(s, cos, sin) for s in slabs(kb)]
    else:
        qa_s, ka_s, qb_s, kb_s = slabs(qa), [ka], slabs(qb), slabs(kb)

    scale = LOG2E * HEAD_DIM ** -0.5
    low = lax.broadcasted_iota(jnp.int32, (tm, LANES), 1) < HEAD_DIM
    zero = jnp.zeros((tm, LANES), F32)
    g0, g1 = qa_s[0] * scale, qa_s[1] * scale
    qa_pad = [jnp.where(low, g0, zero), jnp.where(low, pltpu.roll(g0, HEAD_DIM, 1), zero),
              jnp.where(low, zero, pltpu.roll(g1, HEAD_DIM, 1)), jnp.where(low, zero, g1)]
    pieces = qa_pad + ka_s + [va] + [s * scale for s in qb_s] + kb_s + slabs(vb)
    for n, piece in enumerate(pieces):
        p_ref[:, n * LANES:(n + 1) * LANES] = piece.astype(BF16)


def _project(x, mod, layer, row0, w_in, gq, gk, pmean, rope_tabs, seq=None, earlier=()):
    t = x.shape[0]
    rope = rope_tabs is not None
    nt = t // PROJ_TM
    in_specs = [
        pl.BlockSpec((PROJ_TM, D_MODEL), lambda i: (i, 0)),
        _mod_spec(layer, 3, row0, PROJ_TM),
        _mod_spec(layer, 4, row0, PROJ_TM),
        pl.BlockSpec((None, D_MODEL, IN_WIDTH), lambda i: (layer, 0, 0)),
        pl.BlockSpec((None, 1, 256), lambda i: (layer, 0, 0)),
        pl.BlockSpec((None, 1, LANES), lambda i: (layer, 0, 0)),
        pl.BlockSpec((256, 256), lambda i: (0, 0)),
    ]
    args = [x, mod, mod, w_in, gq, gk, pmean]
    out_shape = [jax.ShapeDtypeStruct((t, PROJ_W), BF16), jax.ShapeDtypeStruct((2, t, LANES), F32)]
    out_specs = [pl.BlockSpec((PROJ_TM, PROJ_W), lambda i: (i, 0)),
                 pl.BlockSpec((2, PROJ_TM, LANES), lambda i: (0, i, 0))]
    if rope:
        per_seg = SEG_LEN // PROJ_TM
        in_specs += [pl.BlockSpec((PROJ_TM, LANES), lambda i: (i % per_seg, 0))] * 2
        args += list(rope_tabs)
    else:
        for e, width in zip(earlier, CACHE_WIDTHS * (len(earlier) // len(CACHE_WIDTHS))):
            in_specs.append(pl.BlockSpec((PROJ_TM, width), lambda i: (i, 0)))
            args.append(e)
        for width in CACHE_WIDTHS:
            if earlier:
                out_shape.append(jax.ShapeDtypeStruct((t // seq, DEPTH, seq, width), F32))
                out_specs.append(pl.BlockSpec((PROJ_TM // seq, DEPTH, seq, width), lambda i: (i, 0, 0, 0)))
            else:
                out_shape.append(jax.ShapeDtypeStruct((t, width), F32))
                out_specs.append(pl.BlockSpec((PROJ_TM, width), lambda i: (i, 0)))
    return pl.pallas_call(
        functools.partial(_proj_kernel, rope, len(earlier)),
        out_shape=out_shape,
        grid=(nt,),
        in_specs=in_specs,
        out_specs=out_specs,
        scratch_shapes=[pltpu.VMEM((D_MODEL, IN_WIDTH), BF16)],
        compiler_params=_cparams("arbitrary"),
        name="project_latent" if rope else "project_context",
    )(*args)


def _ctx_attn_kernel(lam_init, seq, p_ref, lp_ref, gs_ref, oa_ref, ob_ref, s_ref):
    units = []
    for r in range(0, p_ref.shape[0], seq):
        rows = slice(r, r + seq)
        ka = p_ref.at[rows, KA_OFF:KA_OFF + LANES]
        va = p_ref.at[rows, VA_OFF:VA_OFF + LANES]
        for g in range(A_KV_HEADS):
            units.append((_gqa_query(p_ref[rows, QA_OFF + 256 * g:QA_OFF + 256 * (g + 1)]), [(ka, va)]))
        for h in range(B_HEADS):
            k = p_ref.at[rows, KB_OFF + h * LANES:KB_OFF + (h + 1) * LANES]
            v = p_ref.at[rows, VB_OFF + h * LANES:VB_OFF + (h + 1) * LANES]
            units.append((_diff_query(p_ref[rows, QB_OFF + h * LANES:QB_OFF + (h + 1) * LANES]), [(k, v)]))
    outs = _softmax_pv_units(units, s_ref)
    lam = _lambda(lp_ref[...]) + lam_init
    per_seq = A_KV_HEADS + B_HEADS
    for n, r in enumerate(range(0, p_ref.shape[0], seq)):
        rows = slice(r, r + seq)
        for g in range(A_KV_HEADS):
            oa_ref[rows, g * LANES:(g + 1) * LANES] = _gqa_output(outs[n * per_seq + g], g).astype(BF16)
        for h in range(B_HEADS):
            o = _diff_output(outs[n * per_seq + A_KV_HEADS + h], lam, gs_ref[...], lam_init)
            ob_ref[rows, h * LANES:(h + 1) * LANES] = o.astype(BF16)


def _context_attention(p, layer, lam_params, g_subln, lam_init, seq):
    t = p.shape[0]
    tile = CTX_ATT_SEQS * seq
    return pl.pallas_call(
        functools.partial(_ctx_attn_kernel, lam_init, seq),
        out_shape=[jax.ShapeDtypeStruct((t, 256), BF16), jax.ShapeDtypeStruct((t, 512), BF16)],
        grid=(t // tile,),
        in_specs=[
            pl.BlockSpec((tile, PROJ_W), lambda i: (i, 0)),
            pl.BlockSpec((None, 4, B_QK_DIM), lambda i: (layer, 0, 0)),
            pl.BlockSpec((None, 1, B_V_DIM), lambda i: (layer, 0, 0)),
        ],
        out_specs=[pl.BlockSpec((tile, 256), lambda i: (i, 0)), pl.BlockSpec((tile, 512), lambda i: (i, 0))],
        scratch_shapes=[pltpu.VMEM((ATT_SLOTS, 2 * seq, seq), F32)],
        compiler_params=_cparams("parallel"),
        name="context_attention",
    )(p, lam_params, g_subln)


def _lat_gqa_kernel(q_ref, kn_ref, vn_ref, kc_ref, vc_ref, o_ref, s_ref):
    pieces = [(kc_ref, vc_ref), (kn_ref, vn_ref)]
    units = [(_gqa_query(q_ref[u * ATT_TQ:(u + 1) * ATT_TQ, :]), pieces) for u in range(ATT_UNITS)]
    outs = _softmax_pv_units(units, s_ref)
    for u in range(ATT_UNITS):
        o_ref[u * ATT_TQ:(u + 1) * ATT_TQ, :] = _gqa_output(outs[u], pl.program_id(1)).astype(BF16)


def _latent_gqa(p, cache_k, cache_v, layer, batch, length):
    tile = ATT_TQ * ATT_UNITS
    nq = length // tile
    past = cache_k.shape[2]
    return pl.pallas_call(
        _lat_gqa_kernel,
        out_shape=jax.ShapeDtypeStruct((batch * length, 256), BF16),
        grid=(batch, A_KV_HEADS, nq),
        in_specs=[
            pl.BlockSpec((tile, 256), lambda b, g, i: (b * nq + i, QA_OFF // 256 + g)),
            pl.BlockSpec((length, LANES), lambda b, g, i: (b, KA_OFF // LANES)),
            pl.BlockSpec((length, LANES), lambda b, g, i: (b, VA_OFF // LANES)),
            pl.BlockSpec((None, None, past, LANES), lambda b, g, i: (b, layer, 0, 0)),
            pl.BlockSpec((None, None, past, LANES), lambda b, g, i: (b, layer, 0, 0)),
        ],
        out_specs=pl.BlockSpec((tile, LANES), lambda b, g, i: (b * nq + i, g)),
        scratch_shapes=[pltpu.VMEM((ATT_SLOTS, 2 * ATT_TQ, past + length), F32)],
        compiler_params=_cparams("parallel", "parallel", "parallel"),
        name="latent_gqa",
    )(p, p, p, cache_k, cache_v)


def _lat_diff_kernel(lam_init, q_ref, kn_ref, vn_ref, kc_ref, vc_ref, lp_ref, gs_ref, o_ref, s_ref):
    pieces = [(kc_ref, vc_ref), (kn_ref, vn_ref)]
    units = [(_diff_query(q_ref[u * ATT_TQ:(u + 1) * ATT_TQ, :]), pieces) for u in range(ATT_UNITS)]
    outs = _softmax_pv_units(units, s_ref)
    lam = _lambda(lp_ref[...]) + lam_init
    for u in range(ATT_UNITS):
        o = _diff_output(outs[u], lam, gs_ref[...], lam_init)
        o_ref[u * ATT_TQ:(u + 1) * ATT_TQ, :] = o.astype(BF16)


def _latent_diff(p, cache_k, cache_v, layer, lam_params, g_subln, lam_init, batch, length):
    tile = ATT_TQ * ATT_UNITS
    nq = length // tile
    past = cache_k.shape[2]
    return pl.pallas_call(
        functools.partial(_lat_diff_kernel, lam_init),
        out_shape=jax.ShapeDtypeStruct((batch * length, 512), BF16),
        grid=(batch, B_HEADS, nq),
        in_specs=[
            pl.BlockSpec((tile, LANES), lambda b, h, i: (b * nq + i, QB_OFF // LANES + h)),
            pl.BlockSpec((length, LANES), lambda b, h, i: (b, KB_OFF // LANES + h)),
            pl.BlockSpec((length, LANES), lambda b, h, i: (b, VB_OFF // LANES + h)),
            pl.BlockSpec((None, None, past, LANES), lambda b, h, i: (b, layer, 0, h)),
            pl.BlockSpec((None, None, past, LANES), lambda b, h, i: (b, layer, 0, h)),
            pl.BlockSpec((None, 4, B_QK_DIM), lambda b, h, i: (layer, 0, 0)),
            pl.BlockSpec((None, 1, B_V_DIM), lambda b, h, i: (layer, 0, 0)),
        ],
        out_specs=pl.BlockSpec((tile, LANES), lambda b, h, i: (b * nq + i, h)),
        scratch_shapes=[pltpu.VMEM((ATT_SLOTS, 2 * ATT_TQ, past + length), F32)],
        compiler_params=_cparams("parallel", "parallel", "parallel"),
        name="latent_diff",
    )(p, p, p, cache_k, cache_v, lam_params, g_subln)


def _ctx_fourier_kernel(norm, seq, u_ref, bcs_ref, cs_ref, o_ref):
    bcs = bcs_ref[...]
    for r in range(0, u_ref.shape[1], seq):
        rows = slice(r, r + seq)
        u = jnp.concatenate([u_ref[0, rows, :], u_ref[1, rows, :]], axis=1).astype(BF16)
        rhs = jnp.concatenate([_dot(u, bcs[:C_WIDTH]), _dot(u, bcs[C_WIDTH:])], axis=0).astype(BF16)
        f = _dot(cs_ref[...], rhs) * norm
        o_ref[0, rows, :] = f[:, :LANES]
        o_ref[1, rows, :] = f[:, LANES:]


def _context_fourier(uc, bcs, cs, seq):
    t = uc.shape[1]
    tile = CTX_FOUR_SEQS * seq
    norm = 1.0 / math.sqrt(seq * C_GROUP_DIM)
    return pl.pallas_call(
        functools.partial(_ctx_fourier_kernel, norm, seq),
        out_shape=jax.ShapeDtypeStruct((2, t, LANES), F32),
        grid=(t // tile,),
        in_specs=[
            pl.BlockSpec((2, tile, LANES), lambda i: (0, i, 0)),
            pl.BlockSpec((2 * C_WIDTH, C_WIDTH), lambda i: (0, 0)),
            pl.BlockSpec((seq, 2 * seq), lambda i: (0, 0)),
        ],
        out_specs=pl.BlockSpec((2, tile, LANES), lambda i: (0, i, 0)),
        compiler_params=_cparams("parallel"),
        name="context_fourier",
    )(uc, bcs, cs)


def _lat_fourier_kernel(norm, u_ref, m1_ref, m2_ref, bcs_ref, tc_ref, ts_ref, o_ref, z_ref):
    n1 = GRID_W

    def strided(ref, plane, start):
        rows = pl.ds(start, n1, stride=n1)
        return jnp.concatenate([ref[(*plane, 0, rows, slice(None))], ref[(*plane, 1, rows, slice(None))]], axis=1)

    m1 = m1_ref[...]
    for b in range(n1):
        ub = strided(u_ref, (), b).astype(BF16)
        y = _dot(m1, ub)
        yr, yi = y[:n1], y[n1:]
        tc = jnp.concatenate([tc_ref[b * n1:(b + 1) * n1, :]] * 2, axis=1)
        ts = jnp.concatenate([ts_ref[b * n1:(b + 1) * n1, :]] * 2, axis=1)
        zr = yr * tc + yi * ts
        zi = yi * tc - yr * ts
        for half in range(2):
            z_ref[0, half, b * n1:(b + 1) * n1, :] = zr[:, half * LANES:(half + 1) * LANES]
            z_ref[1, half, b * n1:(b + 1) * n1, :] = zi[:, half * LANES:(half + 1) * LANES]
    m2 = m2_ref[...]
    bcs = bcs_ref[...]
    for q in range(n1):
        z = jnp.concatenate([strided(z_ref, (0,), q), strided(z_ref, (1,), q)], axis=0).astype(BF16)
        g = _dot(m2, z)
        gcat = jnp.concatenate([g[:n1], g[n1:]], axis=1).astype(BF16)
        f = _dot(gcat, bcs) * norm
        o_ref[0, pl.ds(q, n1, stride=n1), :] = f[:, :LANES]
        o_ref[1, pl.ds(q, n1, stride=n1), :] = f[:, LANES:]


def _latent_fourier(uc, m1, m2, bcs, tc, ts, length):
    t = uc.shape[1]
    norm = 1.0 / math.sqrt(length * C_GROUP_DIM)
    return pl.pallas_call(
        functools.partial(_lat_fourier_kernel, norm),
        out_shape=jax.ShapeDtypeStruct((2, t, LANES), F32),
        grid=(t // length,),
        in_specs=[
            pl.BlockSpec((2, length, LANES), lambda i: (0, i, 0)),
            pl.BlockSpec((2 * GRID_W, GRID_W), lambda i: (0, 0)),
            pl.BlockSpec((2 * GRID_W, 2 * GRID_W), lambda i: (0, 0)),
            pl.BlockSpec((2 * C_WIDTH, C_WIDTH), lambda i: (0, 0)),
            pl.BlockSpec((length, LANES), lambda i: (0, 0)),
            pl.BlockSpec((length, LANES), lambda i: (0, 0)),
        ],
        out_specs=pl.BlockSpec((2, length, LANES), lambda i: (0, i, 0)),
        scratch_shapes=[pltpu.VMEM((2, 2, length, LANES), F32)],
        compiler_params=_cparams("parallel"),
        name="latent_fourier",
    )(uc, m1, m2, bcs, tc, ts)


def _out_kernel(x_ref, gt_ref, oa_ref, ob_ref, f_ref, wf_ref, wo_ref, lng_ref, lnb_ref, o_ref, wob_ref):
    @pl.when(pl.program_id(0) == 0)
    def _():
        wob_ref[...] = wo_ref[...].astype(BF16)

    wf = wf_ref[...].astype(BF16)
    half = OUT_TM // 2
    for r in range(0, OUT_TM, half):
        rows = slice(r, r + half)
        f = jnp.concatenate([f_ref[0, rows, :], f_ref[1, rows, :]], axis=1).astype(BF16)
        oc = _dot(f, wf).astype(BF16)
        cat = jnp.concatenate([oa_ref[rows, :], ob_ref[rows, :], oc], axis=1)
        mo = _dot(cat, wob_ref[...])
        y = ALPHA * x_ref[rows, :] + gt_ref[...] * mo
        o_ref[rows, :] = _layer_norm(y, lng_ref[...], lnb_ref[...])


def _out_project(x, mod, layer, row0, oa, ob, f, w_fourier, w_out, ln_g, ln_b):
    t = x.shape[0]
    return pl.pallas_call(
        _out_kernel,
        out_shape=jax.ShapeDtypeStruct((t, D_MODEL), F32),
        grid=(t // OUT_TM,),
        in_specs=[
            pl.BlockSpec((OUT_TM, D_MODEL), lambda i: (i, 0)),
            _mod_spec(layer, 5, row0, OUT_TM),
            pl.BlockSpec((OUT_TM, 256), lambda i: (i, 0)),
            pl.BlockSpec((OUT_TM, 512), lambda i: (i, 0)),
            pl.BlockSpec((2, OUT_TM, LANES), lambda i: (0, i, 0)),
            pl.BlockSpec((None, C_WIDTH, C_WIDTH), lambda i: (layer, 0, 0)),
            pl.BlockSpec((None, D_MODEL, D_MODEL), lambda i: (layer, 0, 0)),
            pl.BlockSpec((None, None, 1, D_MODEL), lambda i: (layer, 1, 0, 0)),
            pl.BlockSpec((None, None, 1, D_MODEL), lambda i: (layer, 1, 0, 0)),
        ],
        out_specs=pl.BlockSpec((OUT_TM, D_MODEL), lambda i: (i, 0)),
        scratch_shapes=[pltpu.VMEM((D_MODEL, D_MODEL), BF16)],
        compiler_params=_cparams("arbitrary"),
        name="out_project",
    )(x, mod, oa, ob, f, w_fourier, w_out, ln_g, ln_b)


def kernel(x_prompt, x_sample, cache_a_k, cache_a_v, cache_b_k, cache_b_v, c, c_ctx, w_mod, b_mod, w_in, g_qa, g_ka, lam_q1, lam_k1, lam_q2, lam_k2, g_subln, w_fourier, w_out, w_ffn1_gu, w_ffn1_down, w_ffn2_gu, w_ffn2_down, ln_g, ln_b):
    batch, seq, d = x_prompt.shape
    dec_batch, dec_seq, _ = x_sample.shape
    past = cache_a_k.shape[2]
    assert d == D_MODEL and dec_seq == SEG_LEN and batch * seq == SEG_LEN and dec_batch == 2

    cvec = jnp.concatenate([c_ctx[None, :], c, jnp.zeros((MOD_ROWS - 1 - dec_batch, d), F32)], axis=0)
    mod = _modulation(cvec, w_mod, b_mod)

    gq = jnp.tile(g_qa, (1, A_HEADS)).reshape(DEPTH, 1, A_HEADS * HEAD_DIM)
    gk = jnp.tile(g_ka, (1, A_KV_HEADS)).reshape(DEPTH, 1, A_KV_HEADS * HEAD_DIM)
    gs = g_subln.reshape(DEPTH, 1, B_V_DIM)
    lam_params = jnp.stack([lam_q1, lam_k1, lam_q2, lam_k2], axis=1)
    ln_g4 = ln_g.reshape(DEPTH, 3, 1, d)
    ln_b4 = ln_b.reshape(DEPTH, 3, 1, d)
    pmean = _group_mean_matrix(256, HEAD_DIM).astype(BF16)
    rope_tabs = _rope_tables(dec_seq)
    bcs = _channel_dft().astype(BF16)
    cs_ctx = _position_dft_small(seq).astype(BF16)
    m1, m2, tw_cos, tw_sin = _two_stage_dft(GRID_W)
    m1, m2 = m1.astype(BF16), m2.astype(BF16)
    ck_a = cache_a_k.reshape(dec_batch, DEPTH, past, A_KV_HEADS * HEAD_DIM)
    cv_a = cache_a_v.reshape(dec_batch, DEPTH, past, A_KV_HEADS * HEAD_DIM)
    ck_b = cache_b_k.reshape(dec_batch, DEPTH, past, B_HEADS * 2 * B_QK_DIM)
    cv_b = cache_b_v.reshape(dec_batch, DEPTH, past, B_HEADS * B_V_DIM)

    xp = x_prompt.reshape(batch * seq, d)
    xs = x_sample.reshape(dec_batch * dec_seq, d)
    new = []
    for layer in range(DEPTH):
        lam_init = 0.8 - 0.6 * math.exp(-0.3 * layer)
        xp, xs = _ffn(xp, xs, mod, layer, 0, w_ffn1_gu, w_ffn1_down, ln_g4, ln_b4)
        earlier = tuple(new) if layer == DEPTH - 1 else ()
        pp, ucp, *caches = _project(xp, mod, layer, 0, w_in, gq, gk, pmean, None, seq, earlier)
        new = caches if layer == DEPTH - 1 else new + caches
        oa, ob = _context_attention(pp, layer, lam_params, gs, lam_init, seq)
        fo = _context_fourier(ucp, bcs, cs_ctx, seq)
        xp = _out_project(xp, mod, layer, 0, oa, ob, fo, w_fourier, w_out, ln_g4, ln_b4)
        ps, ucs = _project(xs, mod, layer, 1, w_in, gq, gk, pmean, rope_tabs)
        oa = _latent_gqa(ps, ck_a, cv_a, layer, dec_batch, dec_seq)
        ob = _latent_diff(ps, ck_b, cv_b, layer, lam_params, gs, lam_init, dec_batch, dec_seq)
        fo = _latent_fourier(ucs, m1, m2, bcs, tw_cos, tw_sin, dec_seq)
        xs = _out_project(xs, mod, layer, 1, oa, ob, fo, w_fourier, w_out, ln_g4, ln_b4)
        xp, xs = _ffn(xp, xs, mod, layer, 2, w_ffn2_gu, w_ffn2_down, ln_g4, ln_b4)

    new_a_k, new_a_v, new_b_k, new_b_v = new
    return (xp.reshape(batch, seq, d), xs.reshape(dec_batch, dec_seq, d),
            new_a_k.reshape(batch, DEPTH, seq, A_KV_HEADS, HEAD_DIM),
            new_a_v.reshape(batch, DEPTH, seq, A_KV_HEADS, HEAD_DIM),
            new_b_k.reshape(batch, DEPTH, seq, B_HEADS, 2 * B_QK_DIM),
            new_b_v.reshape(batch, DEPTH, seq, B_HEADS, B_V_DIM))
```

```python
import functools
import math

import numpy as np
import jax
import jax.numpy as jnp
from jax import lax
from jax.experimental import pallas as pl
from jax.experimental.pallas import tpu as pltpu

F32 = jnp.float32
BF16 = jnp.bfloat16

D_MODEL = 1024
DEPTH = 2
GRID_W = 64
HEAD_DIM = 64
ROPE_THETA = 10000.0
A_HEADS = 4
A_KV_HEADS = 2
B_HEADS = 4
B_QK_DIM = 64
B_V_DIM = 128
C_GROUPS = 4
C_GROUP_DIM = 64
C_WIDTH = 256
FFN_DIM = 2816
N_MOD = 9
IN_WIDTH = 2304
ALPHA = (2.0 * DEPTH) ** 0.25
LN_EPS = 1e-5
RMS_EPS = 1e-6

LANES = 128
MOD_ROWS = 8
SEG_LEN = 4096
VMEM_LIMIT = 52 << 20

QA_OFF, KA_OFF, VA_OFF, QB_OFF, KB_OFF, VB_OFF = 0, 512, 640, 768, 1280, 1792
PROJ_W = 2304
CACHE_WIDTHS = (128, 128, 512, 512)

FFN_TM = 512
FFN_TF = 256
FFN_STAGES = 3
PROJ_TM = 512
OUT_TM = 1024
ATT_TQ = 256
ATT_UNITS = 8
ATT_SLOTS = 2
ATT_KC = 512
CTX_ATT_SEQS = 4
CTX_FOUR_SEQS = 4
LOG2E = math.log2(math.e)


def _cparams(*sem):
    return pltpu.CompilerParams(dimension_semantics=sem, vmem_limit_bytes=VMEM_LIMIT)


def _rope_tables(length):
    rows = length // GRID_W
    row = np.repeat(np.arange(rows), GRID_W).astype(np.float64)
    col = np.tile(np.arange(GRID_W), rows).astype(np.float64)
    half = HEAD_DIM // 2
    inv = 1.0 / (ROPE_THETA ** (np.arange(0, half, 2, dtype=np.float64) / half))
    ar = row[:, None] * inv
    ac = col[:, None] * inv
    cos = np.concatenate([np.cos(ar), np.cos(ar), np.cos(ac), np.cos(ac)], -1)
    sin = np.concatenate([-np.sin(ar), np.sin(ar), -np.sin(ac), np.sin(ac)], -1)
    return (jnp.asarray(np.tile(cos, (1, 2)), F32), jnp.asarray(np.tile(sin, (1, 2)), F32))


def _group_mean_matrix(width, group):
    m = np.kron(np.eye(width // group), np.full((group, group), 1.0 / group))
    return jnp.asarray(m, F32)


def _channel_dft():
    c = np.arange(C_GROUP_DIM)
    ang = 2.0 * np.pi * np.outer(c, c) / C_GROUP_DIM
    eye = np.eye(C_GROUPS)
    return jnp.asarray(np.concatenate([np.kron(eye, np.cos(ang)), np.kron(eye, np.sin(ang))], 0), F32)


def _position_dft_small(length):
    n = np.arange(length)
    ang = 2.0 * np.pi * ((np.outer(n, n)) % length) / length
    return jnp.asarray(np.concatenate([np.cos(ang), -np.sin(ang)], 1), F32)


def _two_stage_dft(n1):
    i = np.arange(n1)
    ang = 2.0 * np.pi * np.outer(i, i) / n1
    c, s = np.cos(ang), np.sin(ang)
    m1 = np.concatenate([c, -s], 0)
    m2 = np.block([[c, s], [-s, c]])
    tw = 2.0 * np.pi * np.outer(i, i).reshape(-1) / (n1 * n1)
    tc = np.repeat(np.cos(tw)[:, None], LANES, 1)
    ts = np.repeat(np.sin(tw)[:, None], LANES, 1)
    return tuple(jnp.asarray(a, F32) for a in (m1, m2, tc, ts))


def _layer_norm(y, g, b):
    mu = jnp.mean(y, axis=-1, keepdims=True)
    yc = y - mu
    var = jnp.mean(yc * yc, axis=-1, keepdims=True)
    return yc * lax.rsqrt(var + LN_EPS) * g + b


def _dot(a, b):
    return jnp.dot(a, b, preferred_element_type=F32)


def _dot_nt(a, b):
    return lax.dot_general(a, b, (((1,), (1,)), ((), ())), preferred_element_type=F32)


def _dot_split(a, b):
    hi = a.astype(BF16)
    lo = (a - hi.astype(F32)).astype(BF16)
    return _dot(hi, b) + _dot(lo, b)


def _softmax_pv_units(units, s_ref):
    rows = units[0][0].shape[0]
    chunk_lists = []
    for _, pieces in units:
        chunks, off = [], 0
        for k_ref, v_ref in pieces:
            for st in range(0, k_ref.shape[0], ATT_KC):
                sz = min(ATT_KC, k_ref.shape[0] - st)
                chunks.append((k_ref, v_ref, st, sz, off))
                off += sz
        chunk_lists.append(chunks)
    nchunk = len(chunk_lists[0])
    run_max = [None] * len(units)
    row_max = [None] * len(units)
    acc = [None] * len(units)

    def scores(u, ci):
        k_ref, _, st, sz, off = chunk_lists[u][ci]
        s = _dot_nt(units[u][0], k_ref[st:st + sz, :].astype(BF16))
        s_ref[u % ATT_SLOTS, :, off:off + sz] = s
        part = functools.reduce(jnp.maximum, [s[:, j:j + LANES] for j in range(0, sz, LANES)])
        run_max[u] = part if run_max[u] is None else jnp.maximum(run_max[u], part)

    def finish_max(u):
        m = jnp.max(run_max[u], axis=1, keepdims=True)
        row_max[u] = jnp.broadcast_to(m, (rows, LANES))

    def weighted(u, ci):
        _, v_ref, st, sz, off = chunk_lists[u][ci]
        e = jnp.concatenate([jnp.exp2(s_ref[u % ATT_SLOTS, :, off + j:off + j + LANES] - row_max[u])
                             for j in range(0, sz, LANES)], axis=1).astype(BF16)
        v1 = jnp.concatenate([v_ref[st:st + sz, :].astype(BF16), jnp.ones((sz, LANES), BF16)], axis=1)
        d = _dot(e, v1)
        acc[u] = d if acc[u] is None else acc[u] + d

    for ci in range(nchunk):
        scores(0, ci)
    finish_max(0)
    for u in range(1, len(units)):
        for ci in range(nchunk):
            weighted(u - 1, ci)
            scores(u, ci)
        finish_max(u)
    for ci in range(nchunk):
        weighted(len(units) - 1, ci)
    return [a[:, :LANES] / a[:, LANES:] for a in acc]


def _gqa_query(qpad):
    return jnp.concatenate([qpad[:, :LANES], qpad[:, LANES:]], axis=0)


def _gqa_output(o, group):
    tq = o.shape[0] // 2
    o0, o1 = o[:tq], o[tq:]
    low = lax.broadcasted_iota(jnp.int32, (tq, LANES), 1) < HEAD_DIM
    first = group == 0
    r0, r1 = pltpu.roll(o0, HEAD_DIM, 1), pltpu.roll(o1, HEAD_DIM, 1)
    return jnp.where(low, jnp.where(first, o0, r0), jnp.where(first, r1, o1))


def _diff_query(q):
    low = lax.broadcasted_iota(jnp.int32, q.shape, 1) < B_QK_DIM
    zero = jnp.zeros_like(q)
    return jnp.concatenate([jnp.where(low, q, zero), jnp.where(low, zero, q)], axis=0)


def _diff_output(o, lam, g_sub, lam_init):
    tq = o.shape[0] // 2
    w = o[:tq] - lam * o[tq:]
    y = w * lax.rsqrt(jnp.mean(w * w, axis=-1, keepdims=True) + RMS_EPS)
    return y * g_sub * (1.0 - lam_init)


def _lambda(lp):
    s1 = jnp.sum(lp[0:1] * lp[1:2], axis=1, keepdims=True)
    s2 = jnp.sum(lp[2:3] * lp[3:4], axis=1, keepdims=True)
    return jnp.exp(s1) - jnp.exp(s2)


def _mod_kernel(c_ref, w_ref, b_ref, o_ref):
    c = c_ref[...]
    s = (c * jax.nn.sigmoid(c)).astype(BF16)
    o_ref[...] = _dot(s, w_ref[...].astype(BF16)) + b_ref[...]


def _modulation(cvec, w_mod, b_mod):
    out = pl.pallas_call(
        _mod_kernel,
        out_shape=jax.ShapeDtypeStruct((DEPTH, N_MOD, MOD_ROWS, D_MODEL), F32),
        grid=(DEPTH, N_MOD),
        in_specs=[
            pl.BlockSpec((MOD_ROWS, D_MODEL), lambda l, n: (0, 0)),
            pl.BlockSpec((None, D_MODEL, D_MODEL), lambda l, n: (l, 0, n)),
            pl.BlockSpec((None, None, 1, D_MODEL), lambda l, n: (l, n, 0, 0)),
        ],
        out_specs=pl.BlockSpec((None, None, MOD_ROWS, D_MODEL), lambda l, n: (l, n, 0, 0)),
        compiler_params=_cparams("parallel", "parallel"),
        name="modulation",
    )(cvec, w_mod, b_mod.reshape(DEPTH, N_MOD, 1, D_MODEL))
    return out.reshape(DEPTH, N_MOD, MOD_ROWS, 1, D_MODEL)


def _mod_spec(layer, chunk, row0, tm):
    return pl.BlockSpec((None, None, None, 1, D_MODEL),
                        lambda i: (layer, chunk, row0 + (i * tm) // SEG_LEN, 0, 0))


def _ffn_kernel(layer, n_ctx, xp_ref, xs_ref, sh_ref, sc_ref, gt_ref, wgu_hbm, wd_hbm, lng_ref, lnb_ref,
                op_ref, os_ref, wg_ref, wu_ref, wd_ref, stage_gu, stage_d, sem, h_ref, acc_ref):
    nf = FFN_DIM // FFN_TF
    step = pl.program_id(0)
    first_step = step == 0
    is_ctx = step < n_ctx

    def chunk_copies(c, slot):
        col = c * FFN_TF if isinstance(c, int) else pl.multiple_of(c * FFN_TF, FFN_TF)
        return (
            pltpu.make_async_copy(wgu_hbm.at[layer, :, pl.ds(col, FFN_TF)], stage_gu.at[slot, 0], sem.at[slot, 0]),
            pltpu.make_async_copy(wgu_hbm.at[layer, :, pl.ds(FFN_DIM + col, FFN_TF)], stage_gu.at[slot, 1],
                                  sem.at[slot, 1]),
            pltpu.make_async_copy(wd_hbm.at[layer, pl.ds(col, FFN_TF), :], stage_d.at[slot], sem.at[slot, 2]),
        )

    def start_chunk(c, slot):
        for cp in chunk_copies(c, slot):
            cp.start()

    def modulated(x):
        return (x * (1.0 + sc_ref[...]) + sh_ref[...]).astype(BF16)

    def chunk(h, c):
        g = _dot(h, wg_ref[c])
        u = _dot(h, wu_ref[c])
        a = (g * jax.nn.sigmoid(g) * u).astype(BF16)
        return _dot(a, wd_ref[c])

    def finish(x, acc, o_ref):
        y = ALPHA * x + (0.5 * gt_ref[...]) * acc
        o_ref[...] = _layer_norm(y, lng_ref[...], lnb_ref[...])

    def later_step(x_ref, o_ref):
        x = x_ref[...]
        h = modulated(x)
        acc = chunk(h, 0)
        for c in range(1, nf):
            acc = acc + chunk(h, c)
        finish(x, acc, o_ref)

    @pl.when(first_step)
    def _():
        for c in range(FFN_STAGES - 1):
            start_chunk(c, c)
        x = xp_ref[...]
        h_ref[...] = modulated(x)
        acc_ref[...] = jnp.zeros_like(acc_ref)

        def body(c, carry):
            ahead = c + (FFN_STAGES - 1)

            @pl.when(ahead < nf)
            def _():
                start_chunk(ahead, lax.rem(ahead, FFN_STAGES))

            slot = lax.rem(c, FFN_STAGES)
            for cp in chunk_copies(c, slot):
                cp.wait()
            wg_ref[c] = stage_gu[slot, 0].astype(BF16)
            wu_ref[c] = stage_gu[slot, 1].astype(BF16)
            wd_ref[c] = stage_d[slot].astype(BF16)
            acc_ref[...] += chunk(h_ref[...], c)
            return carry

        lax.fori_loop(0, nf, body, 0)
        finish(x, acc_ref[...], op_ref)

    @pl.when(jnp.logical_and(jnp.logical_not(first_step), is_ctx))
    def _():
        later_step(xp_ref, op_ref)

    @pl.when(jnp.logical_not(is_ctx))
    def _():
        later_step(xs_ref, os_ref)


def _ffn(xp, xs, mod, layer, sub, w_gu, w_down, ln_g, ln_b):
    n_ctx = xp.shape[0] // FFN_TM
    n_lat = xs.shape[0] // FFN_TM
    nf = FFN_DIM // FFN_TF
    assert n_ctx >= 1 and n_lat >= 1 and nf >= FFN_STAGES

    def ctx_map(i):
        return (jnp.minimum(i, n_ctx - 1), 0)

    def lat_map(i):
        return (jnp.maximum(i - n_ctx, 0), 0)

    def mod_spec(chunk):
        def index(i):
            row = jnp.where(i < n_ctx, 0, 1 + (jnp.maximum(i - n_ctx, 0) * FFN_TM) // SEG_LEN)
            return (layer, chunk, row, 0, 0)
        return pl.BlockSpec((None, None, None, 1, D_MODEL), index)

    return pl.pallas_call(
        functools.partial(_ffn_kernel, layer, n_ctx),
        out_shape=[jax.ShapeDtypeStruct(xp.shape, F32), jax.ShapeDtypeStruct(xs.shape, F32)],
        grid=(n_ctx + n_lat,),
        in_specs=[
            pl.BlockSpec((FFN_TM, D_MODEL), ctx_map),
            pl.BlockSpec((FFN_TM, D_MODEL), lat_map),
            mod_spec(3 * sub), mod_spec(3 * sub + 1), mod_spec(3 * sub + 2),
            pl.BlockSpec(memory_space=pl.ANY),
            pl.BlockSpec(memory_space=pl.ANY),
            pl.BlockSpec((None, None, 1, D_MODEL), lambda i: (layer, sub, 0, 0)),
            pl.BlockSpec((None, None, 1, D_MODEL), lambda i: (layer, sub, 0, 0)),
        ],
        out_specs=[pl.BlockSpec((FFN_TM, D_MODEL), ctx_map), pl.BlockSpec((FFN_TM, D_MODEL), lat_map)],
        scratch_shapes=[
            pltpu.VMEM((nf, D_MODEL, FFN_TF), BF16),
            pltpu.VMEM((nf, D_MODEL, FFN_TF), BF16),
            pltpu.VMEM((nf, FFN_TF, D_MODEL), BF16),
            pltpu.VMEM((FFN_STAGES, 2, D_MODEL, FFN_TF), F32),
            pltpu.VMEM((FFN_STAGES, FFN_TF, D_MODEL), F32),
            pltpu.SemaphoreType.DMA((FFN_STAGES, 3)),
            pltpu.VMEM((FFN_TM, D_MODEL), BF16),
            pltpu.VMEM((FFN_TM, D_MODEL), F32),
        ],
        compiler_params=_cparams("arbitrary"),
        name="ffn",
    )(xp, xs, mod, mod, mod, w_gu, w_down, ln_g, ln_b)


def _rope(x, cos, sin):
    quarter = HEAD_DIM // 4
    lane = lax.broadcasted_iota(jnp.int32, x.shape, 1)
    first = (lane % (2 * quarter)) < quarter
    rot = jnp.where(first, pltpu.roll(x, LANES - quarter, 1), pltpu.roll(x, quarter, 1))
    return x * cos + rot * sin


def _proj_kernel(rope, n_earlier, x_ref, sh_ref, sc_ref, w_ref, gq_ref, gk_ref, pm_ref, *rest):
    if rope:
        cos_ref, sin_ref, p_ref, uc_ref, wb_ref = rest
    else:
        earlier, rest = rest[:n_earlier], rest[n_earlier:]
        p_ref, uc_ref, nak_ref, nav_ref, nbk_ref, nbv_ref, wb_ref = rest

    @pl.when(pl.program_id(0) == 0)
    def _():
        wb_ref[...] = w_ref[...].astype(BF16)

    h = (x_ref[...] * (1.0 + sc_ref[...]) + sh_ref[...]).astype(BF16)
    proj = _dot(h, wb_ref[...])
    tm = proj.shape[0]

    qa = proj[:, 0:256]
    qa = qa * lax.rsqrt(_dot_split(qa * qa, pm_ref[...]) + RMS_EPS) * gq_ref[...]
    ka = proj[:, 256:384]
    ka = ka * lax.rsqrt(_dot_split(ka * ka, pm_ref[0:LANES, 0:LANES]) + RMS_EPS) * gk_ref[...]
    va = proj[:, 384:512]
    qb = proj[:, 512:1024]
    kb = proj[:, 1024:1536]
    vb = proj[:, 1536:2048]
    uc_ref[0] = proj[:, 2048:2048 + LANES]
    uc_ref[1] = proj[:, 2048 + LANES:2304]

    if not rope:
        for n, (ref, val) in enumerate(((nak_ref, ka), (nav_ref, va), (nbk_ref, kb), (nbv_ref, vb))):
            if n_earlier:
                layers = [e[...] for e in earlier[n::len(CACHE_WIDTHS)]] + [val]
                for l, v in enumerate(layers):
                    ref[:, l] = v.reshape(ref.shape[0], ref.shape[2], ref.shape[3])
            else:
                ref[...] = val

    def slabs(a):
        return [a[:, s:s + LANES] for s in range(0, a.shape[1], LANES)]

    if rope:
        cos, sin = cos_ref[...], sin_ref[...]
        qa_s = [_rope(s, cos, sin) for s in slabs(qa)]
        ka_s = [_rope(ka, cos, sin)]
        qb_s = [_rope(s, cos, sin) for s in slabs(qb)]
        kb_s = [_rope(s, cos, sin) for s in slabs(kb)]
    else:
        qa_s, ka_s, qb_s, kb_s = slabs(qa), [ka], slabs(qb), slabs(kb)

    scale = LOG2E * HEAD_DIM ** -0.5
    low = lax.broadcasted_iota(jnp.int32, (tm, LANES), 1) < HEAD_DIM
    zero = jnp.zeros((tm, LANES), F32)
    g0, g1 = qa_s[0] * scale, qa_s[1] * scale
    qa_pad = [jnp.where(low, g0, zero), jnp.where(low, pltpu.roll(g0, HEAD_DIM, 1), zero),
              jnp.where(low, zero, pltpu.roll(g1, HEAD_DIM, 1)), jnp.where(low, zero, g1)]
    pieces = qa_pad + ka_s + [va] + [s * scale for s in qb_s] + kb_s + slabs(vb)
    for n, piece in enumerate(pieces):
        p_ref[:, n * LANES:(n + 1) * LANES] = piece.astype(BF16)


def _project(x, mod, layer, row0, w_in, gq, gk, pmean, rope_tabs, seq=None, earlier=()):
    t = x.shape[0]
    rope = rope_tabs is not None
    nt = t // PROJ_TM
    in_specs = [
        pl.BlockSpec((PROJ_TM, D_MODEL), lambda i: (i, 0)),
        _mod_spec(layer, 3, row0, PROJ_TM),
        _mod_spec(layer, 4, row0, PROJ_TM),
        pl.BlockSpec((None, D_MODEL, IN_WIDTH), lambda i: (layer, 0, 0)),
        pl.BlockSpec((None, 1, 256), lambda i: (layer, 0, 0)),
        pl.BlockSpec((None, 1, LANES), lambda i: (layer, 0, 0)),
        pl.BlockSpec((256, 256), lambda i: (0, 0)),
    ]
    args = [x, mod, mod, w_in, gq, gk, pmean]
    out_shape = [jax.ShapeDtypeStruct((t, PROJ_W), BF16), jax.ShapeDtypeStruct((2, t, LANES), F32)]
    out_specs = [pl.BlockSpec((PROJ_TM, PROJ_W), lambda i: (i, 0)),
                 pl.BlockSpec((2, PROJ_TM, LANES), lambda i: (0, i, 0))]
    if rope:
        per_seg = SEG_LEN // PROJ_TM
        in_specs += [pl.BlockSpec((PROJ_TM, LANES), lambda i: (i % per_seg, 0))] * 2
        args += list(rope_tabs)
    else:
        for e, width in zip(earlier, CACHE_WIDTHS * (len(earlier) // len(CACHE_WIDTHS))):
            in_specs.append(pl.BlockSpec((PROJ_TM, width), lambda i: (i, 0)))
            args.append(e)
        for width in CACHE_WIDTHS:
            if earlier:
                out_shape.append(jax.ShapeDtypeStruct((t // seq, DEPTH, seq, width), F32))
                out_specs.append(pl.BlockSpec((PROJ_TM // seq, DEPTH, seq, width), lambda i: (i, 0, 0, 0)))
            else:
                out_shape.append(jax.ShapeDtypeStruct((t, width), F32))
                out_specs.append(pl.BlockSpec((PROJ_TM, width), lambda i: (i, 0)))
    return pl.pallas_call(
        functools.partial(_proj_kernel, rope, len(earlier)),
        out_shape=out_shape,
        grid=(nt,),
        in_specs=in_specs,
        out_specs=out_specs,
        scratch_shapes=[pltpu.VMEM((D_MODEL, IN_WIDTH), BF16)],
        compiler_params=_cparams("arbitrary"),
        name="project_latent" if rope else "project_context",
    )(*args)


def _ctx_attn_kernel(lam_init, seq, p_ref, lp_ref, gs_ref, oa_ref, ob_ref, s_ref):
    units = []
    for r in range(0, p_ref.shape[0], seq):
        rows = slice(r, r + seq)
        ka = p_ref.at[rows, KA_OFF:KA_OFF + LANES]
        va = p_ref.at[rows, VA_OFF:VA_OFF + LANES]
        for g in range(A_KV_HEADS):
            units.append((_gqa_query(p_ref[rows, QA_OFF + 256 * g:QA_OFF + 256 * (g + 1)]), [(ka, va)]))
        for h in range(B_HEADS):
            k = p_ref.at[rows, KB_OFF + h * LANES:KB_OFF + (h + 1) * LANES]
            v = p_ref.at[rows, VB_OFF + h * LANES:VB_OFF + (h + 1) * LANES]
            units.append((_diff_query(p_ref[rows, QB_OFF + h * LANES:QB_OFF + (h + 1) * LANES]), [(k, v)]))
    outs = _softmax_pv_units(units, s_ref)
    lam = _lambda(lp_ref[...]) + lam_init
    per_seq = A_KV_HEADS + B_HEADS
    for n, r in enumerate(range(0, p_ref.shape[0], seq)):
        rows = slice(r, r + seq)
        for g in range(A_KV_HEADS):
            oa_ref[rows, g * LANES:(g + 1) * LANES] = _gqa_output(outs[n * per_seq + g], g).astype(BF16)
        for h in range(B_HEADS):
            o = _diff_output(outs[n * per_seq + A_KV_HEADS + h], lam, gs_ref[...], lam_init)
            ob_ref[rows, h * LANES:(h + 1) * LANES] = o.astype(BF16)


def _context_attention(p, layer, lam_params, g_subln, lam_init, seq):
    t = p.shape[0]
    tile = CTX_ATT_SEQS * seq
    return pl.pallas_call(
        functools.partial(_ctx_attn_kernel, lam_init, seq),
        out_shape=[jax.ShapeDtypeStruct((t, 256), BF16), jax.ShapeDtypeStruct((t, 512), BF16)],
        grid=(t // tile,),
        in_specs=[
            pl.BlockSpec((tile, PROJ_W), lambda i: (i, 0)),
            pl.BlockSpec((None, 4, B_QK_DIM), lambda i: (layer, 0, 0)),
            pl.BlockSpec((None, 1, B_V_DIM), lambda i: (layer, 0, 0)),
        ],
        out_specs=[pl.BlockSpec((tile, 256), lambda i: (i, 0)), pl.BlockSpec((tile, 512), lambda i: (i, 0))],
        scratch_shapes=[pltpu.VMEM((ATT_SLOTS, 2 * seq, seq), F32)],
        compiler_params=_cparams("parallel"),
        name="context_attention",
    )(p, lam_params, g_subln)


def _lat_gqa_kernel(q_ref, kn_ref, vn_ref, kc_ref, vc_ref, o_ref, s_ref):
    pieces = [(kc_ref, vc_ref), (kn_ref, vn_ref)]
    units = [(_gqa_query(q_ref[u * ATT_TQ:(u + 1) * ATT_TQ, :]), pieces) for u in range(ATT_UNITS)]
    outs = _softmax_pv_units(units, s_ref)
    for u in range(ATT_UNITS):
        o_ref[u * ATT_TQ:(u + 1) * ATT_TQ, :] = _gqa_output(outs[u], pl.program_id(1)).astype(BF16)


def _latent_gqa(p, cache_k, cache_v, layer, batch, length):
    tile = ATT_TQ * ATT_UNITS
    nq = length // tile
    past = cache_k.shape[2]
    return pl.pallas_call(
        _lat_gqa_kernel,
        out_shape=jax.ShapeDtypeStruct((batch * length, 256), BF16),
        grid=(batch, A_KV_HEADS, nq),
        in_specs=[
            pl.BlockSpec((tile, 256), lambda b, g, i: (b * nq + i, QA_OFF // 256 + g)),
            pl.BlockSpec((length, LANES), lambda b, g, i: (b, KA_OFF // LANES)),
            pl.BlockSpec((length, LANES), lambda b, g, i: (b, VA_OFF // LANES)),
            pl.BlockSpec((None, None, past, LANES), lambda b, g, i: (b, layer, 0, 0)),
            pl.BlockSpec((None, None, past, LANES), lambda b, g, i: (b, layer, 0, 0)),
        ],
        out_specs=pl.BlockSpec((tile, LANES), lambda b, g, i: (b * nq + i, g)),
        scratch_shapes=[pltpu.VMEM((ATT_SLOTS, 2 * ATT_TQ, past + length), F32)],
        compiler_params=_cparams("parallel", "parallel", "parallel"),
        name="latent_gqa",
    )(p, p, p, cache_k, cache_v)


def _lat_diff_kernel(lam_init, q_ref, kn_ref, vn_ref, kc_ref, vc_ref, lp_ref, gs_ref, o_ref, s_ref):
    pieces = [(kc_ref, vc_ref), (kn_ref, vn_ref)]
    units = [(_diff_query(q_ref[u * ATT_TQ:(u + 1) * ATT_TQ, :]), pieces) for u in range(ATT_UNITS)]
    outs = _softmax_pv_units(units, s_ref)
    lam = _lambda(lp_ref[...]) + lam_init
    for u in range(ATT_UNITS):
        o = _diff_output(outs[u], lam, gs_ref[...], lam_init)
        o_ref[u * ATT_TQ:(u + 1) * ATT_TQ, :] = o.astype(BF16)


def _latent_diff(p, cache_k, cache_v, layer, lam_params, g_subln, lam_init, batch, length):
    tile = ATT_TQ * ATT_UNITS
    nq = length // tile
    past = cache_k.shape[2]
    return pl.pallas_call(
        functools.partial(_lat_diff_kernel, lam_init),
        out_shape=jax.ShapeDtypeStruct((batch * length, 512), BF16),
        grid=(batch, B_HEADS, nq),
        in_specs=[
            pl.BlockSpec((tile, LANES), lambda b, h, i: (b * nq + i, QB_OFF // LANES + h)),
            pl.BlockSpec((length, LANES), lambda b, h, i: (b, KB_OFF // LANES + h)),
            pl.BlockSpec((length, LANES), lambda b, h, i: (b, VB_OFF // LANES + h)),
            pl.BlockSpec((None, None, past, LANES), lambda b, h, i: (b, layer, 0, h)),
            pl.BlockSpec((None, None, past, LANES), lambda b, h, i: (b, layer, 0, h)),
            pl.BlockSpec((None, 4, B_QK_DIM), lambda b, h, i: (layer, 0, 0)),
            pl.BlockSpec((None, 1, B_V_DIM), lambda b, h, i: (layer, 0, 0)),
        ],
        out_specs=pl.BlockSpec((tile, LANES), lambda b, h, i: (b * nq + i, h)),
        scratch_shapes=[pltpu.VMEM((ATT_SLOTS, 2 * ATT_TQ, past + length), F32)],
        compiler_params=_cparams("parallel", "parallel", "parallel"),
        name="latent_diff",
    )(p, p, p, cache_k, cache_v, lam_params, g_subln)


def _ctx_fourier_kernel(norm, seq, u_ref, bcs_ref, cs_ref, o_ref):
    bcs = bcs_ref[...]
    for r in range(0, u_ref.shape[1], seq):
        rows = slice(r, r + seq)
        u = jnp.concatenate([u_ref[0, rows, :], u_ref[1, rows, :]], axis=1).astype(BF16)
        rhs = jnp.concatenate([_dot(u, bcs[:C_WIDTH]), _dot(u, bcs[C_WIDTH:])], axis=0).astype(BF16)
        f = _dot(cs_ref[...], rhs) * norm
        o_ref[0, rows, :] = f[:, :LANES]
        o_ref[1, rows, :] = f[:, LANES:]


def _context_fourier(uc, bcs, cs, seq):
    t = uc.shape[1]
    tile = CTX_FOUR_SEQS * seq
    norm = 1.0 / math.sqrt(seq * C_GROUP_DIM)
    return pl.pallas_call(
        functools.partial(_ctx_fourier_kernel, norm, seq),
        out_shape=jax.ShapeDtypeStruct((2, t, LANES), F32),
        grid=(t // tile,),
        in_specs=[
            pl.BlockSpec((2, tile, LANES), lambda i: (0, i, 0)),
            pl.BlockSpec((2 * C_WIDTH, C_WIDTH), lambda i: (0, 0)),
            pl.BlockSpec((seq, 2 * seq), lambda i: (0, 0)),
        ],
        out_specs=pl.BlockSpec((2, tile, LANES), lambda i: (0, i, 0)),
        compiler_params=_cparams("parallel"),
        name="context_fourier",
    )(uc, bcs, cs)


def _lat_fourier_kernel(norm, u_ref, m1_ref, m2_ref, bcs_ref, tc_ref, ts_ref, o_ref, z_ref):
    n1 = GRID_W

    def strided(ref, plane, start):
        rows = pl.ds(start, n1, stride=n1)
        return jnp.concatenate([ref[(*plane, 0, rows, slice(None))], ref[(*plane, 1, rows, slice(None))]], axis=1)

    m1 = m1_ref[...]
    for b in range(n1):
        ub = strided(u_ref, (), b).astype(BF16)
        y = _dot(m1, ub)
        yr, yi = y[:n1], y[n1:]
        tc = jnp.concatenate([tc_ref[b * n1:(b + 1) * n1, :]] * 2, axis=1)
        ts = jnp.concatenate([ts_ref[b * n1:(b + 1) * n1, :]] * 2, axis=1)
        zr = yr * tc + yi * ts
        zi = yi * tc - yr * ts
        for half in range(2):
            z_ref[0, half, b * n1:(b + 1) * n1, :] = zr[:, half * LANES:(half + 1) * LANES]
            z_ref[1, half, b * n1:(b + 1) * n1, :] = zi[:, half * LANES:(half + 1) * LANES]
    m2 = m2_ref[...]
    bcs = bcs_ref[...]
    for q in range(n1):
        z = jnp.concatenate([strided(z_ref, (0,), q), strided(z_ref, (1,), q)], axis=0).astype(BF16)
        g = _dot(m2, z)
        gcat = jnp.concatenate([g[:n1], g[n1:]], axis=1).astype(BF16)
        f = _dot(gcat, bcs) * norm
        o_ref[0, pl.ds(q, n1, stride=n1), :] = f[:, :LANES]
        o_ref[1, pl.ds(q, n1, stride=n1), :] = f[:, LANES:]


def _latent_fourier(uc, m1, m2, bcs, tc, ts, length):
    t = uc.shape[1]
    norm = 1.0 / math.sqrt(length * C_GROUP_DIM)
    return pl.pallas_call(
        functools.partial(_lat_fourier_kernel, norm),
        out_shape=jax.ShapeDtypeStruct((2, t, LANES), F32),
        grid=(t // length,),
        in_specs=[
            pl.BlockSpec((2, length, LANES), lambda i: (0, i, 0)),
            pl.BlockSpec((2 * GRID_W, GRID_W), lambda i: (0, 0)),
            pl.BlockSpec((2 * GRID_W, 2 * GRID_W), lambda i: (0, 0)),
            pl.BlockSpec((2 * C_WIDTH, C_WIDTH), lambda i: (0, 0)),
            pl.BlockSpec((length, LANES), lambda i: (0, 0)),
            pl.BlockSpec((length, LANES), lambda i: (0, 0)),
        ],
        out_specs=pl.BlockSpec((2, length, LANES), lambda i: (0, i, 0)),
        scratch_shapes=[pltpu.VMEM((2, 2, length, LANES), F32)],
        compiler_params=_cparams("parallel"),
        name="latent_fourier",
    )(uc, m1, m2, bcs, tc, ts)


def _out_kernel(x_ref, gt_ref, oa_ref, ob_ref, f_ref, wf_ref, wo_ref, lng_ref, lnb_ref, o_ref, wob_ref):
    @pl.when(pl.program_id(0) == 0)
    def _():
        wob_ref[...] = wo_ref[...].astype(BF16)

    wf = wf_ref[...].astype(BF16)
    half = OUT_TM // 2
    for r in range(0, OUT_TM, half):
        rows = slice(r, r + half)
        f = jnp.concatenate([f_ref[0, rows, :], f_ref[1, rows, :]], axis=1).astype(BF16)
        oc = _dot(f, wf).astype(BF16)
        cat = jnp.concatenate([oa_ref[rows, :], ob_ref[rows, :], oc], axis=1)
        mo = _dot(cat, wob_ref[...])
        y = ALPHA * x_ref[rows, :] + gt_ref[...] * mo
        o_ref[rows, :] = _layer_norm(y, lng_ref[...], lnb_ref[...])


def _out_project(x, mod, layer, row0, oa, ob, f, w_fourier, w_out, ln_g, ln_b):
    t = x.shape[0]
    return pl.pallas_call(
        _out_kernel,
        out_shape=jax.ShapeDtypeStruct((t, D_MODEL), F32),
        grid=(t // OUT_TM,),
        in_specs=[
            pl.BlockSpec((OUT_TM, D_MODEL), lambda i: (i, 0)),
            _mod_spec(layer, 5, row0, OUT_TM),
            pl.BlockSpec((OUT_TM, 256), lambda i: (i, 0)),
            pl.BlockSpec((OUT_TM, 512), lambda i: (i, 0)),
            pl.BlockSpec((2, OUT_TM, LANES), lambda i: (0, i, 0)),
            pl.BlockSpec((None, C_WIDTH, C_WIDTH), lambda i: (layer, 0, 0)),
            pl.BlockSpec((None, D_MODEL, D_MODEL), lambda i: (layer, 0, 0)),
            pl.BlockSpec((None, None, 1, D_MODEL), lambda i: (layer, 1, 0, 0)),
            pl.BlockSpec((None, None, 1, D_MODEL), lambda i: (layer, 1, 0, 0)),
        ],
        out_specs=pl.BlockSpec((OUT_TM, D_MODEL), lambda i: (i, 0)),
        scratch_shapes=[pltpu.VMEM((D_MODEL, D_MODEL), BF16)],
        compiler_params=_cparams("arbitrary"),
        name="out_project",
    )(x, mod, oa, ob, f, w_fourier, w_out, ln_g, ln_b)


def kernel(x_prompt, x_sample, cache_a_k, cache_a_v, cache_b_k, cache_b_v, c, c_ctx, w_mod, b_mod, w_in, g_qa, g_ka, lam_q1, lam_k1, lam_q2, lam_k2, g_subln, w_fourier, w_out, w_ffn1_gu, w_ffn1_down, w_ffn2_gu, w_ffn2_down, ln_g, ln_b):
    batch, seq, d = x_prompt.shape
    dec_batch, dec_seq, _ = x_sample.shape
    past = cache_a_k.shape[2]
    assert d == D_MODEL and dec_seq == SEG_LEN and batch * seq == SEG_LEN and dec_batch == 2

    cvec = jnp.concatenate([c_ctx[None, :], c, jnp.zeros((MOD_ROWS - 1 - dec_batch, d), F32)], axis=0)
    mod = _modulation(cvec, w_mod, b_mod)

    gq = jnp.tile(g_qa, (1, A_HEADS)).reshape(DEPTH, 1, A_HEADS * HEAD_DIM)
    gk = jnp.tile(g_ka, (1, A_KV_HEADS)).reshape(DEPTH, 1, A_KV_HEADS * HEAD_DIM)
    gs = g_subln.reshape(DEPTH, 1, B_V_DIM)
    lam_params = jnp.stack([lam_q1, lam_k1, lam_q2, lam_k2], axis=1)
    ln_g4 = ln_g.reshape(DEPTH, 3, 1, d)
    ln_b4 = ln_b.reshape(DEPTH, 3, 1, d)
    pmean = _group_mean_matrix(256, HEAD_DIM).astype(BF16)
    rope_tabs = _rope_tables(dec_seq)
    bcs = _channel_dft().astype(BF16)
    cs_ctx = _position_dft_small(seq).astype(BF16)
    m1, m2, tw_cos, tw_sin = _two_stage_dft(GRID_W)
    m1, m2 = m1.astype(BF16), m2.astype(BF16)
    ck_a = cache_a_k.reshape(dec_batch, DEPTH, past, A_KV_HEADS * HEAD_DIM)
    cv_a = cache_a_v.reshape(dec_batch, DEPTH, past, A_KV_HEADS * HEAD_DIM)
    ck_b = cache_b_k.reshape(dec_batch, DEPTH, past, B_HEADS * 2 * B_QK_DIM)
    cv_b = cache_b_v.reshape(dec_batch, DEPTH, past, B_HEADS * B_V_DIM)

    xp = x_prompt.reshape(batch * seq, d)
    xs = x_sample.reshape(dec_batch * dec_seq, d)
    new = []
    for layer in range(DEPTH):
        lam_init = 0.8 - 0.6 * math.exp(-0.3 * layer)
        xp, xs = _ffn(xp, xs, mod, layer, 0, w_ffn1_gu, w_ffn1_down, ln_g4, ln_b4)
        earlier = tuple(new) if layer == DEPTH - 1 else ()
        pp, ucp, *caches = _project(xp, mod, layer, 0, w_in, gq, gk, pmean, None, seq, earlier)
        new = caches if layer == DEPTH - 1 else new + caches
        oa, ob = _context_attention(pp, layer, lam_params, gs, lam_init, seq)
        fo = _context_fourier(ucp, bcs, cs_ctx, seq)
        xp = _out_project(xp, mod, layer, 0, oa, ob, fo, w_fourier, w_out, ln_g4, ln_b4)
        ps, ucs = _project(xs, mod, layer, 1, w_in, gq, gk, pmean, rope_tabs)
        oa = _latent_gqa(ps, ck_a, cv_a, layer, dec_batch, dec_seq)
        ob = _latent_diff(ps, ck_b, cv_b, layer, lam_params, gs, lam_init, dec_batch, dec_seq)
        fo = _latent_fourier(ucs, m1, m2, bcs, tw_cos, tw_sin, dec_seq)
        xs = _out_project(xs, mod, layer, 1, oa, ob, fo, w_fourier, w_out, ln_g4, ln_b4)
        xp, xs = _ffn(xp, xs, mod, layer, 2, w_ffn2_gu, w_ffn2_down, ln_g4, ln_b4)

    new_a_k, new_a_v, new_b_k, new_b_v = new
    return (xp.reshape(batch, seq, d), xs.reshape(dec_batch, dec_seq, d),
            new_a_k.reshape(batch, DEPTH, seq, A_KV_HEADS, HEAD_DIM),
            new_a_v.reshape(batch, DEPTH, seq, A_KV_HEADS, HEAD_DIM),
            new_b_k.reshape(batch, DEPTH, seq, B_HEADS, 2 * B_QK_DIM),
            new_b_v.reshape(batch, DEPTH, seq, B_HEADS, B_V_DIM))
```
